```python
import jax, jax.numpy as jnp
from jax import lax
import numpy as np

D_MODEL = 1024
BATCH = 4
SEQ = 8192
DEPTH = 2

GRID_W = 64
CTX_LEN = 256
EPS = 1e-6

D_CONV = 256
CONV_GROUPS = 4
CONV_W = 3
D_SGU = 256
SGU_HEADS = 4
SGU_HEAD_DIM = D_SGU // SGU_HEADS
SGU_CHUNK = 128
N_HEADS = 8
N_KV_HEADS = 2
HEAD_DIM = 64
Q_PER_KV = N_HEADS // N_KV_HEADS
D_ATTN = N_HEADS * HEAD_DIM
WINDOW = 128
BLOCK = 128
ROPE_THETA = 10000.0
ROPE_AXIS_DIM = HEAD_DIM // 2
ROPE_FREQS = ROPE_AXIS_DIM // 2
D_MIX = D_CONV + D_SGU + D_ATTN

CONV_END = 3 * D_CONV
SGU_END = CONV_END + 2 * D_SGU
Q_END = SGU_END + D_ATTN
K_END = Q_END + N_KV_HEADS * HEAD_DIM
D_IN = K_END + N_KV_HEADS * HEAD_DIM

N_KEYS = 128
N_EXPERTS = N_KEYS * N_KEYS
PEER_HEADS = 8
PEER_TOPK = 16
PEER_DQ = 256
PEER_DHALF = PEER_DQ // 2
EXPERT_CHUNK = 128

kernel_name = "hybrid_parallel_heads_peer_dit"


def rmsnorm(x, g):
    xf = x.astype(jnp.float32)
    y = xf * lax.rsqrt(jnp.mean(xf * xf, axis=-1, keepdims=True) + EPS)
    return (y * g.astype(jnp.float32)).astype(x.dtype)


def rms_plain(x):
    xf = x.astype(jnp.float32)
    return (xf * lax.rsqrt(jnp.mean(xf * xf, axis=-1, keepdims=True) + EPS)).astype(x.dtype)


def layernorm(x, g):
    xf = x.astype(jnp.float32)
    mu = jnp.mean(xf, axis=-1, keepdims=True)
    var = jnp.mean(jnp.square(xf - mu), axis=-1, keepdims=True)
    return ((xf - mu) * lax.rsqrt(var + EPS) * g.astype(jnp.float32)).astype(x.dtype)


def axial_rope_tables(L):
    rows = L // GRID_W
    row = jnp.repeat(jnp.arange(rows), GRID_W).astype(jnp.float32)
    col = jnp.tile(jnp.arange(GRID_W), rows).astype(jnp.float32)
    inv = ROPE_THETA ** (-jnp.arange(ROPE_FREQS, dtype=jnp.float32) / ROPE_FREQS)
    ar = row[:, None] * inv[None, :]
    ac = col[:, None] * inv[None, :]
    ang = jnp.concatenate([ar, ar, ac, ac], axis=-1)
    return jnp.cos(ang), jnp.sin(ang)


def apply_rope(x, cos, sin):
    f = ROPE_FREQS
    xf = x.astype(jnp.float32)
    x1, x2, x3, x4 = xf[..., :f], xf[..., f:2 * f], xf[..., 2 * f:3 * f], xf[..., 3 * f:]
    rot = jnp.concatenate([-x2, x1, -x4, x3], axis=-1)
    return (xf * cos[None, :, None, :] + rot * sin[None, :, None, :]).astype(x.dtype)


def short_conv(p, w):
    b_gate, c_gate, xin = jnp.split(p, 3, axis=-1)
    z = c_gate * xin
    zp = jnp.pad(z, ((0, 0), (1, 1), (0, 0)))
    y = zp[:, :-2] * w[0] + zp[:, 1:-1] * w[1] + zp[:, 2:] * w[2]
    return b_gate * y


def spatial_gating(p, g_norm, w_s, b_s):
    B_, S, _ = p.shape
    z = jax.nn.gelu(p)
    u, v = jnp.split(z, 2, axis=-1)
    v = layernorm(v, g_norm)
    v = v.reshape(B_, S // SGU_CHUNK, SGU_CHUNK, SGU_HEADS, SGU_HEAD_DIM)
    s = jnp.einsum('hpq,bnqhc->bnphc', w_s, v) + b_s.T[None, None, :, :, None]
    return u * s.reshape(B_, S, D_SGU)


def windowed_attention(q, k, v, kc, vc, sink):
    B_, L = q.shape[:2]
    nb = L // BLOCK
    scale = HEAD_DIM ** -0.5
    qb = q.reshape(B_, nb, BLOCK, N_KV_HEADS, Q_PER_KV, HEAD_DIM)
    pad = ((0, 0), (BLOCK, BLOCK), (0, 0), (0, 0))
    kp = jnp.pad(k, pad).reshape(B_, nb + 2, BLOCK, N_KV_HEADS, HEAD_DIM)
    vp = jnp.pad(v, pad).reshape(B_, nb + 2, BLOCK, N_KV_HEADS, HEAD_DIM)
    kb = jnp.concatenate([kp[:, :-2], kp[:, 1:-1], kp[:, 2:]], axis=2)
    vb = jnp.concatenate([vp[:, :-2], vp[:, 1:-1], vp[:, 2:]], axis=2)
    s_loc = jnp.einsum('bnqhgd,bnkhd->bnhgqk', qb, kb).astype(jnp.float32) * scale
    s_ctx = jnp.einsum('bnqhgd,bkhd->bnhgqk', qb, kc).astype(jnp.float32) * scale
    qi = jnp.arange(BLOCK)[:, None]
    ki = jnp.arange(3 * BLOCK)[None, :]
    band = jnp.abs(ki - BLOCK - qi) <= WINDOW
    kblock = jnp.arange(nb)[:, None] + ki // BLOCK - 1
    in_range = (kblock >= 0) & (kblock < nb)
    mask = band[None] & in_range[:, None, :]
    s_loc = jnp.where(mask[None, :, None, None], s_loc, -jnp.inf)
    sk = sink.astype(jnp.float32).reshape(N_KV_HEADS, Q_PER_KV)[None, None, :, :, None]
    m = jnp.maximum(jnp.maximum(jnp.max(s_loc, axis=-1), jnp.max(s_ctx, axis=-1)), sk)
    p_loc = jnp.exp(s_loc - m[..., None])
    p_ctx = jnp.exp(s_ctx - m[..., None])
    denom = jnp.sum(p_loc, axis=-1) + jnp.sum(p_ctx, axis=-1) + jnp.exp(sk - m)
    o = (jnp.einsum('bnhgqk,bnkhd->bnhgqd', p_loc, vb.astype(jnp.float32))
         + jnp.einsum('bnhgqk,bkhd->bnhgqd', p_ctx, vc.astype(jnp.float32))) / denom[..., None]
    return o.transpose(0, 1, 4, 2, 3, 5).reshape(B_, L, D_ATTN).astype(q.dtype)


def context_attention(qc, kc, vc, sink):
    B_, C = qc.shape[:2]
    scale = HEAD_DIM ** -0.5
    qg = qc.reshape(B_, C, N_KV_HEADS, Q_PER_KV, HEAD_DIM)
    s = jnp.einsum('bqhgd,bkhd->bhgqk', qg, kc).astype(jnp.float32) * scale
    sk = sink.astype(jnp.float32).reshape(N_KV_HEADS, Q_PER_KV)[None, :, :, None]
    m = jnp.maximum(jnp.max(s, axis=-1), sk)
    p = jnp.exp(s - m[..., None])
    denom = jnp.sum(p, axis=-1) + jnp.exp(sk - m)
    o = jnp.einsum('bhgqk,bkhd->bhgqd', p, vc.astype(jnp.float32)) / denom[..., None]
    return o.transpose(0, 3, 1, 2, 4).reshape(B_, C, D_ATTN).astype(qc.dtype)


def merge_groups(y_conv, y_sgu, y_attn, g, w_out):
    y = jnp.concatenate([rms_plain(y_conv), rms_plain(y_sgu), rms_plain(y_attn)], axis=-1)
    return (y * g) @ w_out


def token_mixers(hl, hc, w_in, conv_w, sgu_norm_g, sgu_w, sgu_b, sink, mix_norm_g, w_out,
                 cos, sin, need_ctx):
    B_, L, _ = hl.shape
    C = hc.shape[1]
    pl = hl @ w_in
    conv_l = short_conv(pl[..., :CONV_END], conv_w)
    sgu_l = spatial_gating(pl[..., CONV_END:SGU_END], sgu_norm_g, sgu_w, sgu_b)
    q = apply_rope(pl[..., SGU_END:Q_END].reshape(B_, L, N_HEADS, HEAD_DIM), cos, sin)
    k = apply_rope(pl[..., Q_END:K_END].reshape(B_, L, N_KV_HEADS, HEAD_DIM), cos, sin)
    v = pl[..., K_END:].reshape(B_, L, N_KV_HEADS, HEAD_DIM)
    if need_ctx:
        pc = hc @ w_in
        pkv = pc[..., Q_END:]
    else:
        pkv = hc @ w_in[:, Q_END:]
    kc = pkv[..., :N_KV_HEADS * HEAD_DIM].reshape(B_, C, N_KV_HEADS, HEAD_DIM)
    vc = pkv[..., N_KV_HEADS * HEAD_DIM:].reshape(B_, C, N_KV_HEADS, HEAD_DIM)
    attn_l = windowed_attention(q, k, v, kc, vc, sink)
    yl = merge_groups(conv_l, sgu_l, attn_l, mix_norm_g, w_out)
    if not need_ctx:
        return yl, None
    conv_c = short_conv(pc[..., :CONV_END], conv_w)
    sgu_c = spatial_gating(pc[..., CONV_END:SGU_END], sgu_norm_g, sgu_w, sgu_b)
    qc = pc[..., SGU_END:Q_END].reshape(B_, C, N_HEADS, HEAD_DIM)
    attn_c = context_attention(qc, kc, vc, sink)
    yc = merge_groups(conv_c, sgu_c, attn_c, mix_norm_g, w_out)
    return yl, yc


def peer(h, wq, keys, u_tab, v_tab):
    B_, S, D = h.shape
    T = B_ * S
    t = h.reshape(T, D)
    q = (t @ wq).reshape(T, PEER_HEADS, 2, PEER_DHALF)
    s = jnp.einsum('thpd,hpkd->thpk', q, keys).astype(jnp.float32)
    sv, si = lax.top_k(s, PEER_TOPK)
    cand = (sv[..., 0, :, None] + sv[..., 1, None, :]).reshape(T, PEER_HEADS, PEER_TOPK * PEER_TOPK)
    cidx = (si[..., 0, :, None] * N_KEYS + si[..., 1, None, :]).reshape(T, PEER_HEADS, PEER_TOPK * PEER_TOPK)
    fv, fi = lax.top_k(cand, PEER_TOPK)
    experts = jnp.take_along_axis(cidx, fi, axis=-1)
    gates = jax.nn.softmax(fv, axis=-1).astype(h.dtype)
    nc = T // EXPERT_CHUNK
    experts = experts.reshape(nc, EXPERT_CHUNK, PEER_HEADS * PEER_TOPK)
    gates = gates.reshape(nc, EXPERT_CHUNK, PEER_HEADS * PEER_TOPK)
    tc = t.reshape(nc, EXPERT_CHUNK, D)

    def run(args):
        xc, ec, gc = args
        a = jax.nn.gelu(jnp.einsum('tkd,td->tk', u_tab[ec], xc))
        return jnp.einsum('tk,tkd->td', a * gc, v_tab[ec])

    out = lax.map(run, (tc, experts, gates))
    return out.reshape(B_, S, D)


def setup_inputs(seed: int = 0) -> dict:
    key = jax.random.key(seed)
    ks = jax.random.split(key, 24)
    n = jax.random.normal
    f32 = jnp.float32
    return {
        "x": n(ks[0], (BATCH, SEQ, D_MODEL), f32),
        "c": n(ks[1], (BATCH, D_MODEL), f32),
        "ctx": n(ks[2], (BATCH, CTX_LEN, D_MODEL), f32),
        "c_ctx": n(ks[3], (D_MODEL,), f32),
        "w_ada": n(ks[4], (DEPTH, D_MODEL, 6 * D_MODEL), f32) * (0.5 * D_MODEL ** -0.5),
        "b_ada": n(ks[5], (DEPTH, 6 * D_MODEL), f32) * 0.02,
        "norm1_g": 1.0 + 0.05 * n(ks[6], (DEPTH, D_MODEL), f32),
        "norm2_g": 1.0 + 0.05 * n(ks[7], (DEPTH, D_MODEL), f32),
        "w_in": n(ks[8], (DEPTH, D_MODEL, D_IN), f32) * D_MODEL ** -0.5,
        "conv_w": n(ks[9], (DEPTH, CONV_W, D_CONV), f32) * CONV_W ** -0.5,
        "sgu_norm_g": 1.0 + 0.05 * n(ks[10], (DEPTH, D_SGU), f32),
        "sgu_w": n(ks[11], (DEPTH, SGU_HEADS, SGU_CHUNK, SGU_CHUNK), f32) * SGU_CHUNK ** -0.5,
        "sgu_b": 1.0 + 0.1 * n(ks[12], (DEPTH, SGU_HEADS, SGU_CHUNK), f32),
        "attn_sink": 0.5 * n(ks[13], (DEPTH, N_HEADS), f32),
        "mix_norm_g": 1.0 + 0.05 * n(ks[14], (DEPTH, D_MIX), f32),
        "w_out": n(ks[15], (DEPTH, D_MIX, D_MODEL), f32) * D_MIX ** -0.5,
        "peer_wq": n(ks[16], (DEPTH, D_MODEL, PEER_HEADS * PEER_DQ), f32) * D_MODEL ** -0.5,
        "peer_keys": n(ks[17], (DEPTH, PEER_HEADS, 2, N_KEYS, PEER_DHALF), f32) * PEER_DHALF ** -0.5,
        "peer_u": n(ks[18], (DEPTH, N_EXPERTS, D_MODEL), f32) * D_MODEL ** -0.5,
        "peer_v": n(ks[19], (DEPTH, N_EXPERTS, D_MODEL), f32) * 0.5,
        "final_g": 1.0 + 0.05 * n(ks[20], (D_MODEL,), f32),
    }


def reference(x, c, ctx, c_ctx, w_ada, b_ada, norm1_g, norm2_g, w_in, conv_w, sgu_norm_g, sgu_w,
              sgu_b, attn_sink, mix_norm_g, w_out, peer_wq, peer_keys, peer_u, peer_v, final_g):
    L = x.shape[1]
    C = ctx.shape[1]
    cos, sin = axial_rope_tables(L)
    sc = jax.nn.silu(c)
    scc = jax.nn.silu(c_ctx)
    xl, xc = x, ctx
    for i in range(DEPTH):
        last = i == DEPTH - 1
        mod_l = (sc @ w_ada[i] + b_ada[i])[:, None, :]
        mod_c = scc @ w_ada[i] + b_ada[i]
        sh1, s1, g1, sh2, s2, g2 = jnp.split(mod_l, 6, axis=-1)
        csh1, cs1, cg1, csh2, cs2, cg2 = jnp.split(mod_c, 6, axis=-1)
        hl = rmsnorm(xl, norm1_g[i]) * (1.0 + s1) + sh1
        hc = rmsnorm(xc, norm1_g[i]) * (1.0 + cs1) + csh1
        yl, yc = token_mixers(hl, hc, w_in[i], conv_w[i], sgu_norm_g[i], sgu_w[i], sgu_b[i],
                              attn_sink[i], mix_norm_g[i], w_out[i], cos, sin, not last)
        xl = xl + g1 * yl
        hl = rmsnorm(xl, norm2_g[i]) * (1.0 + s2) + sh2
        if last:
            xl = xl + g2 * peer(hl, peer_wq[i], peer_keys[i], peer_u[i], peer_v[i])
        else:
            xc = xc + cg1 * yc
            hc = rmsnorm(xc, norm2_g[i]) * (1.0 + cs2) + csh2
            f = peer(jnp.concatenate([hc, hl], axis=1), peer_wq[i], peer_keys[i], peer_u[i], peer_v[i])
            xc = xc + cg2 * f[:, :C]
            xl = xl + g2 * f[:, C:]
    return rmsnorm(xl, final_g)
```

```python
import functools
import math

import jax
import jax.numpy as jnp
from jax import lax
from jax.experimental import pallas as pl
from jax.experimental.pallas import tpu as pltpu

F32 = jnp.float32
BF16 = jnp.bfloat16

EPS = 1e-6
GRID_W = 64
D_CONV = 256
D_SGU = 256
SGU_HEADS = 4
SGU_CHUNK = 128
N_HEADS = 8
N_KV_HEADS = 2
HEAD_DIM = 64
D_ATTN = N_HEADS * HEAD_DIM
BLOCK = 128
ROPE_THETA = 10000.0
ROPE_FREQS = HEAD_DIM // 4
CONV_END = 3 * D_CONV
SGU_END = CONV_END + 2 * D_SGU
Q_END = SGU_END + D_ATTN
K_END = Q_END + N_KV_HEADS * HEAD_DIM
N_KEYS = 128
PEER_HEADS = 8
PEER_TOPK = 16
PEER_DHALF = 128

LANES = 128
SUBLANES = 8
VMEM_LIMIT_BYTES = 56 * 1024 * 1024

ROW_TILE = 512
PEER_TOKENS = 512
PEER_EXPERTS = 1024

_QO = SGU_END
_QR = _QO + D_ATTN
_KA = _QR + D_ATTN
_KB = _KA + 128
_KAR = _KB + 128
_KBR = _KAR + 128
_VA = _KBR + 128
_VB = _VA + 128
D_AUG = _VB + 128

_CAND = [(k, l) for k in range(PEER_TOPK) for l in range(PEER_TOPK) if (k + 1) * (l + 1) <= PEER_TOPK]


def _cparams(n_axes):
    return pltpu.CompilerParams(dimension_semantics=("arbitrary",) * n_axes,
                                vmem_limit_bytes=VMEM_LIMIT_BYTES)


def _gelu(x):
    c = math.sqrt(2.0 / math.pi)
    return 0.5 * x * (1.0 + jnp.tanh(c * (x + 0.044715 * (x * x * x))))


def _dot(a, b):
    return jnp.dot(a, b, preferred_element_type=F32)


def _dot_nt(a, b):
    return lax.dot_general(a, b, (((1,), (1,)), ((), ())), preferred_element_type=F32)


def _rms(x):
    return x * lax.rsqrt(jnp.mean(x * x, axis=-1, keepdims=True) + EPS)


def _mod_kernel(c_ref, w_ref, b_ref, o_ref):
    c = c_ref[...]
    sc = c / (1.0 + jnp.exp(-c))
    w = w_ref[...]
    c_hi = sc.astype(BF16)
    c_lo = (sc - c_hi.astype(F32)).astype(BF16)
    w_hi = w.astype(BF16)
    w_lo = (w - w_hi.astype(F32)).astype(BF16)
    o_ref[...] = _dot(c_hi, w_hi) + _dot(c_lo, w_hi) + _dot(c_hi, w_lo) + b_ref[...]


def _mod_call(cc, w_ada, b_ada):
    depth, d, n = w_ada.shape
    tn = 1536
    return pl.pallas_call(
        _mod_kernel,
        grid=(depth, n // tn),
        in_specs=[pl.BlockSpec((SUBLANES, d), lambda i, j: (0, 0)),
                  pl.BlockSpec((None, d, tn), lambda i, j: (i, 0, j)),
                  pl.BlockSpec((None, 1, tn), lambda i, j: (i, 0, j))],
        out_specs=pl.BlockSpec((None, SUBLANES, tn), lambda i, j: (i, 0, j)),
        out_shape=jax.ShapeDtypeStruct((depth, SUBLANES, n), F32),
        compiler_params=_cparams(2),
        name="adaln_mod",
    )(cc, w_ada, b_ada.reshape(depth, 1, n))


def _in_kernel(x_ref, mod_ref, g_ref, cos_ref, sin_ref, w_ref, pc_ref, ps_ref, q_ref, kv_ref):
    x = x_ref[...]
    h = _rms(x) * g_ref[...] * (1.0 + mod_ref[1:2, :]) + mod_ref[0:1, :]
    hb = h.astype(BF16)

    def proj(lo, hi):
        return _dot(hb, w_ref[:, lo:hi])

    pc_ref[...] = proj(0, CONV_END)
    ps_ref[...] = proj(CONV_END, SGU_END)
    cos = cos_ref[...]
    sin = sin_ref[...]
    cos4 = jnp.concatenate([cos] * 4, axis=1)
    sin4 = jnp.concatenate([sin] * 4, axis=1)
    scale = HEAD_DIM ** -0.5
    q = (proj(_QO, _QO + D_ATTN) * cos4 + proj(_QR, _QR + D_ATTN) * sin4) * scale
    q_ref[...] = q.astype(BF16)
    ka = proj(_KA, _KA + 128) * cos + proj(_KAR, _KAR + 128) * sin
    kb = proj(_KB, _KB + 128) * cos + proj(_KBR, _KBR + 128) * sin
    kv_ref[:, 0:128] = ka.astype(BF16)
    kv_ref[:, 128:256] = kb.astype(BF16)
    kv_ref[:, 256:512] = proj(_VA, _VA + 256).astype(BF16)


def _in_call(x, mod, norm_g, cos2, sin2, w_aug):
    b, r, d = x.shape
    tm = min(ROW_TILE, r)
    row = lambda n: pl.BlockSpec((None, tm, n), lambda i, t: (i, t, 0))
    return pl.pallas_call(
        _in_kernel,
        grid=(b, r // tm),
        in_specs=[row(d),
                  pl.BlockSpec((None, 6, d), lambda i, t: (i, 0, 0)),
                  pl.BlockSpec((1, d), lambda i, t: (0, 0)),
                  pl.BlockSpec((tm, 128), lambda i, t: (t, 0)),
                  pl.BlockSpec((tm, 128), lambda i, t: (t, 0)),
                  pl.BlockSpec((d, D_AUG), lambda i, t: (0, 0))],
        out_specs=[row(CONV_END), row(2 * D_SGU), row(D_ATTN), row(512)],
        out_shape=[jax.ShapeDtypeStruct((b, r, CONV_END), F32),
                   jax.ShapeDtypeStruct((b, r, 2 * D_SGU), F32),
                   jax.ShapeDtypeStruct((b, r, D_ATTN), BF16),
                   jax.ShapeDtypeStruct((b, r, 512), BF16)],
        compiler_params=_cparams(2),
        name="in_proj",
    )(x, mod, norm_g, cos2, sin2, w_aug)


def _softmax_pair(s_list, sink):
    m = sink
    for s in s_list:
        m = jnp.maximum(m, jnp.max(s, axis=-1, keepdims=True))
    ps = [jnp.exp(s - m) for s in s_list]
    denom = jnp.exp(sink - m)
    for p in ps:
        denom = denom + jnp.sum(p, axis=-1, keepdims=True)
    return ps, 1.0 / denom


def _mix_kernel(x_ref, pc_ref, pcp_ref, pcn_ref, ps_ref, q_ref, kv_ref, kvp_ref, kvn_ref, kvc_ref,
                mod_ref, convw_ref, sgug_ref, sguw_ref, sgub_ref, sink_ref, mixg_ref, wout_ref, n2g_ref,
                xo_ref, h2_ref, kvx_ref, attn_ref, sgu_ref, *, local):
    tq = x_ref.shape[0]
    nblk = tq // BLOCK
    t = pl.program_id(1)
    nt = pl.num_programs(1)

    pc = pc_ref[...]
    z = pc[:, D_CONV:2 * D_CONV] * pc[:, 2 * D_CONV:]
    z_before = pcp_ref[7:8, D_CONV:2 * D_CONV] * pcp_ref[7:8, 2 * D_CONV:]
    z_after = pcn_ref[0:1, D_CONV:2 * D_CONV] * pcn_ref[0:1, 2 * D_CONV:]
    z_before = z_before * (t > 0).astype(F32)
    z_after = z_after * (t < nt - 1).astype(F32)
    rows = lax.broadcasted_iota(jnp.int32, (tq, D_CONV), 0)
    z_prev = jnp.where(rows == 0, z_before, pltpu.roll(z, 1, axis=0))
    z_next = jnp.where(rows == tq - 1, z_after, pltpu.roll(z, tq - 1, axis=0))
    conv = pc[:, :D_CONV] * (z_prev * convw_ref[0:1, :] + z * convw_ref[1:2, :] + z_next * convw_ref[2:3, :])

    zg = _gelu(ps_ref[...])
    u = zg[:, :D_SGU]
    v = zg[:, D_SGU:]
    mu = jnp.mean(v, axis=-1, keepdims=True)
    vc = v - mu
    vn = vc * lax.rsqrt(jnp.mean(vc * vc, axis=-1, keepdims=True) + EPS) * sgug_ref[...]
    lane = lax.broadcasted_iota(jnp.int32, (BLOCK, LANES), 1)
    low = lane < HEAD_DIM
    for cb in range(nblk):
        pieces = []
        for a in range(SGU_HEADS // 2):
            vp = vn[cb * BLOCK:(cb + 1) * BLOCK, a * LANES:(a + 1) * LANES]
            v_lo = jnp.where(low, vp, 0.0).astype(BF16)
            v_hi = jnp.where(low, 0.0, vp).astype(BF16)
            pieces.append(_dot(sguw_ref[2 * a], v_lo) + _dot(sguw_ref[2 * a + 1], v_hi))
        s = jnp.concatenate(pieces, axis=1) + sgub_ref[...]
        sgu_ref[cb * BLOCK:(cb + 1) * BLOCK, :] = u[cb * BLOCK:(cb + 1) * BLOCK, :] * s

    if local:
        kvx_ref[0:BLOCK, :] = kvp_ref[...]
        kvx_ref[BLOCK:BLOCK + tq, :] = kv_ref[...]
        kvx_ref[BLOCK + tq:, :] = kvn_ref[...]
    kvc = kvc_ref[...]
    qi = lax.broadcasted_iota(jnp.int32, (BLOCK, 3 * BLOCK), 0)
    ko = lax.broadcasted_iota(jnp.int32, (BLOCK, 3 * BLOCK), 1)
    band = (ko >= qi) & (ko <= qi + 2 * BLOCK)

    def attend(jb, carry):
        r0 = pl.multiple_of(jb * BLOCK, BLOCK)
        if local:
            n = t * nblk + jb
            first_key = jnp.where(n > 0, 0, BLOCK)
            end_key = jnp.where(n < nt * nblk - 1, 3 * BLOCK, 2 * BLOCK)
            ok = band & (ko >= first_key) & (ko < end_key)
            kvl = kvx_ref[pl.ds(r0, 3 * BLOCK), :]
        for a in range(N_HEADS // 2):
            qp = q_ref[pl.ds(r0, BLOCK), a * LANES:(a + 1) * LANES]
            q_lo = jnp.where(low, qp, jnp.zeros_like(qp))
            q_hi = jnp.where(low, jnp.zeros_like(qp), qp)
            first = a < N_HEADS // 4
            outs = []
            for qh, even in ((q_lo, True), (q_hi, False)):
                natural = first == even
                ksel = slice(0, 128) if natural else slice(128, 256)
                vsel = slice(256, 384) if natural else slice(384, 512)
                hd = 2 * a + (0 if even else 1)
                sink = sink_ref[hd:hd + 1, 0:1]
                s_list = [_dot_nt(qh, kvc[:, ksel])]
                v_list = [kvc[:, vsel]]
                if local:
                    s_list.append(jnp.where(ok, _dot_nt(qh, kvl[:, ksel]), -jnp.inf))
                    v_list.append(kvl[:, vsel])
                ps, rden = _softmax_pair(s_list, sink)
                o = _dot(ps[0].astype(BF16), v_list[0])
                for p, vv in zip(ps[1:], v_list[1:]):
                    o = o + _dot(p.astype(BF16), vv)
                outs.append(o * rden)
            attn_ref[pl.ds(r0, BLOCK), a * LANES:(a + 1) * LANES] = jnp.where(low, outs[0], outs[1])
        return carry

    lax.fori_loop(0, nblk, attend, 0)

    g = mixg_ref[...]
    yc = (_rms(conv) * g[:, :D_CONV]).astype(BF16)
    ys = (_rms(sgu_ref[...]) * g[:, D_CONV:D_CONV + D_SGU]).astype(BF16)
    ya = (_rms(attn_ref[...]) * g[:, D_CONV + D_SGU:]).astype(BF16)
    yl = (_dot(yc, wout_ref[0:D_CONV, :]) + _dot(ys, wout_ref[D_CONV:D_CONV + D_SGU, :])
          + _dot(ya, wout_ref[D_CONV + D_SGU:, :]))
    xn = x_ref[...] + mod_ref[2:3, :] * yl
    xo_ref[...] = xn
    h2 = _rms(xn) * n2g_ref[...] * (1.0 + mod_ref[4:5, :]) + mod_ref[3:4, :]
    h2_ref[...] = h2.astype(BF16)


def _mix_call(x, pc, ps, q, kv, kvc, mod, conv_w, sgu_g, sgu_w, sgu_b, sink_b, mix_g, w_out, n2g, *, local):
    b, r, d = x.shape
    c = kvc.shape[1]
    tq = min(ROW_TILE, r)
    nt = r // tq
    hb = tq // SUBLANES
    kb = tq // BLOCK
    row = lambda n: pl.BlockSpec((None, tq, n), lambda i, t: (i, t, 0))
    full = lambda shape: pl.BlockSpec(shape, lambda i, t: (0,) * len(shape))
    return pl.pallas_call(
        functools.partial(_mix_kernel, local=local),
        grid=(b, nt),
        in_specs=[row(d), row(CONV_END),
                  pl.BlockSpec((None, SUBLANES, CONV_END), lambda i, t: (i, jnp.maximum(t * hb - 1, 0), 0)),
                  pl.BlockSpec((None, SUBLANES, CONV_END), lambda i, t: (i, jnp.minimum((t + 1) * hb, nt * hb - 1), 0)),
                  row(2 * D_SGU), row(D_ATTN), row(512),
                  pl.BlockSpec((None, BLOCK, 512), lambda i, t: (i, jnp.maximum(t * kb - 1, 0), 0)),
                  pl.BlockSpec((None, BLOCK, 512), lambda i, t: (i, jnp.minimum((t + 1) * kb, nt * kb - 1), 0)),
                  pl.BlockSpec((None, c, 512), lambda i, t: (i, 0, 0)),
                  pl.BlockSpec((None, 6, d), lambda i, t: (i, 0, 0)),
                  full((3, D_CONV)), full((1, D_SGU)), full((SGU_HEADS, SGU_CHUNK, SGU_CHUNK)),
                  full((SGU_CHUNK, D_SGU)), full((N_HEADS, LANES)), full((1, d)), full((d, d)), full((1, d))],
        out_specs=[row(d), row(d)],
        out_shape=[jax.ShapeDtypeStruct((b, r, d), F32), jax.ShapeDtypeStruct((b, r, d), BF16)],
        scratch_shapes=[pltpu.VMEM((tq + 2 * BLOCK, 512), BF16),
                        pltpu.VMEM((tq, D_ATTN), F32),
                        pltpu.VMEM((tq, D_SGU), F32)],
        compiler_params=_cparams(2),
        name="mixers_local" if local else "mixers_ctx",
    )(x, pc, pc, pc, ps, q, kv, kv, kv, kvc, mod, conv_w, sgu_g, sgu_w, sgu_b, sink_b, mix_g, w_out, n2g)


def _peer_select(h_ref, wq_ref, k0_ref, k1_ref, e0_ref, e1_ref, th_ref, s0_ref, s1_ref, a_ref, b_ref):
    tm = h_ref.shape[0]
    hb = h_ref[...]
    nh = PEER_HEADS
    half = nh * PEER_DHALF
    q0 = _dot_nt(wq_ref[0:half, :], hb).astype(BF16)
    s0_ref[...] = _dot(k0_ref[...], q0)
    q1 = _dot_nt(wq_ref[half:2 * half, :], hb).astype(BF16)
    for hh in range(nh):
        s1_ref[hh] = _dot(k1_ref[hh], q1[hh * PEER_DHALF:(hh + 1) * PEER_DHALF, :])

    neg = -jnp.inf

    a0 = s0_ref[0:nh, :]
    for i in range(1, N_KEYS):
        a0 = jnp.maximum(a0, s0_ref[i * nh:(i + 1) * nh, :])
    for i in range(N_KEYS):
        e0_ref[i] = jnp.exp(s0_ref[i * nh:(i + 1) * nh, :] - a0)
    a_ref[0] = jnp.ones_like(a0)

    def next_a(k, m_prev):
        m_new = jnp.full_like(m_prev, neg)
        for i in range(N_KEYS):
            blk = s0_ref[i * nh:(i + 1) * nh, :]
            blk = jnp.where(blk == m_prev, neg, blk)
            s0_ref[i * nh:(i + 1) * nh, :] = blk
            m_new = jnp.maximum(m_new, blk)
        a_ref[k] = jnp.exp(m_new - a0)
        return m_new

    lax.fori_loop(1, PEER_TOPK, next_a, a0)
    ea = [a_ref[k] for k in range(PEER_TOPK)]

    for hh in range(nh):
        b0 = jnp.max(s1_ref[hh], axis=0, keepdims=True)
        e1_ref[hh] = jnp.exp(s1_ref[hh] - b0)
        b_ref[0, hh:hh + 1, :] = jnp.ones_like(b0)

        def next_b(l, m_prev, hh=hh, b0=b0):
            s = s1_ref[hh]
            s = jnp.where(s == m_prev, neg, s)
            s1_ref[hh] = s
            m_new = jnp.max(s, axis=0, keepdims=True)
            b_ref[l, hh:hh + 1, :] = jnp.exp(m_new - b0)
            return m_new

        lax.fori_loop(1, PEER_TOPK, next_b, b0)
    eb = [b_ref[l] for l in range(PEER_TOPK)]

    cand = [ea[k] * eb[l] for k, l in _CAND]
    work = list(cand)
    top = None
    for r in range(PEER_TOPK):
        top = work[0]
        for w in work[1:]:
            top = jnp.maximum(top, w)
        if r < PEER_TOPK - 1:
            work = [jnp.where(w == top, -1.0, w) for w in work]
    zsum = jnp.zeros_like(top)
    for p in cand:
        zsum = zsum + jnp.where(p >= top, p, 0.0)
    rz = 1.0 / zsum
    thn = jnp.full_like(top, jnp.inf)
    for (k, l), p in zip(_CAND, cand):
        thn = jnp.minimum(thn, jnp.where(p >= top, (ea[k] * rz) * eb[l], jnp.inf))
    th_ref[...] = thn
    for i in range(N_KEYS):
        e0_ref[i] = e0_ref[i] * rz


def _peer_kernel(h_ref, x_ref, mod_ref, fg_ref, wq_ref, k0_ref, k1_ref, u_ref, vt_ref, o_ref,
                 e0_ref, e1_ref, th_ref, s0_ref, s1_ref, a_ref, b_ref, hbuf_ref, acc_ref, *, final_norm):
    e = pl.program_id(2)
    ne = pl.num_programs(2)
    eb = u_ref.shape[0]
    rows_per_step = eb // N_KEYS

    @pl.when(e == 0)
    def _():
        _peer_select(h_ref, wq_ref, k0_ref, k1_ref, e0_ref, e1_ref, th_ref, s0_ref, s1_ref, a_ref, b_ref)
        acc_ref[...] = jnp.zeros_like(acc_ref)

    def expert_rows(ii, carry):
        r0 = pl.multiple_of(ii * N_KEYS, N_KEYS)
        a = _dot_nt(u_ref[pl.ds(r0, N_KEYS), :], h_ref[...])
        e0 = e0_ref[e * rows_per_step + ii]
        gate = jnp.zeros_like(a)
        for hh in range(PEER_HEADS):
            p = e0[hh:hh + 1, :] * e1_ref[hh]
            gate = gate + jnp.where(p >= th_ref[hh:hh + 1, :], p, 0.0)
        hbuf_ref[pl.ds(r0, N_KEYS), :] = (_gelu(a) * gate).astype(BF16)
        return carry

    lax.fori_loop(0, rows_per_step, expert_rows, 0)
    acc_ref[...] += _dot(vt_ref[...], hbuf_ref[...])

    @pl.when(e == ne - 1)
    def _():
        y = x_ref[...] + mod_ref[5:6, :] * acc_ref[...].T
        if final_norm:
            y = _rms(y) * fg_ref[...]
        o_ref[...] = y


def _peer_call(h2, x, mod, final_g, wq_t, k0p, k1, u_b, vt_b, *, final_norm):
    b, r, d = x.shape
    n_exp = u_b.shape[0]
    tm = min(PEER_TOKENS, r)
    eb = PEER_EXPERTS
    nh = PEER_HEADS
    row = lambda: pl.BlockSpec((None, tm, d), lambda i, t, e: (i, t, 0))
    full = lambda shape: pl.BlockSpec(shape, lambda i, t, e: (0,) * len(shape))
    return pl.pallas_call(
        functools.partial(_peer_kernel, final_norm=final_norm),
        grid=(b, r // tm, n_exp // eb),
        in_specs=[row(), row(),
                  pl.BlockSpec((None, 6, d), lambda i, t, e: (i, 0, 0)),
                  full((1, d)), full(wq_t.shape), full(k0p.shape), full(k1.shape),
                  pl.BlockSpec((eb, d), lambda i, t, e: (e, 0)),
                  pl.BlockSpec((d, eb), lambda i, t, e: (0, e))],
        out_specs=row(),
        out_shape=jax.ShapeDtypeStruct((b, r, d), F32),
        scratch_shapes=[pltpu.VMEM((N_KEYS, nh, tm), F32),
                        pltpu.VMEM((nh, N_KEYS, tm), F32),
                        pltpu.VMEM((nh, tm), F32),
                        pltpu.VMEM((N_KEYS * nh, tm), F32),
                        pltpu.VMEM((nh, N_KEYS, tm), F32),
                        pltpu.VMEM((PEER_TOPK, nh, tm), F32),
                        pltpu.VMEM((PEER_TOPK, nh, tm), F32),
                        pltpu.VMEM((eb, tm), BF16),
                        pltpu.VMEM((d, tm), F32)],
        compiler_params=_cparams(3),
        name="peer_final" if final_norm else "peer",
    )(h2, x, mod, final_g, wq_t, k0p, k1, u_b, vt_b)


def _rope_tables(length):
    rows = length // GRID_W
    row = jnp.repeat(jnp.arange(rows), GRID_W).astype(F32)
    col = jnp.tile(jnp.arange(GRID_W), rows).astype(F32)
    inv = ROPE_THETA ** (-jnp.arange(ROPE_FREQS, dtype=F32) / ROPE_FREQS)
    ar = row[:, None] * inv[None, :]
    ac = col[:, None] * inv[None, :]
    ang = jnp.concatenate([ar, ar, ac, ac, ar, ar, ac, ac], axis=-1)
    return jnp.cos(ang), jnp.sin(ang)


def _rot_cols(w, heads):
    r = w.reshape(w.shape[0], heads, 4, ROPE_FREQS)
    return jnp.stack([-r[:, :, 1], r[:, :, 0], -r[:, :, 3], r[:, :, 2]], axis=2).reshape(w.shape)


def _swap_heads(w):
    return jnp.concatenate([w[:, HEAD_DIM:], w[:, :HEAD_DIM]], axis=1)


def _augment_w_in(w):
    wq = w[:, SGU_END:Q_END]
    wk = w[:, Q_END:K_END]
    wv = w[:, K_END:]
    wkr = _rot_cols(wk, N_KV_HEADS)
    cols = [w[:, :SGU_END], wq, _rot_cols(wq, N_HEADS), wk, _swap_heads(wk), wkr, _swap_heads(wkr),
            wv, _swap_heads(wv)]
    return jnp.concatenate(cols, axis=1).astype(BF16)


def kernel(x, c, ctx, c_ctx, w_ada, b_ada, norm1_g, norm2_g, w_in, conv_w, sgu_norm_g, sgu_w, sgu_b,
           attn_sink, mix_norm_g, w_out, peer_wq, peer_keys, peer_u, peer_v, final_g):
    bsz, length, d = x.shape
    n_ctx = ctx.shape[1]
    depth = w_ada.shape[0]
    nh = PEER_HEADS

    cc = jnp.zeros((SUBLANES, d), F32).at[:bsz].set(c).at[bsz].set(c_ctx)
    mod = _mod_call(cc, w_ada, b_ada)

    cos_l, sin_l = _rope_tables(length)
    cos_c = jnp.ones((n_ctx, 2 * HEAD_DIM), F32)
    sin_c = jnp.zeros((n_ctx, 2 * HEAD_DIM), F32)
    fg = final_g.reshape(1, d)

    xl, xc = x, ctx
    for i in range(depth):
        last = i == depth - 1
        mod_l = mod[i, :bsz].reshape(bsz, 6, d)
        mod_c = jnp.broadcast_to(mod[i, bsz].reshape(1, 6, d), (bsz, 6, d))
        n1g = norm1_g[i].reshape(1, d)
        n2g = norm2_g[i].reshape(1, d)
        w_aug = _augment_w_in(w_in[i])
        sgu_g = sgu_norm_g[i].reshape(1, D_SGU)
        sgu_wb = sgu_w[i].astype(BF16)
        sgu_bias = jnp.repeat(sgu_b[i].T, D_SGU // SGU_HEADS, axis=1)
        sink_b = jnp.broadcast_to(attn_sink[i][:, None], (N_HEADS, LANES))
        mix_g = mix_norm_g[i].reshape(1, d)
        w_out_b = w_out[i].astype(BF16)
        wq_t = peer_wq[i].reshape(d, nh, 2, PEER_DHALF).transpose(2, 1, 3, 0).reshape(2 * nh * PEER_DHALF, d)
        wq_t = wq_t.astype(BF16)
        k0p = jnp.einsum('hid,hg->ihgd', peer_keys[i][:, 0], jnp.eye(nh, dtype=F32))
        k0p = k0p.reshape(N_KEYS * nh, nh * PEER_DHALF).astype(BF16)
        k1 = peer_keys[i][:, 1].astype(BF16)
        u_b = peer_u[i].astype(BF16)
        vt_b = peer_v[i].T.astype(BF16)
        mixer_w = (conv_w[i], sgu_g, sgu_wb, sgu_bias, sink_b, mix_g, w_out_b, n2g)

        pc_c, ps_c, q_c, kv_c = _in_call(xc, mod_c, n1g, cos_c, sin_c, w_aug)
        pc_l, ps_l, q_l, kv_l = _in_call(xl, mod_l, n1g, cos_l, sin_l, w_aug)
        xl, h2_l = _mix_call(xl, pc_l, ps_l, q_l, kv_l, kv_c, mod_l, *mixer_w, local=True)
        if not last:
            xc, h2_c = _mix_call(xc, pc_c, ps_c, q_c, kv_c, kv_c, mod_c, *mixer_w, local=False)
            xc = _peer_call(h2_c, xc, mod_c, fg, wq_t, k0p, k1, u_b, vt_b, final_norm=False)
        xl = _peer_call(h2_l, xl, mod_l, fg, wq_t, k0p, k1, u_b, vt_b, final_norm=last)
    return xl
```

```python
import functools
import math

import jax
import jax.numpy as jnp
from jax import lax
from jax.experimental import pallas as pl
from jax.experimental.pallas import tpu as pltpu

F32 = jnp.float32
BF16 = jnp.bfloat16

EPS = 1e-6
GRID_W = 64
D_CONV = 256
D_SGU = 256
SGU_HEADS = 4
SGU_CHUNK = 128
N_HEADS = 8
N_KV_HEADS = 2
HEAD_DIM = 64
D_ATTN = N_HEADS * HEAD_DIM
BLOCK = 128
ROPE_THETA = 10000.0
ROPE_FREQS = HEAD_DIM // 4
CONV_END = 3 * D_CONV
SGU_END = CONV_END + 2 * D_SGU
Q_END = SGU_END + D_ATTN
K_END = Q_END + N_KV_HEADS * HEAD_DIM
N_KEYS = 128
PEER_HEADS = 8
PEER_TOPK = 16
PEER_DHALF = 128

LANES = 128
SUBLANES = 8
VMEM_LIMIT_BYTES = 56 * 1024 * 1024

ROW_TILE = 512
PEER_TOKENS = 512
PEER_EXPERTS = 1024

_QO = SGU_END
_QR = _QO + D_ATTN
_KA = _QR + D_ATTN
_KB = _KA + 128
_KAR = _KB + 128
_KBR = _KAR + 128
_VA = _KBR + 128
_VB = _VA + 128
D_AUG = _VB + 128

_CAND = [(k, l) for k in range(PEER_TOPK) for l in range(PEER_TOPK) if (k + 1) * (l + 1) <= PEER_TOPK]


def _cparams(n_axes):
    return pltpu.CompilerParams(dimension_semantics=("arbitrary",) * n_axes,
                                vmem_limit_bytes=VMEM_LIMIT_BYTES)


def _gelu(x):
    c = math.sqrt(2.0 / math.pi)
    return 0.5 * x * (1.0 + jnp.tanh(c * (x + 0.044715 * (x * x * x))))


def _dot(a, b):
    return jnp.dot(a, b, preferred_element_type=F32)


def _dot_nt(a, b):
    return lax.dot_general(a, b, (((1,), (1,)), ((), ())), preferred_element_type=F32)


def _rms(x):
    return x * lax.rsqrt(jnp.mean(x * x, axis=-1, keepdims=True) + EPS)


def _mod_kernel(c_ref, w_ref, b_ref, o_ref):
    c = c_ref[...]
    sc = c / (1.0 + jnp.exp(-c))
    w = w_ref[...]
    c_hi = sc.astype(BF16)
    c_lo = (sc - c_hi.astype(F32)).astype(BF16)
    w_hi = w.astype(BF16)
    w_lo = (w - w_hi.astype(F32)).astype(BF16)
    o_ref[...] = _dot(c_hi, w_hi) + _dot(c_lo, w_hi) + _dot(c_hi, w_lo) + b_ref[...]


def _mod_call(cc, w_ada, b_ada):
    depth, d, n = w_ada.shape
    tn = 1536
    return pl.pallas_call(
        _mod_kernel,
        grid=(depth, n // tn),
        in_specs=[pl.BlockSpec((SUBLANES, d), lambda i, j: (0, 0)),
                  pl.BlockSpec((None, d, tn), lambda i, j: (i, 0, j)),
                  pl.BlockSpec((None, 1, tn), lambda i, j: (i, 0, j))],
        out_specs=pl.BlockSpec((None, SUBLANES, tn), lambda i, j: (i, 0, j)),
        out_shape=jax.ShapeDtypeStruct((depth, SUBLANES, n), F32),
        compiler_params=_cparams(2),
        name="adaln_mod",
    )(cc, w_ada, b_ada.reshape(depth, 1, n))


def _in_kernel(x_ref, mod_ref, g_ref, cos_ref, sin_ref, w_ref, pc_ref, ps_ref, q_ref, kv_ref):
    x = x_ref[...]
    h = _rms(x) * g_ref[...] * (1.0 + mod_ref[1:2, :]) + mod_ref[0:1, :]
    hb = h.astype(BF16)

    def proj(lo, hi):
        return _dot(hb, w_ref[:, lo:hi])

    pc_ref[...] = proj(0, CONV_END)
    ps_ref[...] = proj(CONV_END, SGU_END)
    cos = cos_ref[...]
    sin = sin_ref[...]
    cos4 = jnp.concatenate([cos] * 4, axis=1)
    sin4 = jnp.concatenate([sin] * 4, axis=1)
    scale = HEAD_DIM ** -0.5
    q = (proj(_QO, _QO + D_ATTN) * cos4 + proj(_QR, _QR + D_ATTN) * sin4) * scale
    q_ref[...] = q.astype(BF16)
    ka = proj(_KA, _KA + 128) * cos + proj(_KAR, _KAR + 128) * sin
    kb = proj(_KB, _KB + 128) * cos + proj(_KBR, _KBR + 128) * sin
    kv_ref[:, 0:128] = ka.astype(BF16)
    kv_ref[:, 128:256] = kb.astype(BF16)
    kv_ref[:, 256:512] = proj(_VA, _VA + 256).astype(BF16)


def _in_call(x, mod, norm_g, cos2, sin2, w_aug):
    b, r, d = x.shape
    tm = min(ROW_TILE, r)
    row = lambda n: pl.BlockSpec((None, tm, n), lambda i, t: (i, t, 0))
    return pl.pallas_call(
        _in_kernel,
        grid=(b, r // tm),
        in_specs=[row(d),
                  pl.BlockSpec((None, 6, d), lambda i, t: (i, 0, 0)),
                  pl.BlockSpec((1, d), lambda i, t: (0, 0)),
                  pl.BlockSpec((tm, 128), lambda i, t: (t, 0)),
                  pl.BlockSpec((tm, 128), lambda i, t: (t, 0)),
                  pl.BlockSpec((d, D_AUG), lambda i, t: (0, 0))],
        out_specs=[row(CONV_END), row(2 * D_SGU), row(D_ATTN), row(512)],
        out_shape=[jax.ShapeDtypeStruct((b, r, CONV_END), F32),
                   jax.ShapeDtypeStruct((b, r, 2 * D_SGU), F32),
                   jax.ShapeDtypeStruct((b, r, D_ATTN), BF16),
                   jax.ShapeDtypeStruct((b, r, 512), BF16)],
        compiler_params=_cparams(2),
        name="in_proj",
    )(x, mod, norm_g, cos2, sin2, w_aug)


def _softmax_pair(s_list, sink):
    m = sink
    for s in s_list:
        m = jnp.maximum(m, jnp.max(s, axis=-1, keepdims=True))
    ps = [jnp.exp(s - m) for s in s_list]
    denom = jnp.exp(sink - m)
    for p in ps:
        denom = denom + jnp.sum(p, axis=-1, keepdims=True)
    return ps, 1.0 / denom


def _mix_kernel(x_ref, pc_ref, pcp_ref, pcn_ref, ps_ref, q_ref, kv_ref, kvp_ref, kvn_ref, kvc_ref,
                mod_ref, convw_ref, sgug_ref, sguw_ref, sgub_ref, sink_ref, mixg_ref, wout_ref, n2g_ref,
                xo_ref, h2_ref, kvx_ref, attn_ref, sgu_ref, *, local):
    tq = x_ref.shape[0]
    nblk = tq // BLOCK
    t = pl.program_id(1)
    nt = pl.num_programs(1)

    pc = pc_ref[...]
    z = pc[:, D_CONV:2 * D_CONV] * pc[:, 2 * D_CONV:]
    z_before = pcp_ref[7:8, D_CONV:2 * D_CONV] * pcp_ref[7:8, 2 * D_CONV:]
    z_after = pcn_ref[0:1, D_CONV:2 * D_CONV] * pcn_ref[0:1, 2 * D_CONV:]
    z_before = z_before * (t > 0).astype(F32)
    z_after = z_after * (t < nt - 1).astype(F32)
    rows = lax.broadcasted_iota(jnp.int32, (tq, D_CONV), 0)
    z_prev = jnp.where(rows == 0, z_before, pltpu.roll(z, 1, axis=0))
    z_next = jnp.where(rows == tq - 1, z_after, pltpu.roll(z, tq - 1, axis=0))
    conv = pc[:, :D_CONV] * (z_prev * convw_ref[0:1, :] + z * convw_ref[1:2, :] + z_next * convw_ref[2:3, :])

    zg = _gelu(ps_ref[...])
    u = zg[:, :D_SGU]
    v = zg[:, D_SGU:]
    mu = jnp.mean(v, axis=-1, keepdims=True)
    vc = v - mu
    vn = vc * lax.rsqrt(jnp.mean(vc * vc, axis=-1, keepdims=True) + EPS) * sgug_ref[...]
    lane = lax.broadcasted_iota(jnp.int32, (BLOCK, LANES), 1)
    low = lane < HEAD_DIM
    for cb in range(nblk):
        pieces = []
        for a in range(SGU_HEADS // 2):
            vp = vn[cb * BLOCK:(cb + 1) * BLOCK, a * LANES:(a + 1) * LANES]
            v_lo = jnp.where(low, vp, 0.0).astype(BF16)
            v_hi = jnp.where(low, 0.0, vp).astype(BF16)
            pieces.append(_dot(sguw_ref[2 * a], v_lo) + _dot(sguw_ref[2 * a + 1], v_hi))
        s = jnp.concatenate(pieces, axis=1) + sgub_ref[...]
        sgu_ref[cb * BLOCK:(cb + 1) * BLOCK, :] = u[cb * BLOCK:(cb + 1) * BLOCK, :] * s

    if local:
        kvx_ref[0:BLOCK, :] = kvp_ref[...]
        kvx_ref[BLOCK:BLOCK + tq, :] = kv_ref[...]
        kvx_ref[BLOCK + tq:, :] = kvn_ref[...]
    kvc = kvc_ref[...]
    qi = lax.broadcasted_iota(jnp.int32, (BLOCK, 3 * BLOCK), 0)
    ko = lax.broadcasted_iota(jnp.int32, (BLOCK, 3 * BLOCK), 1)
    band = (ko >= qi) & (ko <= qi + 2 * BLOCK)

    def attend(jb, carry):
        r0 = pl.multiple_of(jb * BLOCK, BLOCK)
        if local:
            n = t * nblk + jb
            first_key = jnp.where(n > 0, 0, BLOCK)
            end_key = jnp.where(n < nt * nblk - 1, 3 * BLOCK, 2 * BLOCK)
            ok = band & (ko >= first_key) & (ko < end_key)
            kvl = kvx_ref[pl.ds(r0, 3 * BLOCK), :]
        for a in range(N_HEADS // 2):
            qp = q_ref[pl.ds(r0, BLOCK), a * LANES:(a + 1) * LANES]
            q_lo = jnp.where(low, qp, jnp.zeros_like(qp))
            q_hi = jnp.where(low, jnp.zeros_like(qp), qp)
            first = a < N_HEADS // 4
            outs = []
            for qh, even in ((q_lo, True), (q_hi, False)):
                natural = first == even
                ksel = slice(0, 128) if natural else slice(128, 256)
                vsel = slice(256, 384) if natural else slice(384, 512)
                hd = 2 * a + (0 if even else 1)
                sink = sink_ref[hd:hd + 1, 0:1]
                s_list = [_dot_nt(qh, kvc[:, ksel])]
                v_list = [kvc[:, vsel]]
                if local:
                    s_list.append(jnp.where(ok, _dot_nt(qh, kvl[:, ksel]), -jnp.inf))
                    v_list.append(kvl[:, vsel])
                ps, rden = _softmax_pair(s_list, sink)
                o = _dot(ps[0].astype(BF16), v_list[0])
                for p, vv in zip(ps[1:], v_list[1:]):
                    o = o + _dot(p.astype(BF16), vv)
                outs.append(o * rden)
            attn_ref[pl.ds(r0, BLOCK), a * LANES:(a + 1) * LANES] = jnp.where(low, outs[0], outs[1])
        return carry

    lax.fori_loop(0, nblk, attend, 0)

    g = mixg_ref[...]
    yc = (_rms(conv) * g[:, :D_CONV]).astype(BF16)
    ys = (_rms(sgu_ref[...]) * g[:, D_CONV:D_CONV + D_SGU]).astype(BF16)
    ya = (_rms(attn_ref[...]) * g[:, D_CONV + D_SGU:]).astype(BF16)
    yl = (_dot(yc, wout_ref[0:D_CONV, :]) + _dot(ys, wout_ref[D_CONV:D_CONV + D_SGU, :])
          + _dot(ya, wout_ref[D_CONV + D_SGU:, :]))
    xn = x_ref[...] + mod_ref[2:3, :] * yl
    xo_ref[...] = xn
    h2 = _rms(xn) * n2g_ref[...] * (1.0 + mod_ref[4:5, :]) + mod_ref[3:4, :]
    h2_ref[...] = h2.astype(BF16)


def _mix_call(x, pc, ps, q, kv, kvc, mod, conv_w, sgu_g, sgu_w, sgu_b, sink_b, mix_g, w_out, n2g, *, local):
    b, r, d = x.shape
    c = kvc.shape[1]
    tq = min(ROW_TILE, r)
    nt = r // tq
    hb = tq // SUBLANES
    kb = tq // BLOCK
    row = lambda n: pl.BlockSpec((None, tq, n), lambda i, t: (i, t, 0))
    full = lambda shape: pl.BlockSpec(shape, lambda i, t: (0,) * len(shape))
    return pl.pallas_call(
        functools.partial(_mix_kernel, local=local),
        grid=(b, nt),
        in_specs=[row(d), row(CONV_END),
                  pl.BlockSpec((None, SUBLANES, CONV_END), lambda i, t: (i, jnp.maximum(t * hb - 1, 0), 0)),
                  pl.BlockSpec((None, SUBLANES, CONV_END), lambda i, t: (i, jnp.minimum((t + 1) * hb, nt * hb - 1), 0)),
                  row(2 * D_SGU), row(D_ATTN), row(512),
                  pl.BlockSpec((None, BLOCK, 512), lambda i, t: (i, jnp.maximum(t * kb - 1, 0), 0)),
                  pl.BlockSpec((None, BLOCK, 512), lambda i, t: (i, jnp.minimum((t + 1) * kb, nt * kb - 1), 0)),
                  pl.BlockSpec((None, c, 512), lambda i, t: (i, 0, 0)),
                  pl.BlockSpec((None, 6, d), lambda i, t: (i, 0, 0)),
                  full((3, D_CONV)), full((1, D_SGU)), full((SGU_HEADS, SGU_CHUNK, SGU_CHUNK)),
                  full((SGU_CHUNK, D_SGU)), full((N_HEADS, LANES)), full((1, d)), full((d, d)), full((1, d))],
        out_specs=[row(d), row(d)],
        out_shape=[jax.ShapeDtypeStruct((b, r, d), F32), jax.ShapeDtypeStruct((b, r, d), BF16)],
        scratch_shapes=[pltpu.VMEM((tq + 2 * BLOCK, 512), BF16),
                        pltpu.VMEM((tq, D_ATTN), F32),
                        pltpu.VMEM((tq, D_SGU), F32)],
        compiler_params=_cparams(2),
        name="mixers_local" if local else "mixers_ctx",
    )(x, pc, pc, pc, ps, q, kv, kv, kv, kvc, mod, conv_w, sgu_g, sgu_w, sgu_b, sink_b, mix_g, w_out, n2g)


def _peer_select(h_ref, wq_ref, k0_ref, k1_ref, ht_ref, e0_ref, e1_ref, th_ref, s0_ref, s1_ref, a_ref, b_ref):
    tm = h_ref.shape[0]
    nh = PEER_HEADS
    half = nh * PEER_DHALF
    ht_ref[...] = h_ref[...].astype(F32).T.astype(BF16)
    ht = ht_ref[...]
    q0 = _dot(wq_ref[0:half, :], ht).astype(BF16)
    s0_ref[...] = _dot(k0_ref[...], q0)
    q1 = _dot(wq_ref[half:2 * half, :], ht).astype(BF16)
    for hh in range(nh):
        s1_ref[hh] = _dot(k1_ref[hh], q1[hh * PEER_DHALF:(hh + 1) * PEER_DHALF, :])

    neg = -jnp.inf

    a0 = s0_ref[0:nh, :]
    for i in range(1, N_KEYS):
        a0 = jnp.maximum(a0, s0_ref[i * nh:(i + 1) * nh, :])
    for i in range(N_KEYS):
        e0_ref[i] = jnp.exp(s0_ref[i * nh:(i + 1) * nh, :] - a0)
    a_ref[0] = jnp.ones_like(a0)

    def next_a(k, m_prev):
        m_new = jnp.full_like(m_prev, neg)
        for i in range(N_KEYS):
            blk = s0_ref[i * nh:(i + 1) * nh, :]
            blk = jnp.where(blk == m_prev, neg, blk)
            s0_ref[i * nh:(i + 1) * nh, :] = blk
            m_new = jnp.maximum(m_new, blk)
        a_ref[k] = jnp.exp(m_new - a0)
        return m_new

    lax.fori_loop(1, PEER_TOPK, next_a, a0)
    ea = [a_ref[k] for k in range(PEER_TOPK)]

    for hh in range(nh):
        b0 = jnp.max(s1_ref[hh], axis=0, keepdims=True)
        e1_ref[hh] = jnp.exp(s1_ref[hh] - b0).astype(BF16)
        b_ref[0, hh:hh + 1, :] = jnp.ones_like(b0)

        def next_b(l, m_prev, hh=hh, b0=b0):
            s = s1_ref[hh]
            s = jnp.where(s == m_prev, neg, s)
            s1_ref[hh] = s
            m_new = jnp.max(s, axis=0, keepdims=True)
            b_ref[l, hh:hh + 1, :] = jnp.exp(m_new - b0)
            return m_new

        lax.fori_loop(1, PEER_TOPK, next_b, b0)
    eb = [b_ref[l] for l in range(PEER_TOPK)]

    cand = [ea[k] * eb[l] for k, l in _CAND]
    work = list(cand)
    top = None
    for r in range(PEER_TOPK):
        top = work[0]
        for w in work[1:]:
            top = jnp.maximum(top, w)
        if r < PEER_TOPK - 1:
            work = [jnp.where(w == top, -1.0, w) for w in work]
    zsum = jnp.zeros_like(top)
    for p in cand:
        zsum = zsum + jnp.where(p >= top, p, 0.0)
    rz = 1.0 / zsum
    rnd = lambda v: v.astype(BF16).astype(F32)
    ean = [rnd(v * rz) for v in ea]
    ebn = [rnd(v) for v in eb]
    thn = jnp.full_like(top, jnp.inf)
    for (k, l), p in zip(_CAND, cand):
        thn = jnp.minimum(thn, jnp.where(p >= top, rnd(ean[k] * ebn[l]), jnp.inf))
    for hh in range(nh):
        th_ref[hh] = jnp.broadcast_to(thn[hh:hh + 1, :], (2 * SUBLANES, tm)).astype(BF16)
    for i in range(N_KEYS):
        e0_ref[i] = e0_ref[i] * rz


def _peer_scores(u_ref, ht_ref, lanes):
    return _dot(u_ref[...], ht_ref[:, lanes])


def _peer_gate(a, row0, e0_ref, e1_ref, th_ref, hbuf_ref, lanes):
    n_lanes = lanes.stop - lanes.start
    pack = 2 * SUBLANES
    for ii in range(a.shape[0] // N_KEYS):
        e0 = e0_ref[row0 + ii, :, lanes]
        e0r = [jnp.broadcast_to(e0[hh:hh + 1, :], (pack, n_lanes)).astype(BF16) for hh in range(PEER_HEADS)]
        for c in range(N_KEYS // pack):
            gate = None
            for hh in range(PEER_HEADS):
                p = e0r[hh] * e1_ref[hh, c * pack:(c + 1) * pack, lanes]
                sel = jnp.where(p >= th_ref[hh, :, lanes], p, jnp.zeros_like(p))
                gate = sel if gate is None else gate + sel
            r0 = ii * N_KEYS + c * pack
            hbuf_ref[r0:r0 + pack, lanes] = _gelu(a[r0:r0 + pack, :]).astype(BF16) * gate


def _peer_kernel(h_ref, x_ref, mod_ref, fg_ref, wq_ref, k0_ref, k1_ref, u_ref, vt_ref, o_ref,
                 ht_ref, e0_ref, e1_ref, th_ref, s0_ref, s1_ref, a_ref, b_ref, hbuf_ref, acc_ref, *, final_norm):
    e = pl.program_id(2)
    ne = pl.num_programs(2)
    tm = h_ref.shape[0]
    row0 = e * (u_ref.shape[0] // N_KEYS)

    @pl.when(e == 0)
    def _():
        _peer_select(h_ref, wq_ref, k0_ref, k1_ref, ht_ref, e0_ref, e1_ref, th_ref, s0_ref, s1_ref, a_ref, b_ref)
        acc_ref[...] = jnp.zeros_like(acc_ref)

    n_split = 2 if tm % (2 * 2 * LANES) == 0 else 1
    ranges = [slice(s * (tm // n_split), (s + 1) * (tm // n_split)) for s in range(n_split)]
    scores = [_peer_scores(u_ref, ht_ref, lanes) for lanes in ranges]
    for a, lanes in zip(scores, ranges):
        _peer_gate(a, row0, e0_ref, e1_ref, th_ref, hbuf_ref, lanes)
    for lanes in ranges:
        acc_ref[:, lanes] += _dot(vt_ref[...], hbuf_ref[:, lanes])

    @pl.when(e == ne - 1)
    def _():
        y = x_ref[...] + mod_ref[5:6, :] * acc_ref[...].T
        if final_norm:
            y = _rms(y) * fg_ref[...]
        o_ref[...] = y


def _peer_call(h2, x, mod, final_g, wq_t, k0p, k1, u_b, vt_b, *, final_norm):
    b, r, d = x.shape
    n_exp = u_b.shape[0]
    tm = min(PEER_TOKENS, r)
    eb = PEER_EXPERTS
    nh = PEER_HEADS
    row = lambda: pl.BlockSpec((None, tm, d), lambda i, t, e: (i, t, 0))
    full = lambda shape: pl.BlockSpec(shape, lambda i, t, e: (0,) * len(shape))
    return pl.pallas_call(
        functools.partial(_peer_kernel, final_norm=final_norm),
        grid=(b, r // tm, n_exp // eb),
        in_specs=[row(), row(),
                  pl.BlockSpec((None, 6, d), lambda i, t, e: (i, 0, 0)),
                  full((1, d)), full(wq_t.shape), full(k0p.shape), full(k1.shape),
                  pl.BlockSpec((eb, d), lambda i, t, e: (e, 0)),
                  pl.BlockSpec((d, eb), lambda i, t, e: (0, e))],
        out_specs=row(),
        out_shape=jax.ShapeDtypeStruct((b, r, d), F32),
        scratch_shapes=[pltpu.VMEM((d, tm), BF16),
                        pltpu.VMEM((N_KEYS, nh, tm), F32),
                        pltpu.VMEM((nh, N_KEYS, tm), BF16),
                        pltpu.VMEM((nh, 2 * SUBLANES, tm), BF16),
                        pltpu.VMEM((N_KEYS * nh, tm), F32),
                        pltpu.VMEM((nh, N_KEYS, tm), F32),
                        pltpu.VMEM((PEER_TOPK, nh, tm), F32),
                        pltpu.VMEM((PEER_TOPK, nh, tm), F32),
                        pltpu.VMEM((eb, tm), BF16),
                        pltpu.VMEM((d, tm), F32)],
        compiler_params=_cparams(3),
        name="peer_final" if final_norm else "peer",
    )(h2, x, mod, final_g, wq_t, k0p, k1, u_b, vt_b)


def _rope_tables(length):
    rows = length // GRID_W
    row = jnp.repeat(jnp.arange(rows), GRID_W).astype(F32)
    col = jnp.tile(jnp.arange(GRID_W), rows).astype(F32)
    inv = ROPE_THETA ** (-jnp.arange(ROPE_FREQS, dtype=F32) / ROPE_FREQS)
    ar = row[:, None] * inv[None, :]
    ac = col[:, None] * inv[None, :]
    ang = jnp.concatenate([ar, ar, ac, ac, ar, ar, ac, ac], axis=-1)
    return jnp.cos(ang), jnp.sin(ang)


def _rot_cols(w, heads):
    r = w.reshape(w.shape[0], heads, 4, ROPE_FREQS)
    return jnp.stack([-r[:, :, 1], r[:, :, 0], -r[:, :, 3], r[:, :, 2]], axis=2).reshape(w.shape)


def _swap_heads(w):
    return jnp.concatenate([w[:, HEAD_DIM:], w[:, :HEAD_DIM]], axis=1)


def _augment_w_in(w):
    wq = w[:, SGU_END:Q_END]
    wk = w[:, Q_END:K_END]
    wv = w[:, K_END:]
    wkr = _rot_cols(wk, N_KV_HEADS)
    cols = [w[:, :SGU_END], wq, _rot_cols(wq, N_HEADS), wk, _swap_heads(wk), wkr, _swap_heads(wkr),
            wv, _swap_heads(wv)]
    return jnp.concatenate(cols, axis=1).astype(BF16)


def kernel(x, c, ctx, c_ctx, w_ada, b_ada, norm1_g, norm2_g, w_in, conv_w, sgu_norm_g, sgu_w, sgu_b,
           attn_sink, mix_norm_g, w_out, peer_wq, peer_keys, peer_u, peer_v, final_g):
    bsz, length, d = x.shape
    n_ctx = ctx.shape[1]
    depth = w_ada.shape[0]
    nh = PEER_HEADS

    cc = jnp.zeros((SUBLANES, d), F32).at[:bsz].set(c).at[bsz].set(c_ctx)
    mod = _mod_call(cc, w_ada, b_ada)

    cos_l, sin_l = _rope_tables(length)
    cos_c = jnp.ones((n_ctx, 2 * HEAD_DIM), F32)
    sin_c = jnp.zeros((n_ctx, 2 * HEAD_DIM), F32)
    fg = final_g.reshape(1, d)

    xl, xc = x, ctx
    for i in range(depth):
        last = i == depth - 1
        mod_l = mod[i, :bsz].reshape(bsz, 6, d)
        mod_c = jnp.broadcast_to(mod[i, bsz].reshape(1, 6, d), (bsz, 6, d))
        n1g = norm1_g[i].reshape(1, d)
        n2g = norm2_g[i].reshape(1, d)
        w_aug = _augment_w_in(w_in[i])
        sgu_g = sgu_norm_g[i].reshape(1, D_SGU)
        sgu_wb = sgu_w[i].astype(BF16)
        sgu_bias = jnp.repeat(sgu_b[i].T, D_SGU // SGU_HEADS, axis=1)
        sink_b = jnp.broadcast_to(attn_sink[i][:, None], (N_HEADS, LANES))
        mix_g = mix_norm_g[i].reshape(1, d)
        w_out_b = w_out[i].astype(BF16)
        wq_t = peer_wq[i].reshape(d, nh, 2, PEER_DHALF).transpose(2, 1, 3, 0).reshape(2 * nh * PEER_DHALF, d)
        wq_t = wq_t.astype(BF16)
        k0p = jnp.einsum('hid,hg->ihgd', peer_keys[i][:, 0], jnp.eye(nh, dtype=F32))
        k0p = k0p.reshape(N_KEYS * nh, nh * PEER_DHALF).astype(BF16)
        k1 = peer_keys[i][:, 1].astype(BF16)
        u_b = peer_u[i].astype(BF16)
        vt_b = peer_v[i].T.astype(BF16)
        mixer_w = (conv_w[i], sgu_g, sgu_wb, sgu_bias, sink_b, mix_g, w_out_b, n2g)

        pc_c, ps_c, q_c, kv_c = _in_call(xc, mod_c, n1g, cos_c, sin_c, w_aug)
        pc_l, ps_l, q_l, kv_l = _in_call(xl, mod_l, n1g, cos_l, sin_l, w_aug)
        xl, h2_l = _mix_call(xl, pc_l, ps_l, q_l, kv_l, kv_c, mod_l, *mixer_w, local=True)
        if not last:
            xc, h2_c = _mix_call(xc, pc_c, ps_c, q_c, kv_c, kv_c, mod_c, *mixer_w, local=False)
            xc = _peer_call(h2_c, xc, mod_c, fg, wq_t, k0p, k1, u_b, vt_b, final_norm=False)
        xl = _peer_call(h2_l, xl, mod_l, fg, wq_t, k0p, k1, u_b, vt_b, final_norm=last)
    return xl
```

```python
import functools
import math

import jax
import jax.numpy as jnp
from jax import lax
from jax.experimental import pallas as pl
from jax.experimental.pallas import tpu as pltpu

F32 = jnp.float32
BF16 = jnp.bfloat16

EPS = 1e-6
GRID_W = 64
D_CONV = 256
D_SGU = 256
SGU_HEADS = 4
SGU_CHUNK = 128
N_HEADS = 8
N_KV_HEADS = 2
HEAD_DIM = 64
D_ATTN = N_HEADS * HEAD_DIM
BLOCK = 128
ROPE_THETA = 10000.0
ROPE_FREQS = HEAD_DIM // 4
CONV_END = 3 * D_CONV
SGU_END = CONV_END + 2 * D_SGU
Q_END = SGU_END + D_ATTN
K_END = Q_END + N_KV_HEADS * HEAD_DIM
N_KEYS = 128
PEER_HEADS = 8
PEER_TOPK = 16
PEER_DHALF = 128

LANES = 128
SUBLANES = 8
VMEM_LIMIT_BYTES = 56 * 1024 * 1024

ROW_TILE = 512
PEER_TOKENS = 512
PEER_EXPERTS = 1024

_QO = SGU_END
_QR = _QO + D_ATTN
_KA = _QR + D_ATTN
_KB = _KA + 128
_KAR = _KB + 128
_KBR = _KAR + 128
_VA = _KBR + 128
_VB = _VA + 128
D_AUG = _VB + 128

_CAND = [(k, l) for k in range(PEER_TOPK) for l in range(PEER_TOPK) if (k + 1) * (l + 1) <= PEER_TOPK]


def _cparams(n_axes):
    return pltpu.CompilerParams(dimension_semantics=("arbitrary",) * n_axes,
                                vmem_limit_bytes=VMEM_LIMIT_BYTES)


def _gelu(x):
    c = math.sqrt(2.0 / math.pi)
    return 0.5 * x * (1.0 + jnp.tanh(c * (x + 0.044715 * (x * x * x))))


def _dot(a, b):
    return jnp.dot(a, b, preferred_element_type=F32)


def _dot_nt(a, b):
    return lax.dot_general(a, b, (((1,), (1,)), ((), ())), preferred_element_type=F32)


def _rms(x):
    return x * lax.rsqrt(jnp.mean(x * x, axis=-1, keepdims=True) + EPS)


def _mod_kernel(c_ref, w_ref, b_ref, o_ref):
    c = c_ref[...]
    sc = c / (1.0 + jnp.exp(-c))
    w = w_ref[...]
    c_hi = sc.astype(BF16)
    c_lo = (sc - c_hi.astype(F32)).astype(BF16)
    w_hi = w.astype(BF16)
    w_lo = (w - w_hi.astype(F32)).astype(BF16)
    o_ref[...] = _dot(c_hi, w_hi) + _dot(c_lo, w_hi) + _dot(c_hi, w_lo) + b_ref[...]


def _mod_call(cc, w_ada, b_ada):
    depth, d, n = w_ada.shape
    tn = 1536
    return pl.pallas_call(
        _mod_kernel,
        grid=(depth, n // tn),
        in_specs=[pl.BlockSpec((SUBLANES, d), lambda i, j: (0, 0)),
                  pl.BlockSpec((None, d, tn), lambda i, j: (i, 0, j)),
                  pl.BlockSpec((None, 1, tn), lambda i, j: (i, 0, j))],
        out_specs=pl.BlockSpec((None, SUBLANES, tn), lambda i, j: (i, 0, j)),
        out_shape=jax.ShapeDtypeStruct((depth, SUBLANES, n), F32),
        compiler_params=_cparams(2),
        name="adaln_mod",
    )(cc, w_ada, b_ada.reshape(depth, 1, n))


def _in_kernel(x_ref, mod_ref, g_ref, cos_ref, sin_ref, w_ref, pc_ref, ps_ref, q_ref, kv_ref):
    x = x_ref[...]
    h = _rms(x) * g_ref[...] * (1.0 + mod_ref[1:2, :]) + mod_ref[0:1, :]
    hb = h.astype(BF16)

    def proj(lo, hi):
        return _dot(hb, w_ref[:, lo:hi])

    pc_ref[...] = proj(0, CONV_END)
    ps_ref[...] = proj(CONV_END, SGU_END)
    cos = cos_ref[...]
    sin = sin_ref[...]
    cos4 = jnp.concatenate([cos] * 4, axis=1)
    sin4 = jnp.concatenate([sin] * 4, axis=1)
    scale = HEAD_DIM ** -0.5
    q = (proj(_QO, _QO + D_ATTN) * cos4 + proj(_QR, _QR + D_ATTN) * sin4) * scale
    q_ref[...] = q.astype(BF16)
    ka = proj(_KA, _KA + 128) * cos + proj(_KAR, _KAR + 128) * sin
    kb = proj(_KB, _KB + 128) * cos + proj(_KBR, _KBR + 128) * sin
    kv_ref[:, 0:128] = ka.astype(BF16)
    kv_ref[:, 128:256] = kb.astype(BF16)
    kv_ref[:, 256:512] = proj(_VA, _VA + 256).astype(BF16)


def _in_call(x, mod, norm_g, cos2, sin2, w_aug):
    b, r, d = x.shape
    tm = min(ROW_TILE, r)
    row = lambda n: pl.BlockSpec((None, tm, n), lambda i, t: (i, t, 0))
    return pl.pallas_call(
        _in_kernel,
        grid=(b, r // tm),
        in_specs=[row(d),
                  pl.BlockSpec((None, 6, d), lambda i, t: (i, 0, 0)),
                  pl.BlockSpec((1, d), lambda i, t: (0, 0)),
                  pl.BlockSpec((tm, 128), lambda i, t: (t, 0)),
                  pl.BlockSpec((tm, 128), lambda i, t: (t, 0)),
                  pl.BlockSpec((d, D_AUG), lambda i, t: (0, 0))],
        out_specs=[row(CONV_END), row(2 * D_SGU), row(D_ATTN), row(512)],
        out_shape=[jax.ShapeDtypeStruct((b, r, CONV_END), F32),
                   jax.ShapeDtypeStruct((b, r, 2 * D_SGU), F32),
                   jax.ShapeDtypeStruct((b, r, D_ATTN), BF16),
                   jax.ShapeDtypeStruct((b, r, 512), BF16)],
        compiler_params=_cparams(2),
        name="in_proj",
    )(x, mod, norm_g, cos2, sin2, w_aug)


def _softmax_pair(s_list, sink):
    m = sink
    for s in s_list:
        m = jnp.maximum(m, jnp.max(s, axis=-1, keepdims=True))
    ps = [jnp.exp(s - m) for s in s_list]
    denom = jnp.exp(sink - m)
    for p in ps:
        denom = denom + jnp.sum(p, axis=-1, keepdims=True)
    return ps, 1.0 / denom


def _mix_kernel(x_ref, pc_ref, pcp_ref, pcn_ref, ps_ref, q_ref, kv_ref, kvp_ref, kvn_ref, kvc_ref,
                mod_ref, convw_ref, sgug_ref, sguw_ref, sgub_ref, sink_ref, mixg_ref, wout_ref, n2g_ref,
                xo_ref, h2_ref, kvx_ref, attn_ref, sgu_ref, *, local):
    tq = x_ref.shape[0]
    nblk = tq // BLOCK
    t = pl.program_id(1)
    nt = pl.num_programs(1)

    pc = pc_ref[...]
    z = pc[:, D_CONV:2 * D_CONV] * pc[:, 2 * D_CONV:]
    z_before = pcp_ref[7:8, D_CONV:2 * D_CONV] * pcp_ref[7:8, 2 * D_CONV:]
    z_after = pcn_ref[0:1, D_CONV:2 * D_CONV] * pcn_ref[0:1, 2 * D_CONV:]
    z_before = z_before * (t > 0).astype(F32)
    z_after = z_after * (t < nt - 1).astype(F32)
    rows = lax.broadcasted_iota(jnp.int32, (tq, D_CONV), 0)
    z_prev = jnp.where(rows == 0, z_before, pltpu.roll(z, 1, axis=0))
    z_next = jnp.where(rows == tq - 1, z_after, pltpu.roll(z, tq - 1, axis=0))
    conv = pc[:, :D_CONV] * (z_prev * convw_ref[0:1, :] + z * convw_ref[1:2, :] + z_next * convw_ref[2:3, :])

    zg = _gelu(ps_ref[...])
    u = zg[:, :D_SGU]
    v = zg[:, D_SGU:]
    mu = jnp.mean(v, axis=-1, keepdims=True)
    vc = v - mu
    vn = vc * lax.rsqrt(jnp.mean(vc * vc, axis=-1, keepdims=True) + EPS) * sgug_ref[...]
    lane = lax.broadcasted_iota(jnp.int32, (BLOCK, LANES), 1)
    low = lane < HEAD_DIM
    for cb in range(nblk):
        pieces = []
        for a in range(SGU_HEADS // 2):
            vp = vn[cb * BLOCK:(cb + 1) * BLOCK, a * LANES:(a + 1) * LANES]
            v_lo = jnp.where(low, vp, 0.0).astype(BF16)
            v_hi = jnp.where(low, 0.0, vp).astype(BF16)
            pieces.append(_dot(sguw_ref[2 * a], v_lo) + _dot(sguw_ref[2 * a + 1], v_hi))
        s = jnp.concatenate(pieces, axis=1) + sgub_ref[...]
        sgu_ref[cb * BLOCK:(cb + 1) * BLOCK, :] = u[cb * BLOCK:(cb + 1) * BLOCK, :] * s

    if local:
        kvx_ref[0:BLOCK, :] = kvp_ref[...]
        kvx_ref[BLOCK:BLOCK + tq, :] = kv_ref[...]
        kvx_ref[BLOCK + tq:, :] = kvn_ref[...]
    kvc = kvc_ref[...]
    qi = lax.broadcasted_iota(jnp.int32, (BLOCK, 3 * BLOCK), 0)
    ko = lax.broadcasted_iota(jnp.int32, (BLOCK, 3 * BLOCK), 1)
    band = (ko >= qi) & (ko <= qi + 2 * BLOCK)

    def attend(jb, carry):
        r0 = pl.multiple_of(jb * BLOCK, BLOCK)
        if local:
            n = t * nblk + jb
            first_key = jnp.where(n > 0, 0, BLOCK)
            end_key = jnp.where(n < nt * nblk - 1, 3 * BLOCK, 2 * BLOCK)
            ok = band & (ko >= first_key) & (ko < end_key)
            kvl = kvx_ref[pl.ds(r0, 3 * BLOCK), :]
        for a in range(N_HEADS // 2):
            qp = q_ref[pl.ds(r0, BLOCK), a * LANES:(a + 1) * LANES]
            q_lo = jnp.where(low, qp, jnp.zeros_like(qp))
            q_hi = jnp.where(low, jnp.zeros_like(qp), qp)
            first = a < N_HEADS // 4
            outs = []
            for qh, even in ((q_lo, True), (q_hi, False)):
                natural = first == even
                ksel = slice(0, 128) if natural else slice(128, 256)
                vsel = slice(256, 384) if natural else slice(384, 512)
                hd = 2 * a + (0 if even else 1)
                sink = sink_ref[hd:hd + 1, 0:1]
                s_list = [_dot_nt(qh, kvc[:, ksel])]
                v_list = [kvc[:, vsel]]
                if local:
                    s_list.append(jnp.where(ok, _dot_nt(qh, kvl[:, ksel]), -jnp.inf))
                    v_list.append(kvl[:, vsel])
                ps, rden = _softmax_pair(s_list, sink)
                o = _dot(ps[0].astype(BF16), v_list[0])
                for p, vv in zip(ps[1:], v_list[1:]):
                    o = o + _dot(p.astype(BF16), vv)
                outs.append(o * rden)
            attn_ref[pl.ds(r0, BLOCK), a * LANES:(a + 1) * LANES] = jnp.where(low, outs[0], outs[1])
        return carry

    lax.fori_loop(0, nblk, attend, 0)

    g = mixg_ref[...]
    yc = (_rms(conv) * g[:, :D_CONV]).astype(BF16)
    ys = (_rms(sgu_ref[...]) * g[:, D_CONV:D_CONV + D_SGU]).astype(BF16)
    ya = (_rms(attn_ref[...]) * g[:, D_CONV + D_SGU:]).astype(BF16)
    yl = (_dot(yc, wout_ref[0:D_CONV, :]) + _dot(ys, wout_ref[D_CONV:D_CONV + D_SGU, :])
          + _dot(ya, wout_ref[D_CONV + D_SGU:, :]))
    xn = x_ref[...] + mod_ref[2:3, :] * yl
    xo_ref[...] = xn
    h2 = _rms(xn) * n2g_ref[...] * (1.0 + mod_ref[4:5, :]) + mod_ref[3:4, :]
    h2_ref[...] = h2.astype(BF16)


def _mix_call(x, pc, ps, q, kv, kvc, mod, conv_w, sgu_g, sgu_w, sgu_b, sink_b, mix_g, w_out, n2g, *, local):
    b, r, d = x.shape
    c = kvc.shape[1]
    tq = min(ROW_TILE, r)
    nt = r // tq
    hb = tq // SUBLANES
    kb = tq // BLOCK
    row = lambda n: pl.BlockSpec((None, tq, n), lambda i, t: (i, t, 0))
    full = lambda shape: pl.BlockSpec(shape, lambda i, t: (0,) * len(shape))
    return pl.pallas_call(
        functools.partial(_mix_kernel, local=local),
        grid=(b, nt),
        in_specs=[row(d), row(CONV_END),
                  pl.BlockSpec((None, SUBLANES, CONV_END), lambda i, t: (i, jnp.maximum(t * hb - 1, 0), 0)),
                  pl.BlockSpec((None, SUBLANES, CONV_END), lambda i, t: (i, jnp.minimum((t + 1) * hb, nt * hb - 1), 0)),
                  row(2 * D_SGU), row(D_ATTN), row(512),
                  pl.BlockSpec((None, BLOCK, 512), lambda i, t: (i, jnp.maximum(t * kb - 1, 0), 0)),
                  pl.BlockSpec((None, BLOCK, 512), lambda i, t: (i, jnp.minimum((t + 1) * kb, nt * kb - 1), 0)),
                  pl.BlockSpec((None, c, 512), lambda i, t: (i, 0, 0)),
                  pl.BlockSpec((None, 6, d), lambda i, t: (i, 0, 0)),
                  full((3, D_CONV)), full((1, D_SGU)), full((SGU_HEADS, SGU_CHUNK, SGU_CHUNK)),
                  full((SGU_CHUNK, D_SGU)), full((N_HEADS, LANES)), full((1, d)), full((d, d)), full((1, d))],
        out_specs=[row(d), row(d)],
        out_shape=[jax.ShapeDtypeStruct((b, r, d), F32), jax.ShapeDtypeStruct((b, r, d), BF16)],
        scratch_shapes=[pltpu.VMEM((tq + 2 * BLOCK, 512), BF16),
                        pltpu.VMEM((tq, D_ATTN), F32),
                        pltpu.VMEM((tq, D_SGU), F32)],
        compiler_params=_cparams(2),
        name="mixers_local" if local else "mixers_ctx",
    )(x, pc, pc, pc, ps, q, kv, kv, kv, kvc, mod, conv_w, sgu_g, sgu_w, sgu_b, sink_b, mix_g, w_out, n2g)


def _oddeven_merge(lo, hi, r):
    step = r * 2
    if step < hi - lo:
        yield from _oddeven_merge(lo, hi, step)
        yield from _oddeven_merge(lo + r, hi, step)
        yield from [(i, i + r) for i in range(lo + r, hi - r, step)]
    else:
        yield (lo, lo + r)


def _oddeven_sort(lo, hi):
    if hi - lo >= 1:
        mid = lo + (hi - lo) // 2
        yield from _oddeven_sort(lo, mid)
        yield from _oddeven_sort(mid + 1, hi)
        yield from _oddeven_merge(lo, hi, 1)


_SORT16 = tuple(_oddeven_sort(0, PEER_TOPK - 1))


def _sort16(x):
    x = list(x)
    for i, j in _SORT16:
        x[i], x[j] = jnp.maximum(x[i], x[j]), jnp.minimum(x[i], x[j])
    return x


def _merge_top16(a, b):
    n = PEER_TOPK
    c = [jnp.maximum(a[i], b[n - 1 - i]) for i in range(n)]
    d = n // 2
    while d:
        for i in range(n):
            if not i & d:
                c[i], c[i + d] = jnp.maximum(c[i], c[i + d]), jnp.minimum(c[i], c[i + d])
        d //= 2
    return c


def _top16(load):
    def tree(lo, n):
        if n == PEER_TOPK:
            return _sort16([load(lo + i) for i in range(n)])
        return _merge_top16(tree(lo, n // 2), tree(lo + n // 2, n // 2))
    return tree(0, N_KEYS)


def _peer_select(h_ref, wq_ref, k0_ref, k1p_ref, k1_ref, ht_ref, e0_ref, e1_ref, th_ref,
                 s0_ref, s1_ref, s1h_ref, a0_ref, b0_ref, rz_ref):
    tm = h_ref.shape[0]
    nh = PEER_HEADS
    half = nh * PEER_DHALF
    ht_ref[...] = h_ref[...].astype(F32).T.astype(BF16)
    ht = ht_ref[...]
    q0 = _dot(wq_ref[0:half, :], ht).astype(BF16)
    s0_ref[...] = _dot(k0_ref[...], q0)
    q1 = _dot(wq_ref[half:2 * half, :], ht).astype(BF16)
    s1_ref[...] = _dot(k1p_ref[...], q1)
    for hh in range(nh):
        s1h_ref[hh] = _dot(k1_ref[hh], q1[hh * PEER_DHALF:(hh + 1) * PEER_DHALF, :])

    rnd = lambda v: v.astype(BF16).astype(F32)

    def select(lc, carry):
        lanes = pl.ds(pl.multiple_of(lc * LANES, LANES), LANES)
        a = _top16(lambda i: s0_ref[i * nh:(i + 1) * nh, lanes])
        b = _top16(lambda j: s1_ref[j * nh:(j + 1) * nh, lanes])
        ea = [jnp.exp(v - a[0]) for v in a]
        eb = [jnp.exp(v - b[0]) for v in b]
        cand = [ea[k] * eb[l] for k, l in _CAND]
        rest = cand[PEER_TOPK:]
        rest = rest + [jnp.full_like(cand[0], -1.0)] * (-len(rest) % PEER_TOPK)
        best = cand[:PEER_TOPK]
        for g in range(0, len(rest), PEER_TOPK):
            best = _merge_top16(best, _sort16(rest[g:g + PEER_TOPK]))
        top = best[PEER_TOPK - 1]
        zsum = jnp.zeros_like(top)
        for p in cand:
            zsum = zsum + jnp.where(p >= top, p, 0.0)
        rz = 1.0 / zsum
        ean = [rnd(v * rz) for v in ea]
        ebn = [rnd(v) for v in eb]
        thn = jnp.full_like(top, jnp.inf)
        for (k, l), p in zip(_CAND, cand):
            thn = jnp.minimum(thn, jnp.where(p >= top, rnd(ean[k] * ebn[l]), jnp.inf))
        for hh in range(nh):
            th_ref[hh, :, lanes] = jnp.broadcast_to(thn[hh:hh + 1, :], (2 * SUBLANES, LANES)).astype(BF16)
        a0_ref[:, lanes] = a[0]
        b0_ref[:, lanes] = b[0]
        rz_ref[:, lanes] = rz
        return carry

    lax.fori_loop(0, tm // LANES, select, 0)

    a0 = a0_ref[...]
    rz = rz_ref[...]
    for i in range(N_KEYS):
        e0_ref[i] = jnp.exp(s0_ref[i * nh:(i + 1) * nh, :] - a0) * rz
    for hh in range(nh):
        e1_ref[hh] = jnp.exp(s1h_ref[hh] - b0_ref[hh:hh + 1, :]).astype(BF16)


def _gelu_sigmoid_form(x):
    k0 = -2.0 * math.sqrt(2.0 / math.pi) * math.log2(math.e)
    k1 = 0.044715 * k0
    return x / (1.0 + jnp.exp2(x * (x * x * k1 + k0)))


def _peer_gate(a, row0, rows, e0_ref, e1_ref, th_ref, hbuf_ref, lanes):
    n_lanes = lanes.stop - lanes.start
    pack = 2 * SUBLANES
    for ii in range(a.shape[0] // N_KEYS):
        e0 = e0_ref[row0 + rows.start // N_KEYS + ii, :, lanes]
        e0r = [jnp.broadcast_to(e0[hh:hh + 1, :], (pack, n_lanes)).astype(BF16) for hh in range(PEER_HEADS)]
        for c in range(N_KEYS // pack):
            gate = None
            for hh in range(PEER_HEADS):
                p = e0r[hh] * e1_ref[hh, c * pack:(c + 1) * pack, lanes]
                sel = jnp.where(p >= th_ref[hh, :, lanes], p, jnp.zeros_like(p))
                gate = sel if gate is None else gate + sel
            r0 = ii * N_KEYS + c * pack
            act = _gelu_sigmoid_form(a[r0:r0 + pack, :].astype(BF16))
            hbuf_ref[rows.start + r0:rows.start + r0 + pack, lanes] = act * gate


def _peer_kernel(h_ref, x_ref, mod_ref, fg_ref, wq_ref, k0_ref, k1p_ref, k1_ref, u_ref, vt_ref, o_ref,
                 ht_ref, e0_ref, e1_ref, th_ref, s0_ref, s1_ref, s1h_ref, a0_ref, b0_ref, rz_ref,
                 hbuf_ref, acc_ref, *, final_norm):
    e = pl.program_id(2)
    ne = pl.num_programs(2)
    tm = h_ref.shape[0]
    row0 = e * (u_ref.shape[0] // N_KEYS)

    @pl.when(e == 0)
    def _():
        _peer_select(h_ref, wq_ref, k0_ref, k1p_ref, k1_ref, ht_ref, e0_ref, e1_ref, th_ref,
                     s0_ref, s1_ref, s1h_ref, a0_ref, b0_ref, rz_ref)
        acc_ref[...] = jnp.zeros_like(acc_ref)

    eb = u_ref.shape[0]
    n_split = 2 if tm % (2 * 2 * LANES) == 0 else 1
    lane_ranges = [slice(s * (tm // n_split), (s + 1) * (tm // n_split)) for s in range(n_split)]
    row_ranges = [slice(0, eb // 2), slice(eb // 2, eb)]
    for lanes in lane_ranges:
        for rows in row_ranges:
            a = _dot(u_ref[rows, :], ht_ref[:, lanes])
            _peer_gate(a, row0, rows, e0_ref, e1_ref, th_ref, hbuf_ref, lanes)
    for lanes in lane_ranges:
        acc_ref[:, lanes] += _dot(vt_ref[...], hbuf_ref[:, lanes])

    @pl.when(e == ne - 1)
    def _():
        y = x_ref[...] + mod_ref[5:6, :] * acc_ref[...].T
        if final_norm:
            y = _rms(y) * fg_ref[...]
        o_ref[...] = y


def _peer_call(h2, x, mod, final_g, wq_t, k0p, k1p, k1, u_b, vt_b, *, final_norm):
    b, r, d = x.shape
    n_exp = u_b.shape[0]
    tm = min(PEER_TOKENS, r)
    eb = PEER_EXPERTS
    nh = PEER_HEADS
    row = lambda: pl.BlockSpec((None, tm, d), lambda i, t, e: (i, t, 0))
    full = lambda shape: pl.BlockSpec(shape, lambda i, t, e: (0,) * len(shape))
    return pl.pallas_call(
        functools.partial(_peer_kernel, final_norm=final_norm),
        grid=(b, r // tm, n_exp // eb),
        in_specs=[row(), row(),
                  pl.BlockSpec((None, 6, d), lambda i, t, e: (i, 0, 0)),
                  full((1, d)), full(wq_t.shape), full(k0p.shape), full(k1p.shape), full(k1.shape),
                  pl.BlockSpec((eb, d), lambda i, t, e: (e, 0)),
                  pl.BlockSpec((d, eb), lambda i, t, e: (0, e))],
        out_specs=row(),
        out_shape=jax.ShapeDtypeStruct((b, r, d), F32),
        scratch_shapes=[pltpu.VMEM((d, tm), BF16),
                        pltpu.VMEM((N_KEYS, nh, tm), F32),
                        pltpu.VMEM((nh, N_KEYS, tm), BF16),
                        pltpu.VMEM((nh, 2 * SUBLANES, tm), BF16),
                        pltpu.VMEM((N_KEYS * nh, tm), F32),
                        pltpu.VMEM((N_KEYS * nh, tm), F32),
                        pltpu.VMEM((nh, N_KEYS, tm), F32),
                        pltpu.VMEM((nh, tm), F32),
                        pltpu.VMEM((nh, tm), F32),
                        pltpu.VMEM((nh, tm), F32),
                        pltpu.VMEM((eb, tm), BF16),
                        pltpu.VMEM((d, tm), F32)],
        compiler_params=_cparams(3),
        name="peer_final" if final_norm else "peer",
    )(h2, x, mod, final_g, wq_t, k0p, k1p, k1, u_b, vt_b)


def _rope_tables(length):
    rows = length // GRID_W
    row = jnp.repeat(jnp.arange(rows), GRID_W).astype(F32)
    col = jnp.tile(jnp.arange(GRID_W), rows).astype(F32)
    inv = ROPE_THETA ** (-jnp.arange(ROPE_FREQS, dtype=F32) / ROPE_FREQS)
    ar = row[:, None] * inv[None, :]
    ac = col[:, None] * inv[None, :]
    ang = jnp.concatenate([ar, ar, ac, ac, ar, ar, ac, ac], axis=-1)
    return jnp.cos(ang), jnp.sin(ang)


def _rot_cols(w, heads):
    r = w.reshape(w.shape[0], heads, 4, ROPE_FREQS)
    return jnp.stack([-r[:, :, 1], r[:, :, 0], -r[:, :, 3], r[:, :, 2]], axis=2).reshape(w.shape)


def _swap_heads(w):
    return jnp.concatenate([w[:, HEAD_DIM:], w[:, :HEAD_DIM]], axis=1)


def _augment_w_in(w):
    wq = w[:, SGU_END:Q_END]
    wk = w[:, Q_END:K_END]
    wv = w[:, K_END:]
    wkr = _rot_cols(wk, N_KV_HEADS)
    cols = [w[:, :SGU_END], wq, _rot_cols(wq, N_HEADS), wk, _swap_heads(wk), wkr, _swap_heads(wkr),
            wv, _swap_heads(wv)]
    return jnp.concatenate(cols, axis=1).astype(BF16)


def kernel(x, c, ctx, c_ctx, w_ada, b_ada, norm1_g, norm2_g, w_in, conv_w, sgu_norm_g, sgu_w, sgu_b,
           attn_sink, mix_norm_g, w_out, peer_wq, peer_keys, peer_u, peer_v, final_g):
    bsz, length, d = x.shape
    n_ctx = ctx.shape[1]
    depth = w_ada.shape[0]
    nh = PEER_HEADS

    cc = jnp.zeros((SUBLANES, d), F32).at[:bsz].set(c).at[bsz].set(c_ctx)
    mod = _mod_call(cc, w_ada, b_ada)

    cos_l, sin_l = _rope_tables(length)
    cos_c = jnp.ones((n_ctx, 2 * HEAD_DIM), F32)
    sin_c = jnp.zeros((n_ctx, 2 * HEAD_DIM), F32)
    fg = final_g.reshape(1, d)

    xl, xc = x, ctx
    for i in range(depth):
        last = i == depth - 1
        mod_l = mod[i, :bsz].reshape(bsz, 6, d)
        mod_c = jnp.broadcast_to(mod[i, bsz].reshape(1, 6, d), (bsz, 6, d))
        n1g = norm1_g[i].reshape(1, d)
        n2g = norm2_g[i].reshape(1, d)
        w_aug = _augment_w_in(w_in[i])
        sgu_g = sgu_norm_g[i].reshape(1, D_SGU)
        sgu_wb = sgu_w[i].astype(BF16)
        sgu_bias = jnp.repeat(sgu_b[i].T, D_SGU // SGU_HEADS, axis=1)
        sink_b = jnp.broadcast_to(attn_sink[i][:, None], (N_HEADS, LANES))
        mix_g = mix_norm_g[i].reshape(1, d)
        w_out_b = w_out[i].astype(BF16)
        wq_t = peer_wq[i].reshape(d, nh, 2, PEER_DHALF).transpose(2, 1, 3, 0).reshape(2 * nh * PEER_DHALF, d)
        wq_t = wq_t.astype(BF16)
        eye = jnp.eye(nh, dtype=F32)
        k0p = jnp.einsum('hid,hg->ihgd', peer_keys[i][:, 0], eye).reshape(N_KEYS * nh, nh * PEER_DHALF)
        k1p = jnp.einsum('hid,hg->ihgd', peer_keys[i][:, 1], eye).reshape(N_KEYS * nh, nh * PEER_DHALF)
        k0p = k0p.astype(BF16)
        k1p = k1p.astype(BF16)
        k1 = peer_keys[i][:, 1].astype(BF16)
        u_b = peer_u[i].astype(BF16)
        vt_b = peer_v[i].T.astype(BF16)
        mixer_w = (conv_w[i], sgu_g, sgu_wb, sgu_bias, sink_b, mix_g, w_out_b, n2g)

        pc_c, ps_c, q_c, kv_c = _in_call(xc, mod_c, n1g, cos_c, sin_c, w_aug)
        pc_l, ps_l, q_l, kv_l = _in_call(xl, mod_l, n1g, cos_l, sin_l, w_aug)
        xl, h2_l = _mix_call(xl, pc_l, ps_l, q_l, kv_l, kv_c, mod_l, *mixer_w, local=True)
        if not last:
            xc, h2_c = _mix_call(xc, pc_c, ps_c, q_c, kv_c, kv_c, mod_c, *mixer_w, local=False)
            xc = _peer_call(h2_c, xc, mod_c, fg, wq_t, k0p, k1p, k1, u_b, vt_b, final_norm=False)
        xl = _peer_call(h2_l, xl, mod_l, fg, wq_t, k0p, k1p, k1, u_b, vt_b, final_norm=last)
    return xl
```

```python
import functools
import math

import jax
import jax.numpy as jnp
from jax import lax
from jax.experimental import pallas as pl
from jax.experimental.pallas import tpu as pltpu

F32 = jnp.float32
BF16 = jnp.bfloat16

EPS = 1e-6
GRID_W = 64
D_CONV = 256
D_SGU = 256
SGU_HEADS = 4
SGU_CHUNK = 128
N_HEADS = 8
N_KV_HEADS = 2
HEAD_DIM = 64
D_ATTN = N_HEADS * HEAD_DIM
BLOCK = 128
ROPE_THETA = 10000.0
ROPE_FREQS = HEAD_DIM // 4
CONV_END = 3 * D_CONV
SGU_END = CONV_END + 2 * D_SGU
Q_END = SGU_END + D_ATTN
K_END = Q_END + N_KV_HEADS * HEAD_DIM
N_KEYS = 128
PEER_HEADS = 8
PEER_TOPK = 16
PEER_DHALF = 128

LANES = 128
SUBLANES = 8
VMEM_LIMIT_BYTES = 56 * 1024 * 1024

ROW_TILE = 512
PEER_TOKENS = 512
PEER_EXPERTS = 1024

_QO = SGU_END
_QR = _QO + D_ATTN
_KA = _QR + D_ATTN
_KB = _KA + 128
_KAR = _KB + 128
_KBR = _KAR + 128
_VA = _KBR + 128
_VB = _VA + 128
D_AUG = _VB + 128

_CAND = [(k, l) for k in range(PEER_TOPK) for l in range(PEER_TOPK) if (k + 1) * (l + 1) <= PEER_TOPK]


def _cparams(n_axes):
    return pltpu.CompilerParams(dimension_semantics=("arbitrary",) * n_axes,
                                vmem_limit_bytes=VMEM_LIMIT_BYTES)


def _gelu(x):
    c = math.sqrt(2.0 / math.pi)
    return 0.5 * x * (1.0 + jnp.tanh(c * (x + 0.044715 * (x * x * x))))


def _dot(a, b):
    return jnp.dot(a, b, preferred_element_type=F32)


def _dot_nt(a, b):
    return lax.dot_general(a, b, (((1,), (1,)), ((), ())), preferred_element_type=F32)


def _rms(x):
    return x * lax.rsqrt(jnp.mean(x * x, axis=-1, keepdims=True) + EPS)


def _mod_kernel(c_ref, w_ref, b_ref, o_ref):
    c = c_ref[...]
    sc = c / (1.0 + jnp.exp(-c))
    w = w_ref[...]
    c_hi = sc.astype(BF16)
    c_lo = (sc - c_hi.astype(F32)).astype(BF16)
    w_hi = w.astype(BF16)
    w_lo = (w - w_hi.astype(F32)).astype(BF16)
    o_ref[...] = _dot(c_hi, w_hi) + _dot(c_lo, w_hi) + _dot(c_hi, w_lo) + b_ref[...]


def _mod_call(cc, w_ada, b_ada):
    depth, d, n = w_ada.shape
    tn = 1536
    return pl.pallas_call(
        _mod_kernel,
        grid=(depth, n // tn),
        in_specs=[pl.BlockSpec((SUBLANES, d), lambda i, j: (0, 0)),
                  pl.BlockSpec((None, d, tn), lambda i, j: (i, 0, j)),
                  pl.BlockSpec((None, 1, tn), lambda i, j: (i, 0, j))],
        out_specs=pl.BlockSpec((None, SUBLANES, tn), lambda i, j: (i, 0, j)),
        out_shape=jax.ShapeDtypeStruct((depth, SUBLANES, n), F32),
        compiler_params=_cparams(2),
        name="adaln_mod",
    )(cc, w_ada, b_ada.reshape(depth, 1, n))


def _in_kernel(x_ref, mod_ref, g_ref, cos_ref, sin_ref, w_ref, pc_ref, ps_ref, q_ref, kv_ref):
    x = x_ref[...]
    h = _rms(x) * g_ref[...] * (1.0 + mod_ref[1:2, :]) + mod_ref[0:1, :]
    hb = h.astype(BF16)

    def proj(lo, hi):
        return _dot(hb, w_ref[:, lo:hi])

    pc_ref[...] = proj(0, CONV_END)
    ps_ref[...] = proj(CONV_END, SGU_END)
    cos = cos_ref[...]
    sin = sin_ref[...]
    cos4 = jnp.concatenate([cos] * 4, axis=1)
    sin4 = jnp.concatenate([sin] * 4, axis=1)
    scale = HEAD_DIM ** -0.5
    q = (proj(_QO, _QO + D_ATTN) * cos4 + proj(_QR, _QR + D_ATTN) * sin4) * scale
    q_ref[...] = q.astype(BF16)
    ka = proj(_KA, _KA + 128) * cos + proj(_KAR, _KAR + 128) * sin
    kb = proj(_KB, _KB + 128) * cos + proj(_KBR, _KBR + 128) * sin
    kv_ref[:, 0:128] = ka.astype(BF16)
    kv_ref[:, 128:256] = kb.astype(BF16)
    kv_ref[:, 256:512] = proj(_VA, _VA + 256).astype(BF16)


def _in_call(x, mod, norm_g, cos2, sin2, w_aug):
    b, r, d = x.shape
    tm = min(ROW_TILE, r)
    row = lambda n: pl.BlockSpec((None, tm, n), lambda i, t: (i, t, 0))
    return pl.pallas_call(
        _in_kernel,
        grid=(b, r // tm),
        in_specs=[row(d),
                  pl.BlockSpec((None, 6, d), lambda i, t: (i, 0, 0)),
                  pl.BlockSpec((1, d), lambda i, t: (0, 0)),
                  pl.BlockSpec((tm, 128), lambda i, t: (t, 0)),
                  pl.BlockSpec((tm, 128), lambda i, t: (t, 0)),
                  pl.BlockSpec((d, D_AUG), lambda i, t: (0, 0))],
        out_specs=[row(CONV_END), row(2 * D_SGU), row(D_ATTN), row(512)],
        out_shape=[jax.ShapeDtypeStruct((b, r, CONV_END), F32),
                   jax.ShapeDtypeStruct((b, r, 2 * D_SGU), F32),
                   jax.ShapeDtypeStruct((b, r, D_ATTN), BF16),
                   jax.ShapeDtypeStruct((b, r, 512), BF16)],
        compiler_params=_cparams(2),
        name="in_proj",
    )(x, mod, norm_g, cos2, sin2, w_aug)


def _softmax_pair(s_list, sink):
    m = sink
    for s in s_list:
        m = jnp.maximum(m, jnp.max(s, axis=-1, keepdims=True))
    ps = [jnp.exp(s - m) for s in s_list]
    denom = jnp.exp(sink - m)
    for p in ps:
        denom = denom + jnp.sum(p, axis=-1, keepdims=True)
    return ps, 1.0 / denom


def _mix_kernel(x_ref, pc_ref, pcp_ref, pcn_ref, ps_ref, q_ref, kv_ref, kvp_ref, kvn_ref, kvc_ref,
                mod_ref, convw_ref, sgug_ref, sguw_ref, sgub_ref, sink_ref, mixg_ref, wout_ref, n2g_ref,
                xo_ref, h2_ref, kvx_ref, attn_ref, sgu_ref, *, local):
    tq = x_ref.shape[0]
    nblk = tq // BLOCK
    t = pl.program_id(1)
    nt = pl.num_programs(1)

    pc = pc_ref[...]
    z = pc[:, D_CONV:2 * D_CONV] * pc[:, 2 * D_CONV:]
    z_before = pcp_ref[7:8, D_CONV:2 * D_CONV] * pcp_ref[7:8, 2 * D_CONV:]
    z_after = pcn_ref[0:1, D_CONV:2 * D_CONV] * pcn_ref[0:1, 2 * D_CONV:]
    z_before = z_before * (t > 0).astype(F32)
    z_after = z_after * (t < nt - 1).astype(F32)
    rows = lax.broadcasted_iota(jnp.int32, (tq, D_CONV), 0)
    z_prev = jnp.where(rows == 0, z_before, pltpu.roll(z, 1, axis=0))
    z_next = jnp.where(rows == tq - 1, z_after, pltpu.roll(z, tq - 1, axis=0))
    conv = pc[:, :D_CONV] * (z_prev * convw_ref[0:1, :] + z * convw_ref[1:2, :] + z_next * convw_ref[2:3, :])

    zg = _gelu(ps_ref[...])
    u = zg[:, :D_SGU]
    v = zg[:, D_SGU:]
    mu = jnp.mean(v, axis=-1, keepdims=True)
    vc = v - mu
    vn = vc * lax.rsqrt(jnp.mean(vc * vc, axis=-1, keepdims=True) + EPS) * sgug_ref[...]
    lane = lax.broadcasted_iota(jnp.int32, (BLOCK, LANES), 1)
    low = lane < HEAD_DIM
    for cb in range(nblk):
        pieces = []
        for a in range(SGU_HEADS // 2):
            vp = vn[cb * BLOCK:(cb + 1) * BLOCK, a * LANES:(a + 1) * LANES]
            v_lo = jnp.where(low, vp, 0.0).astype(BF16)
            v_hi = jnp.where(low, 0.0, vp).astype(BF16)
            pieces.append(_dot(sguw_ref[2 * a], v_lo) + _dot(sguw_ref[2 * a + 1], v_hi))
        s = jnp.concatenate(pieces, axis=1) + sgub_ref[...]
        sgu_ref[cb * BLOCK:(cb + 1) * BLOCK, :] = u[cb * BLOCK:(cb + 1) * BLOCK, :] * s

    if local:
        kvx_ref[0:BLOCK, :] = kvp_ref[...]
        kvx_ref[BLOCK:BLOCK + tq, :] = kv_ref[...]
        kvx_ref[BLOCK + tq:, :] = kvn_ref[...]
    kvc = kvc_ref[...]
    qi = lax.broadcasted_iota(jnp.int32, (BLOCK, 3 * BLOCK), 0)
    ko = lax.broadcasted_iota(jnp.int32, (BLOCK, 3 * BLOCK), 1)
    band = (ko >= qi) & (ko <= qi + 2 * BLOCK)

    def attend(jb, carry):
        r0 = pl.multiple_of(jb * BLOCK, BLOCK)
        if local:
            n = t * nblk + jb
            first_key = jnp.where(n > 0, 0, BLOCK)
            end_key = jnp.where(n < nt * nblk - 1, 3 * BLOCK, 2 * BLOCK)
            ok = band & (ko >= first_key) & (ko < end_key)
            kvl = kvx_ref[pl.ds(r0, 3 * BLOCK), :]
        for a in range(N_HEADS // 2):
            qp = q_ref[pl.ds(r0, BLOCK), a * LANES:(a + 1) * LANES]
            q_lo = jnp.where(low, qp, jnp.zeros_like(qp))
            q_hi = jnp.where(low, jnp.zeros_like(qp), qp)
            first = a < N_HEADS // 4
            outs = []
            for qh, even in ((q_lo, True), (q_hi, False)):
                natural = first == even
                ksel = slice(0, 128) if natural else slice(128, 256)
                vsel = slice(256, 384) if natural else slice(384, 512)
                hd = 2 * a + (0 if even else 1)
                sink = sink_ref[hd:hd + 1, 0:1]
                s_list = [_dot_nt(qh, kvc[:, ksel])]
                v_list = [kvc[:, vsel]]
                if local:
                    s_list.append(jnp.where(ok, _dot_nt(qh, kvl[:, ksel]), -jnp.inf))
                    v_list.append(kvl[:, vsel])
                ps, rden = _softmax_pair(s_list, sink)
                o = _dot(ps[0].astype(BF16), v_list[0])
                for p, vv in zip(ps[1:], v_list[1:]):
                    o = o + _dot(p.astype(BF16), vv)
                outs.append(o * rden)
            attn_ref[pl.ds(r0, BLOCK), a * LANES:(a + 1) * LANES] = jnp.where(low, outs[0], outs[1])
        return carry

    lax.fori_loop(0, nblk, attend, 0)

    g = mixg_ref[...]
    yc = (_rms(conv) * g[:, :D_CONV]).astype(BF16)
    ys = (_rms(sgu_ref[...]) * g[:, D_CONV:D_CONV + D_SGU]).astype(BF16)
    ya = (_rms(attn_ref[...]) * g[:, D_CONV + D_SGU:]).astype(BF16)
    yl = (_dot(yc, wout_ref[0:D_CONV, :]) + _dot(ys, wout_ref[D_CONV:D_CONV + D_SGU, :])
          + _dot(ya, wout_ref[D_CONV + D_SGU:, :]))
    xn = x_ref[...] + mod_ref[2:3, :] * yl
    xo_ref[...] = xn
    h2 = _rms(xn) * n2g_ref[...] * (1.0 + mod_ref[4:5, :]) + mod_ref[3:4, :]
    h2_ref[...] = h2.astype(BF16)


def _mix_call(x, pc, ps, q, kv, kvc, mod, conv_w, sgu_g, sgu_w, sgu_b, sink_b, mix_g, w_out, n2g, *, local):
    b, r, d = x.shape
    c = kvc.shape[1]
    tq = min(ROW_TILE, r)
    nt = r // tq
    hb = tq // SUBLANES
    kb = tq // BLOCK
    row = lambda n: pl.BlockSpec((None, tq, n), lambda i, t: (i, t, 0))
    full = lambda shape: pl.BlockSpec(shape, lambda i, t: (0,) * len(shape))
    return pl.pallas_call(
        functools.partial(_mix_kernel, local=local),
        grid=(b, nt),
        in_specs=[row(d), row(CONV_END),
                  pl.BlockSpec((None, SUBLANES, CONV_END), lambda i, t: (i, jnp.maximum(t * hb - 1, 0), 0)),
                  pl.BlockSpec((None, SUBLANES, CONV_END), lambda i, t: (i, jnp.minimum((t + 1) * hb, nt * hb - 1), 0)),
                  row(2 * D_SGU), row(D_ATTN), row(512),
                  pl.BlockSpec((None, BLOCK, 512), lambda i, t: (i, jnp.maximum(t * kb - 1, 0), 0)),
                  pl.BlockSpec((None, BLOCK, 512), lambda i, t: (i, jnp.minimum((t + 1) * kb, nt * kb - 1), 0)),
                  pl.BlockSpec((None, c, 512), lambda i, t: (i, 0, 0)),
                  pl.BlockSpec((None, 6, d), lambda i, t: (i, 0, 0)),
                  full((3, D_CONV)), full((1, D_SGU)), full((SGU_HEADS, SGU_CHUNK, SGU_CHUNK)),
                  full((SGU_CHUNK, D_SGU)), full((N_HEADS, LANES)), full((1, d)), full((d, d)), full((1, d))],
        out_specs=[row(d), row(d)],
        out_shape=[jax.ShapeDtypeStruct((b, r, d), F32), jax.ShapeDtypeStruct((b, r, d), BF16)],
        scratch_shapes=[pltpu.VMEM((tq + 2 * BLOCK, 512), BF16),
                        pltpu.VMEM((tq, D_ATTN), F32),
                        pltpu.VMEM((tq, D_SGU), F32)],
        compiler_params=_cparams(2),
        name="mixers_local" if local else "mixers_ctx",
    )(x, pc, pc, pc, ps, q, kv, kv, kv, kvc, mod, conv_w, sgu_g, sgu_w, sgu_b, sink_b, mix_g, w_out, n2g)


def _oddeven_merge(lo, hi, r):
    step = r * 2
    if step < hi - lo:
        yield from _oddeven_merge(lo, hi, step)
        yield from _oddeven_merge(lo + r, hi, step)
        yield from [(i, i + r) for i in range(lo + r, hi - r, step)]
    else:
        yield (lo, lo + r)


def _oddeven_sort(lo, hi):
    if hi - lo >= 1:
        mid = lo + (hi - lo) // 2
        yield from _oddeven_sort(lo, mid)
        yield from _oddeven_sort(mid + 1, hi)
        yield from _oddeven_merge(lo, hi, 1)


_SORT16 = tuple(_oddeven_sort(0, PEER_TOPK - 1))


def _sort16(x):
    x = list(x)
    for i, j in _SORT16:
        x[i], x[j] = jnp.maximum(x[i], x[j]), jnp.minimum(x[i], x[j])
    return x


def _merge_top16(a, b):
    n = PEER_TOPK
    c = [jnp.maximum(a[i], b[n - 1 - i]) for i in range(n)]
    d = n // 2
    while d:
        for i in range(n):
            if not i & d:
                c[i], c[i + d] = jnp.maximum(c[i], c[i + d]), jnp.minimum(c[i], c[i + d])
        d //= 2
    return c


def _top16(load):
    def tree(lo, n):
        if n == PEER_TOPK:
            return _sort16([load(lo + i) for i in range(n)])
        return _merge_top16(tree(lo, n // 2), tree(lo + n // 2, n // 2))
    return tree(0, N_KEYS)


def _peer_select(h_ref, wq_ref, k0_ref, k1p_ref, k1_ref, ht_ref, e0_ref, e1_ref, th_ref,
                 s0_ref, s1_ref, s1h_ref, a0_ref, b0_ref, rz_ref):
    tm = h_ref.shape[0]
    nh = PEER_HEADS
    half = nh * PEER_DHALF
    ht_ref[...] = h_ref[...].astype(F32).T.astype(BF16)
    ht = ht_ref[...]
    q0 = _dot(wq_ref[0:half, :], ht).astype(BF16)
    s0_ref[...] = _dot(k0_ref[...], q0)
    q1 = _dot(wq_ref[half:2 * half, :], ht).astype(BF16)
    s1_ref[...] = _dot(k1p_ref[...], q1)
    for hh in range(nh):
        s1h_ref[hh] = _dot(k1_ref[hh], q1[hh * PEER_DHALF:(hh + 1) * PEER_DHALF, :])

    rnd = lambda v: v.astype(BF16).astype(F32)

    def select(lc, carry):
        lanes = pl.ds(pl.multiple_of(lc * LANES, LANES), LANES)
        a = _top16(lambda i: s0_ref[i * nh:(i + 1) * nh, lanes])
        b = _top16(lambda j: s1_ref[j * nh:(j + 1) * nh, lanes])
        ea = [jnp.exp(v - a[0]) for v in a]
        eb = [jnp.exp(v - b[0]) for v in b]
        cand = [ea[k] * eb[l] for k, l in _CAND]
        rest = cand[PEER_TOPK:]
        rest = rest + [jnp.full_like(cand[0], -1.0)] * (-len(rest) % PEER_TOPK)
        best = cand[:PEER_TOPK]
        for g in range(0, len(rest), PEER_TOPK):
            best = _merge_top16(best, _sort16(rest[g:g + PEER_TOPK]))
        top = best[PEER_TOPK - 1]
        zsum = jnp.zeros_like(top)
        for p in cand:
            zsum = zsum + jnp.where(p >= top, p, 0.0)
        rz = 1.0 / zsum
        ean = [rnd(v * rz) for v in ea]
        ebn = [rnd(v) for v in eb]
        thn = jnp.full_like(top, jnp.inf)
        for (k, l), p in zip(_CAND, cand):
            thn = jnp.minimum(thn, jnp.where(p >= top, rnd(ean[k] * ebn[l]), jnp.inf))
        for hh in range(nh):
            th_ref[hh, :, lanes] = jnp.broadcast_to(thn[hh:hh + 1, :], (2 * SUBLANES, LANES)).astype(BF16)
        a0_ref[:, lanes] = a[0]
        b0_ref[:, lanes] = b[0]
        rz_ref[:, lanes] = rz
        return carry

    lax.fori_loop(0, tm // LANES, select, 0)

    a0 = a0_ref[...]
    rz = rz_ref[...]
    for i in range(N_KEYS):
        e0_ref[i] = jnp.exp(s0_ref[i * nh:(i + 1) * nh, :] - a0) * rz
    for hh in range(nh):
        e1_ref[hh] = jnp.exp(s1h_ref[hh] - b0_ref[hh:hh + 1, :]).astype(BF16)


def _gelu_sigmoid_form(x):
    k0 = -2.0 * math.sqrt(2.0 / math.pi) * math.log2(math.e)
    k1 = 0.044715 * k0
    return x / (1.0 + jnp.exp2(x * (x * x * k1 + k0)))


def _peer_gate(a_ref, row0, e0_ref, e1_ref, th_ref, hbuf_ref, lanes):
    n_lanes = lanes.stop - lanes.start
    pack = 2 * SUBLANES
    for ii in range(a_ref.shape[0] // N_KEYS):
        e0 = e0_ref[row0 + ii, :, lanes]
        e0r = [jnp.broadcast_to(e0[hh:hh + 1, :], (pack, n_lanes)).astype(BF16) for hh in range(PEER_HEADS)]
        for c in range(N_KEYS // pack):
            gate = None
            for hh in range(PEER_HEADS):
                p = e0r[hh] * e1_ref[hh, c * pack:(c + 1) * pack, lanes]
                sel = jnp.where(p >= th_ref[hh, :, lanes], p, jnp.zeros_like(p))
                gate = sel if gate is None else gate + sel
            r0 = ii * N_KEYS + c * pack
            act = _gelu_sigmoid_form(a_ref[r0:r0 + pack, lanes].astype(BF16))
            hbuf_ref[r0:r0 + pack, lanes] = act * gate


def _peer_kernel(h_ref, x_ref, mod_ref, fg_ref, wq_ref, k0_ref, k1p_ref, k1_ref, u0_ref, u_ref, vt_ref, o_ref,
                 ht_ref, e0_ref, e1_ref, th_ref, s0_ref, s1_ref, s1h_ref, a0_ref, b0_ref, rz_ref,
                 acur_ref, anext_ref, hbuf_ref, acc_ref, *, final_norm):
    e = pl.program_id(2)
    ne = pl.num_programs(2)
    tm = h_ref.shape[0]
    row0 = e * (u_ref.shape[0] // N_KEYS)

    @pl.when(e == 0)
    def _():
        _peer_select(h_ref, wq_ref, k0_ref, k1p_ref, k1_ref, ht_ref, e0_ref, e1_ref, th_ref,
                     s0_ref, s1_ref, s1h_ref, a0_ref, b0_ref, rz_ref)
        acur_ref[...] = _dot(u0_ref[...], ht_ref[...])
        acc_ref[...] = jnp.zeros_like(acc_ref)

    n_split = 2 if tm % (2 * 2 * LANES) == 0 else 1
    lane_ranges = [slice(s * (tm // n_split), (s + 1) * (tm // n_split)) for s in range(n_split)]
    for lanes in lane_ranges:
        anext_ref[:, lanes] = _dot(u_ref[...], ht_ref[:, lanes])
        _peer_gate(acur_ref, row0, e0_ref, e1_ref, th_ref, hbuf_ref, lanes)
        acc_ref[:, lanes] += _dot(vt_ref[...], hbuf_ref[:, lanes])
    acur_ref[...] = anext_ref[...]

    @pl.when(e == ne - 1)
    def _():
        y = x_ref[...] + mod_ref[5:6, :] * acc_ref[...].T
        if final_norm:
            y = _rms(y) * fg_ref[...]
        o_ref[...] = y


def _peer_call(h2, x, mod, final_g, wq_t, k0p, k1p, k1, u_b, vt_b, *, final_norm):
    b, r, d = x.shape
    n_exp = u_b.shape[0]
    tm = min(PEER_TOKENS, r)
    eb = PEER_EXPERTS
    nh = PEER_HEADS
    row = lambda: pl.BlockSpec((None, tm, d), lambda i, t, e: (i, t, 0))
    full = lambda shape: pl.BlockSpec(shape, lambda i, t, e: (0,) * len(shape))
    return pl.pallas_call(
        functools.partial(_peer_kernel, final_norm=final_norm),
        grid=(b, r // tm, n_exp // eb),
        in_specs=[row(), row(),
                  pl.BlockSpec((None, 6, d), lambda i, t, e: (i, 0, 0)),
                  full((1, d)), full(wq_t.shape), full(k0p.shape), full(k1p.shape), full(k1.shape),
                  pl.BlockSpec((eb, d), lambda i, t, e: (0, 0)),
                  pl.BlockSpec((eb, d), lambda i, t, e: (jnp.minimum(e + 1, n_exp // eb - 1), 0)),
                  pl.BlockSpec((d, eb), lambda i, t, e: (0, e))],
        out_specs=row(),
        out_shape=jax.ShapeDtypeStruct((b, r, d), F32),
        scratch_shapes=[pltpu.VMEM((d, tm), BF16),
                        pltpu.VMEM((N_KEYS, nh, tm), F32),
                        pltpu.VMEM((nh, N_KEYS, tm), BF16),
                        pltpu.VMEM((nh, 2 * SUBLANES, tm), BF16),
                        pltpu.VMEM((N_KEYS * nh, tm), F32),
                        pltpu.VMEM((N_KEYS * nh, tm), F32),
                        pltpu.VMEM((nh, N_KEYS, tm), F32),
                        pltpu.VMEM((nh, tm), F32),
                        pltpu.VMEM((nh, tm), F32),
                        pltpu.VMEM((nh, tm), F32),
                        pltpu.VMEM((eb, tm), F32),
                        pltpu.VMEM((eb, tm), F32),
                        pltpu.VMEM((eb, tm), BF16),
                        pltpu.VMEM((d, tm), F32)],
        compiler_params=_cparams(3),
        name="peer_final" if final_norm else "peer",
    )(h2, x, mod, final_g, wq_t, k0p, k1p, k1, u_b, u_b, vt_b)


def _rope_tables(length):
    rows = length // GRID_W
    row = jnp.repeat(jnp.arange(rows), GRID_W).astype(F32)
    col = jnp.tile(jnp.arange(GRID_W), rows).astype(F32)
    inv = ROPE_THETA ** (-jnp.arange(ROPE_FREQS, dtype=F32) / ROPE_FREQS)
    ar = row[:, None] * inv[None, :]
    ac = col[:, None] * inv[None, :]
    ang = jnp.concatenate([ar, ar, ac, ac, ar, ar, ac, ac], axis=-1)
    return jnp.cos(ang), jnp.sin(ang)


def _rot_cols(w, heads):
    r = w.reshape(w.shape[0], heads, 4, ROPE_FREQS)
    return jnp.stack([-r[:, :, 1], r[:, :, 0], -r[:, :, 3], r[:, :, 2]], axis=2).reshape(w.shape)


def _swap_heads(w):
    return jnp.concatenate([w[:, HEAD_DIM:], w[:, :HEAD_DIM]], axis=1)


def _augment_w_in(w):
    wq = w[:, SGU_END:Q_END]
    wk = w[:, Q_END:K_END]
    wv = w[:, K_END:]
    wkr = _rot_cols(wk, N_KV_HEADS)
    cols = [w[:, :SGU_END], wq, _rot_cols(wq, N_HEADS), wk, _swap_heads(wk), wkr, _swap_heads(wkr),
            wv, _swap_heads(wv)]
    return jnp.concatenate(cols, axis=1).astype(BF16)


def kernel(x, c, ctx, c_ctx, w_ada, b_ada, norm1_g, norm2_g, w_in, conv_w, sgu_norm_g, sgu_w, sgu_b,
           attn_sink, mix_norm_g, w_out, peer_wq, peer_keys, peer_u, peer_v, final_g):
    bsz, length, d = x.shape
    n_ctx = ctx.shape[1]
    depth = w_ada.shape[0]
    nh = PEER_HEADS

    cc = jnp.zeros((SUBLANES, d), F32).at[:bsz].set(c).at[bsz].set(c_ctx)
    mod = _mod_call(cc, w_ada, b_ada)

    cos_l, sin_l = _rope_tables(length)
    cos_c = jnp.ones((n_ctx, 2 * HEAD_DIM), F32)
    sin_c = jnp.zeros((n_ctx, 2 * HEAD_DIM), F32)
    fg = final_g.reshape(1, d)

    xl, xc = x, ctx
    for i in range(depth):
        last = i == depth - 1
        mod_l = mod[i, :bsz].reshape(bsz, 6, d)
        mod_c = jnp.broadcast_to(mod[i, bsz].reshape(1, 6, d), (bsz, 6, d))
        n1g = norm1_g[i].reshape(1, d)
        n2g = norm2_g[i].reshape(1, d)
        w_aug = _augment_w_in(w_in[i])
        sgu_g = sgu_norm_g[i].reshape(1, D_SGU)
        sgu_wb = sgu_w[i].astype(BF16)
        sgu_bias = jnp.repeat(sgu_b[i].T, D_SGU // SGU_HEADS, axis=1)
        sink_b = jnp.broadcast_to(attn_sink[i][:, None], (N_HEADS, LANES))
        mix_g = mix_norm_g[i].reshape(1, d)
        w_out_b = w_out[i].astype(BF16)
        wq_t = peer_wq[i].reshape(d, nh, 2, PEER_DHALF).transpose(2, 1, 3, 0).reshape(2 * nh * PEER_DHALF, d)
        wq_t = wq_t.astype(BF16)
        eye = jnp.eye(nh, dtype=F32)
        k0p = jnp.einsum('hid,hg->ihgd', peer_keys[i][:, 0], eye).reshape(N_KEYS * nh, nh * PEER_DHALF)
        k1p = jnp.einsum('hid,hg->ihgd', peer_keys[i][:, 1], eye).reshape(N_KEYS * nh, nh * PEER_DHALF)
        k0p = k0p.astype(BF16)
        k1p = k1p.astype(BF16)
        k1 = peer_keys[i][:, 1].astype(BF16)
        u_b = peer_u[i].astype(BF16)
        vt_b = peer_v[i].T.astype(BF16)
        mixer_w = (conv_w[i], sgu_g, sgu_wb, sgu_bias, sink_b, mix_g, w_out_b, n2g)

        pc_c, ps_c, q_c, kv_c = _in_call(xc, mod_c, n1g, cos_c, sin_c, w_aug)
        pc_l, ps_l, q_l, kv_l = _in_call(xl, mod_l, n1g, cos_l, sin_l, w_aug)
        xl, h2_l = _mix_call(xl, pc_l, ps_l, q_l, kv_l, kv_c, mod_l, *mixer_w, local=True)
        if not last:
            xc, h2_c = _mix_call(xc, pc_c, ps_c, q_c, kv_c, kv_c, mod_c, *mixer_w, local=False)
            xc = _peer_call(h2_c, xc, mod_c, fg, wq_t, k0p, k1p, k1, u_b, vt_b, final_norm=False)
        xl = _peer_call(h2_l, xl, mod_l, fg, wq_t, k0p, k1p, k1, u_b, vt_b, final_norm=last)
    return xl
```

```python
import functools
import math

import jax
import jax.numpy as jnp
from jax import lax
from jax.experimental import pallas as pl
from jax.experimental.pallas import tpu as pltpu

F32 = jnp.float32
BF16 = jnp.bfloat16

EPS = 1e-6
GRID_W = 64
D_CONV = 256
D_SGU = 256
SGU_HEADS = 4
SGU_CHUNK = 128
N_HEADS = 8
N_KV_HEADS = 2
HEAD_DIM = 64
D_ATTN = N_HEADS * HEAD_DIM
BLOCK = 128
ROPE_THETA = 10000.0
ROPE_FREQS = HEAD_DIM // 4
CONV_END = 3 * D_CONV
SGU_END = CONV_END + 2 * D_SGU
Q_END = SGU_END + D_ATTN
K_END = Q_END + N_KV_HEADS * HEAD_DIM
N_KEYS = 128
PEER_HEADS = 8
PEER_TOPK = 16
PEER_DHALF = 128

LANES = 128
SUBLANES = 8
VMEM_LIMIT_BYTES = 56 * 1024 * 1024

ROW_TILE = 512
PEER_TOKENS = 512
PEER_EXPERTS = 1024

_QO = SGU_END
_QR = _QO + D_ATTN
_KA = _QR + D_ATTN
_KB = _KA + 128
_KAR = _KB + 128
_KBR = _KAR + 128
_VA = _KBR + 128
_VB = _VA + 128
D_AUG = _VB + 128

_CAND = [(k, l) for k in range(PEER_TOPK) for l in range(PEER_TOPK) if (k + 1) * (l + 1) <= PEER_TOPK]


def _cparams(n_axes):
    return pltpu.CompilerParams(dimension_semantics=("arbitrary",) * n_axes,
                                vmem_limit_bytes=VMEM_LIMIT_BYTES)


def _gelu(x):
    c = math.sqrt(2.0 / math.pi)
    return 0.5 * x * (1.0 + jnp.tanh(c * (x + 0.044715 * (x * x * x))))


def _dot(a, b):
    return jnp.dot(a, b, preferred_element_type=F32)


def _dot_nt(a, b):
    return lax.dot_general(a, b, (((1,), (1,)), ((), ())), preferred_element_type=F32)


def _rms(x):
    return x * lax.rsqrt(jnp.mean(x * x, axis=-1, keepdims=True) + EPS)


def _mod_kernel(c_ref, w_ref, b_ref, o_ref):
    c = c_ref[...]
    sc = c / (1.0 + jnp.exp(-c))
    w = w_ref[...]
    c_hi = sc.astype(BF16)
    c_lo = (sc - c_hi.astype(F32)).astype(BF16)
    w_hi = w.astype(BF16)
    w_lo = (w - w_hi.astype(F32)).astype(BF16)
    o_ref[...] = _dot(c_hi, w_hi) + _dot(c_lo, w_hi) + _dot(c_hi, w_lo) + b_ref[...]


def _mod_call(cc, w_ada, b_ada):
    depth, d, n = w_ada.shape
    tn = 1536
    return pl.pallas_call(
        _mod_kernel,
        grid=(depth, n // tn),
        in_specs=[pl.BlockSpec((SUBLANES, d), lambda i, j: (0, 0)),
                  pl.BlockSpec((None, d, tn), lambda i, j: (i, 0, j)),
                  pl.BlockSpec((None, 1, tn), lambda i, j: (i, 0, j))],
        out_specs=pl.BlockSpec((None, SUBLANES, tn), lambda i, j: (i, 0, j)),
        out_shape=jax.ShapeDtypeStruct((depth, SUBLANES, n), F32),
        compiler_params=_cparams(2),
        name="adaln_mod",
    )(cc, w_ada, b_ada.reshape(depth, 1, n))


def _in_kernel(x_ref, mod_ref, g_ref, cos_ref, sin_ref, w_ref, pc_ref, ps_ref, q_ref, kv_ref):
    x = x_ref[...]
    h = _rms(x) * g_ref[...] * (1.0 + mod_ref[1:2, :]) + mod_ref[0:1, :]
    hb = h.astype(BF16)

    def proj(lo, hi):
        return _dot(hb, w_ref[:, lo:hi])

    pc_ref[...] = proj(0, CONV_END)
    ps_ref[...] = proj(CONV_END, SGU_END)
    cos = cos_ref[...]
    sin = sin_ref[...]
    cos4 = jnp.concatenate([cos] * 4, axis=1)
    sin4 = jnp.concatenate([sin] * 4, axis=1)
    scale = HEAD_DIM ** -0.5
    q = (proj(_QO, _QO + D_ATTN) * cos4 + proj(_QR, _QR + D_ATTN) * sin4) * scale
    q_ref[...] = q.astype(BF16)
    ka = proj(_KA, _KA + 128) * cos + proj(_KAR, _KAR + 128) * sin
    kb = proj(_KB, _KB + 128) * cos + proj(_KBR, _KBR + 128) * sin
    kv_ref[:, 0:128] = ka.astype(BF16)
    kv_ref[:, 128:256] = kb.astype(BF16)
    kv_ref[:, 256:512] = proj(_VA, _VA + 256).astype(BF16)


def _in_call(x, mod, norm_g, cos2, sin2, w_aug):
    b, r, d = x.shape
    tm = min(ROW_TILE, r)
    row = lambda n: pl.BlockSpec((None, tm, n), lambda i, t: (i, t, 0))
    return pl.pallas_call(
        _in_kernel,
        grid=(b, r // tm),
        in_specs=[row(d),
                  pl.BlockSpec((None, 6, d), lambda i, t: (i, 0, 0)),
                  pl.BlockSpec((1, d), lambda i, t: (0, 0)),
                  pl.BlockSpec((tm, 128), lambda i, t: (t, 0)),
                  pl.BlockSpec((tm, 128), lambda i, t: (t, 0)),
                  pl.BlockSpec((d, D_AUG), lambda i, t: (0, 0))],
        out_specs=[row(CONV_END), row(2 * D_SGU), row(D_ATTN), row(512)],
        out_shape=[jax.ShapeDtypeStruct((b, r, CONV_END), F32),
                   jax.ShapeDtypeStruct((b, r, 2 * D_SGU), F32),
                   jax.ShapeDtypeStruct((b, r, D_ATTN), BF16),
                   jax.ShapeDtypeStruct((b, r, 512), BF16)],
        compiler_params=_cparams(2),
        name="in_proj",
    )(x, mod, norm_g, cos2, sin2, w_aug)


def _fold_lanes(blocks, op):
    parts = [b[:, c:c + LANES] for b in blocks for c in range(0, b.shape[1], LANES)]
    out = parts[0]
    for p in parts[1:]:
        out = op(out, p)
    return out


def _mix_kernel(x_ref, pc_ref, pcp_ref, pcn_ref, ps_ref, q_ref, kv_ref, kvp_ref, kvn_ref, kvc_ref,
                mod_ref, convw_ref, sgug_ref, sguw_ref, sgub_ref, sink_ref, mixg_ref, wout_ref, n2g_ref,
                xo_ref, h2_ref, kvx_ref, attn_ref, sgu_ref, *, local):
    tq = x_ref.shape[0]
    nblk = tq // BLOCK
    t = pl.program_id(1)
    nt = pl.num_programs(1)

    pc = pc_ref[...]
    z = pc[:, D_CONV:2 * D_CONV] * pc[:, 2 * D_CONV:]
    z_before = pcp_ref[7:8, D_CONV:2 * D_CONV] * pcp_ref[7:8, 2 * D_CONV:]
    z_after = pcn_ref[0:1, D_CONV:2 * D_CONV] * pcn_ref[0:1, 2 * D_CONV:]
    z_before = z_before * (t > 0).astype(F32)
    z_after = z_after * (t < nt - 1).astype(F32)
    rows = lax.broadcasted_iota(jnp.int32, (tq, D_CONV), 0)
    z_prev = jnp.where(rows == 0, z_before, pltpu.roll(z, 1, axis=0))
    z_next = jnp.where(rows == tq - 1, z_after, pltpu.roll(z, tq - 1, axis=0))
    conv = pc[:, :D_CONV] * (z_prev * convw_ref[0:1, :] + z * convw_ref[1:2, :] + z_next * convw_ref[2:3, :])

    zg = _gelu(ps_ref[...])
    u = zg[:, :D_SGU]
    v = zg[:, D_SGU:]
    mu = jnp.mean(v, axis=-1, keepdims=True)
    vc = v - mu
    vn = vc * lax.rsqrt(jnp.mean(vc * vc, axis=-1, keepdims=True) + EPS) * sgug_ref[...]
    lane = lax.broadcasted_iota(jnp.int32, (BLOCK, LANES), 1)
    low = lane < HEAD_DIM
    for cb in range(nblk):
        pieces = []
        for a in range(SGU_HEADS // 2):
            vp = vn[cb * BLOCK:(cb + 1) * BLOCK, a * LANES:(a + 1) * LANES]
            v_lo = jnp.where(low, vp, 0.0).astype(BF16)
            v_hi = jnp.where(low, 0.0, vp).astype(BF16)
            pieces.append(_dot(sguw_ref[2 * a], v_lo) + _dot(sguw_ref[2 * a + 1], v_hi))
        s = jnp.concatenate(pieces, axis=1) + sgub_ref[...]
        sgu_ref[cb * BLOCK:(cb + 1) * BLOCK, :] = u[cb * BLOCK:(cb + 1) * BLOCK, :] * s

    if local:
        kvx_ref[0:BLOCK, :] = kvp_ref[...]
        kvx_ref[BLOCK:BLOCK + tq, :] = kv_ref[...]
        kvx_ref[BLOCK + tq:, :] = kvn_ref[...]
    kvc = kvc_ref[...]
    qi = lax.broadcasted_iota(jnp.int32, (BLOCK, 3 * BLOCK), 0)
    ko = lax.broadcasted_iota(jnp.int32, (BLOCK, 3 * BLOCK), 1)
    band = (ko >= qi) & (ko <= qi + 2 * BLOCK)

    def attend(jb, carry):
        r0 = pl.multiple_of(jb * BLOCK, BLOCK)
        if local:
            n = t * nblk + jb
            first_key = jnp.where(n > 0, 0, BLOCK)
            end_key = jnp.where(n < nt * nblk - 1, 3 * BLOCK, 2 * BLOCK)
            ok = band & (ko >= first_key) & (ko < end_key)
            kvl = kvx_ref[pl.ds(r0, 3 * BLOCK), :]
        heads = []
        for a in range(N_HEADS // 2):
            qp = q_ref[pl.ds(r0, BLOCK), a * LANES:(a + 1) * LANES]
            q_lo = jnp.where(low, qp, jnp.zeros_like(qp))
            q_hi = jnp.where(low, jnp.zeros_like(qp), qp)
            first = a < N_HEADS // 4
            for qh, even in ((q_lo, True), (q_hi, False)):
                natural = first == even
                ksel = slice(0, 128) if natural else slice(128, 256)
                vsel = slice(256, 384) if natural else slice(384, 512)
                s_list = [_dot_nt(qh, kvc[:, ksel])]
                v_list = [kvc[:, vsel]]
                if local:
                    s_list.append(jnp.where(ok, _dot_nt(qh, kvl[:, ksel]), -jnp.inf))
                    v_list.append(kvl[:, vsel])
                heads.append((s_list, v_list))
        probs = []
        for hd, (s_list, _) in enumerate(heads):
            sink = sink_ref[hd:hd + 1, 0:1]
            m = jnp.maximum(sink, jnp.max(_fold_lanes(s_list, jnp.maximum), axis=-1, keepdims=True))
            ps = [jnp.exp(s - m) for s in s_list]
            denom = jnp.exp(sink - m) + jnp.sum(_fold_lanes(ps, jnp.add), axis=-1, keepdims=True)
            probs.append(([p.astype(BF16) for p in ps], 1.0 / denom))
        outs = []
        for (ps, rden), (_, v_list) in zip(probs, heads):
            o = _dot(ps[0], v_list[0])
            for p, vv in zip(ps[1:], v_list[1:]):
                o = o + _dot(p, vv)
            outs.append(o * rden)
        for a in range(N_HEADS // 2):
            attn_ref[pl.ds(r0, BLOCK), a * LANES:(a + 1) * LANES] = jnp.where(low, outs[2 * a], outs[2 * a + 1])
        return carry

    lax.fori_loop(0, nblk, attend, 0)

    g = mixg_ref[...]
    yc = (_rms(conv) * g[:, :D_CONV]).astype(BF16)
    ys = (_rms(sgu_ref[...]) * g[:, D_CONV:D_CONV + D_SGU]).astype(BF16)
    ya = (_rms(attn_ref[...]) * g[:, D_CONV + D_SGU:]).astype(BF16)
    yl = (_dot(yc, wout_ref[0:D_CONV, :]) + _dot(ys, wout_ref[D_CONV:D_CONV + D_SGU, :])
          + _dot(ya, wout_ref[D_CONV + D_SGU:, :]))
    xn = x_ref[...] + mod_ref[2:3, :] * yl
    xo_ref[...] = xn
    h2 = _rms(xn) * n2g_ref[...] * (1.0 + mod_ref[4:5, :]) + mod_ref[3:4, :]
    h2_ref[...] = h2.astype(BF16)


def _mix_call(x, pc, ps, q, kv, kvc, mod, conv_w, sgu_g, sgu_w, sgu_b, sink_b, mix_g, w_out, n2g, *, local):
    b, r, d = x.shape
    c = kvc.shape[1]
    tq = min(ROW_TILE, r)
    nt = r // tq
    hb = tq // SUBLANES
    kb = tq // BLOCK
    row = lambda n: pl.BlockSpec((None, tq, n), lambda i, t: (i, t, 0))
    full = lambda shape: pl.BlockSpec(shape, lambda i, t: (0,) * len(shape))
    return pl.pallas_call(
        functools.partial(_mix_kernel, local=local),
        grid=(b, nt),
        in_specs=[row(d), row(CONV_END),
                  pl.BlockSpec((None, SUBLANES, CONV_END), lambda i, t: (i, jnp.maximum(t * hb - 1, 0), 0)),
                  pl.BlockSpec((None, SUBLANES, CONV_END), lambda i, t: (i, jnp.minimum((t + 1) * hb, nt * hb - 1), 0)),
                  row(2 * D_SGU), row(D_ATTN), row(512),
                  pl.BlockSpec((None, BLOCK, 512), lambda i, t: (i, jnp.maximum(t * kb - 1, 0), 0)),
                  pl.BlockSpec((None, BLOCK, 512), lambda i, t: (i, jnp.minimum((t + 1) * kb, nt * kb - 1), 0)),
                  pl.BlockSpec((None, c, 512), lambda i, t: (i, 0, 0)),
                  pl.BlockSpec((None, 6, d), lambda i, t: (i, 0, 0)),
                  full((3, D_CONV)), full((1, D_SGU)), full((SGU_HEADS, SGU_CHUNK, SGU_CHUNK)),
                  full((SGU_CHUNK, D_SGU)), full((N_HEADS, LANES)), full((1, d)), full((d, d)), full((1, d))],
        out_specs=[row(d), row(d)],
        out_shape=[jax.ShapeDtypeStruct((b, r, d), F32), jax.ShapeDtypeStruct((b, r, d), BF16)],
        scratch_shapes=[pltpu.VMEM((tq + 2 * BLOCK, 512), BF16),
                        pltpu.VMEM((tq, D_ATTN), F32),
                        pltpu.VMEM((tq, D_SGU), F32)],
        compiler_params=_cparams(2),
        name="mixers_local" if local else "mixers_ctx",
    )(x, pc, pc, pc, ps, q, kv, kv, kv, kvc, mod, conv_w, sgu_g, sgu_w, sgu_b, sink_b, mix_g, w_out, n2g)


def _oddeven_merge(lo, hi, r):
    step = r * 2
    if step < hi - lo:
        yield from _oddeven_merge(lo, hi, step)
        yield from _oddeven_merge(lo + r, hi, step)
        yield from [(i, i + r) for i in range(lo + r, hi - r, step)]
    else:
        yield (lo, lo + r)


def _oddeven_sort(lo, hi):
    if hi - lo >= 1:
        mid = lo + (hi - lo) // 2
        yield from _oddeven_sort(lo, mid)
        yield from _oddeven_sort(mid + 1, hi)
        yield from _oddeven_merge(lo, hi, 1)


_SORT16 = tuple(_oddeven_sort(0, PEER_TOPK - 1))


def _sort16(x):
    x = list(x)
    for i, j in _SORT16:
        x[i], x[j] = jnp.maximum(x[i], x[j]), jnp.minimum(x[i], x[j])
    return x


def _merge_top16(a, b):
    n = PEER_TOPK
    c = [jnp.maximum(a[i], b[n - 1 - i]) for i in range(n)]
    d = n // 2
    while d:
        for i in range(n):
            if not i & d:
                c[i], c[i + d] = jnp.maximum(c[i], c[i + d]), jnp.minimum(c[i], c[i + d])
        d //= 2
    return c


def _top16(load):
    def tree(lo, n):
        if n == PEER_TOPK:
            return _sort16([load(lo + i) for i in range(n)])
        return _merge_top16(tree(lo, n // 2), tree(lo + n // 2, n // 2))
    return tree(0, N_KEYS)


def _peer_select(h_ref, wq_ref, k0_ref, k1p_ref, k1_ref, ht_ref, e0_ref, e1_ref, th_ref,
                 s0_ref, s1_ref, s1h_ref, a0_ref, b0_ref, rz_ref):
    tm = h_ref.shape[0]
    nh = PEER_HEADS
    half = nh * PEER_DHALF
    ht_ref[...] = h_ref[...].astype(F32).T.astype(BF16)
    ht = ht_ref[...]
    q0 = _dot(wq_ref[0:half, :], ht).astype(BF16)
    s0_ref[...] = _dot(k0_ref[...], q0)
    q1 = _dot(wq_ref[half:2 * half, :], ht).astype(BF16)
    s1_ref[...] = _dot(k1p_ref[...], q1)
    for hh in range(nh):
        s1h_ref[hh] = _dot(k1_ref[hh], q1[hh * PEER_DHALF:(hh + 1) * PEER_DHALF, :])

    rnd = lambda v: v.astype(BF16).astype(F32)

    def select(lc, carry):
        lanes = pl.ds(pl.multiple_of(lc * LANES, LANES), LANES)
        a = _top16(lambda i: s0_ref[i * nh:(i + 1) * nh, lanes])
        b = _top16(lambda j: s1_ref[j * nh:(j + 1) * nh, lanes])
        ea = [jnp.exp(v - a[0]) for v in a]
        eb = [jnp.exp(v - b[0]) for v in b]
        cand = [ea[k] * eb[l] for k, l in _CAND]
        rest = cand[PEER_TOPK:]
        rest = rest + [jnp.full_like(cand[0], -1.0)] * (-len(rest) % PEER_TOPK)
        best = cand[:PEER_TOPK]
        for g in range(0, len(rest), PEER_TOPK):
            best = _merge_top16(best, _sort16(rest[g:g + PEER_TOPK]))
        top = best[PEER_TOPK - 1]
        zsum = jnp.zeros_like(top)
        for p in cand:
            zsum = zsum + jnp.where(p >= top, p, 0.0)
        rz = 1.0 / zsum
        ean = [rnd(v * rz) for v in ea]
        ebn = [rnd(v) for v in eb]
        thn = jnp.full_like(top, jnp.inf)
        for (k, l), p in zip(_CAND, cand):
            thn = jnp.minimum(thn, jnp.where(p >= top, rnd(ean[k] * ebn[l]), jnp.inf))
        for hh in range(nh):
            th_ref[hh, :, lanes] = jnp.broadcast_to(thn[hh:hh + 1, :], (2 * SUBLANES, LANES)).astype(BF16)
        a0_ref[:, lanes] = a[0]
        b0_ref[:, lanes] = b[0]
        rz_ref[:, lanes] = rz
        return carry

    lax.fori_loop(0, tm // LANES, select, 0)

    a0 = a0_ref[...]
    rz = rz_ref[...]
    for i in range(N_KEYS):
        e0_ref[i] = jnp.exp(s0_ref[i * nh:(i + 1) * nh, :] - a0) * rz
    for hh in range(nh):
        e1_ref[hh] = jnp.exp(s1h_ref[hh] - b0_ref[hh:hh + 1, :]).astype(BF16)


def _gelu_sigmoid_form(x):
    k0 = -2.0 * math.sqrt(2.0 / math.pi) * math.log2(math.e)
    k1 = 0.044715 * k0
    return x / (1.0 + jnp.exp2(x * (x * x * k1 + k0)))


def _peer_gate(a_ref, row0, e0_ref, e1_ref, th_ref, hbuf_ref, lanes):
    n_lanes = lanes.stop - lanes.start
    pack = 2 * SUBLANES
    for ii in range(a_ref.shape[0] // N_KEYS):
        e0 = e0_ref[row0 + ii, :, lanes]
        e0r = [jnp.broadcast_to(e0[hh:hh + 1, :], (pack, n_lanes)).astype(BF16) for hh in range(PEER_HEADS)]
        for c in range(N_KEYS // pack):
            gate = None
            for hh in range(PEER_HEADS):
                p = e0r[hh] * e1_ref[hh, c * pack:(c + 1) * pack, lanes]
                sel = jnp.where(p >= th_ref[hh, :, lanes], p, jnp.zeros_like(p))
                gate = sel if gate is None else gate + sel
            r0 = ii * N_KEYS + c * pack
            act = _gelu_sigmoid_form(a_ref[r0:r0 + pack, lanes].astype(BF16))
            hbuf_ref[r0:r0 + pack, lanes] = act * gate


def _peer_kernel(h_ref, x_ref, mod_ref, fg_ref, wq_ref, k0_ref, k1p_ref, k1_ref, u0_ref, u_ref, vt_ref, o_ref,
                 ht_ref, e0_ref, e1_ref, th_ref, s0_ref, s1_ref, s1h_ref, a0_ref, b0_ref, rz_ref,
                 acur_ref, anext_ref, hbuf_ref, acc_ref, *, final_norm):
    e = pl.program_id(2)
    ne = pl.num_programs(2)
    tm = h_ref.shape[0]
    row0 = e * (u_ref.shape[0] // N_KEYS)

    @pl.when(e == 0)
    def _():
        _peer_select(h_ref, wq_ref, k0_ref, k1p_ref, k1_ref, ht_ref, e0_ref, e1_ref, th_ref,
                     s0_ref, s1_ref, s1h_ref, a0_ref, b0_ref, rz_ref)
        acur_ref[...] = _dot(u0_ref[...], ht_ref[...])
        acc_ref[...] = jnp.zeros_like(acc_ref)

    n_split = 2 if tm % (2 * 2 * LANES) == 0 else 1
    lane_ranges = [slice(s * (tm // n_split), (s + 1) * (tm // n_split)) for s in range(n_split)]
    for lanes in lane_ranges:
        anext_ref[:, lanes] = _dot(u_ref[...], ht_ref[:, lanes])
        _peer_gate(acur_ref, row0, e0_ref, e1_ref, th_ref, hbuf_ref, lanes)
        acc_ref[:, lanes] += _dot(vt_ref[...], hbuf_ref[:, lanes])
    acur_ref[...] = anext_ref[...]

    @pl.when(e == ne - 1)
    def _():
        y = x_ref[...] + mod_ref[5:6, :] * acc_ref[...].T
        if final_norm:
            y = _rms(y) * fg_ref[...]
        o_ref[...] = y


def _peer_call(h2, x, mod, final_g, wq_t, k0p, k1p, k1, u_b, vt_b, *, final_norm):
    b, r, d = x.shape
    n_exp = u_b.shape[0]
    tm = min(PEER_TOKENS, r)
    eb = PEER_EXPERTS
    nh = PEER_HEADS
    row = lambda: pl.BlockSpec((None, tm, d), lambda i, t, e: (i, t, 0))
    full = lambda shape: pl.BlockSpec(shape, lambda i, t, e: (0,) * len(shape))
    return pl.pallas_call(
        functools.partial(_peer_kernel, final_norm=final_norm),
        grid=(b, r // tm, n_exp // eb),
        in_specs=[row(), row(),
                  pl.BlockSpec((None, 6, d), lambda i, t, e: (i, 0, 0)),
                  full((1, d)), full(wq_t.shape), full(k0p.shape), full(k1p.shape), full(k1.shape),
                  pl.BlockSpec((eb, d), lambda i, t, e: (0, 0)),
                  pl.BlockSpec((eb, d), lambda i, t, e: (jnp.minimum(e + 1, n_exp // eb - 1), 0)),
                  pl.BlockSpec((d, eb), lambda i, t, e: (0, e))],
        out_specs=row(),
        out_shape=jax.ShapeDtypeStruct((b, r, d), F32),
        scratch_shapes=[pltpu.VMEM((d, tm), BF16),
                        pltpu.VMEM((N_KEYS, nh, tm), F32),
                        pltpu.VMEM((nh, N_KEYS, tm), BF16),
                        pltpu.VMEM((nh, 2 * SUBLANES, tm), BF16),
                        pltpu.VMEM((N_KEYS * nh, tm), F32),
                        pltpu.VMEM((N_KEYS * nh, tm), F32),
                        pltpu.VMEM((nh, N_KEYS, tm), F32),
                        pltpu.VMEM((nh, tm), F32),
                        pltpu.VMEM((nh, tm), F32),
                        pltpu.VMEM((nh, tm), F32),
                        pltpu.VMEM((eb, tm), F32),
                        pltpu.VMEM((eb, tm), F32),
                        pltpu.VMEM((eb, tm), BF16),
                        pltpu.VMEM((d, tm), F32)],
        compiler_params=_cparams(3),
        name="peer_final" if final_norm else "peer",
    )(h2, x, mod, final_g, wq_t, k0p, k1p, k1, u_b, u_b, vt_b)


def _rope_tables(length):
    rows = length // GRID_W
    row = jnp.repeat(jnp.arange(rows), GRID_W).astype(F32)
    col = jnp.tile(jnp.arange(GRID_W), rows).astype(F32)
    inv = ROPE_THETA ** (-jnp.arange(ROPE_FREQS, dtype=F32) / ROPE_FREQS)
    ar = row[:, None] * inv[None, :]
    ac = col[:, None] * inv[None, :]
    ang = jnp.concatenate([ar, ar, ac, ac, ar, ar, ac, ac], axis=-1)
    return jnp.cos(ang), jnp.sin(ang)


def _rot_cols(w, heads):
    r = w.reshape(w.shape[0], heads, 4, ROPE_FREQS)
    return jnp.stack([-r[:, :, 1], r[:, :, 0], -r[:, :, 3], r[:, :, 2]], axis=2).reshape(w.shape)


def _swap_heads(w):
    return jnp.concatenate([w[:, HEAD_DIM:], w[:, :HEAD_DIM]], axis=1)


def _augment_w_in(w):
    wq = w[:, SGU_END:Q_END]
    wk = w[:, Q_END:K_END]
    wv = w[:, K_END:]
    wkr = _rot_cols(wk, N_KV_HEADS)
    cols = [w[:, :SGU_END], wq, _rot_cols(wq, N_HEADS), wk, _swap_heads(wk), wkr, _swap_heads(wkr),
            wv, _swap_heads(wv)]
    return jnp.concatenate(cols, axis=1).astype(BF16)


def kernel(x, c, ctx, c_ctx, w_ada, b_ada, norm1_g, norm2_g, w_in, conv_w, sgu_norm_g, sgu_w, sgu_b,
           attn_sink, mix_norm_g, w_out, peer_wq, peer_keys, peer_u, peer_v, final_g):
    bsz, length, d = x.shape
    n_ctx = ctx.shape[1]
    depth = w_ada.shape[0]
    nh = PEER_HEADS

    cc = jnp.zeros((SUBLANES, d), F32).at[:bsz].set(c).at[bsz].set(c_ctx)
    mod = _mod_call(cc, w_ada, b_ada)

    cos_l, sin_l = _rope_tables(length)
    cos_c = jnp.ones((n_ctx, 2 * HEAD_DIM), F32)
    sin_c = jnp.zeros((n_ctx, 2 * HEAD_DIM), F32)
    fg = final_g.reshape(1, d)

    xl, xc = x, ctx
    for i in range(depth):
        last = i == depth - 1
        mod_l = mod[i, :bsz].reshape(bsz, 6, d)
        mod_c = jnp.broadcast_to(mod[i, bsz].reshape(1, 6, d), (bsz, 6, d))
        n1g = norm1_g[i].reshape(1, d)
        n2g = norm2_g[i].reshape(1, d)
        w_aug = _augment_w_in(w_in[i])
        sgu_g = sgu_norm_g[i].reshape(1, D_SGU)
        sgu_wb = sgu_w[i].astype(BF16)
        sgu_bias = jnp.repeat(sgu_b[i].T, D_SGU // SGU_HEADS, axis=1)
        sink_b = jnp.broadcast_to(attn_sink[i][:, None], (N_HEADS, LANES))
        mix_g = mix_norm_g[i].reshape(1, d)
        w_out_b = w_out[i].astype(BF16)
        wq_t = peer_wq[i].reshape(d, nh, 2, PEER_DHALF).transpose(2, 1, 3, 0).reshape(2 * nh * PEER_DHALF, d)
        wq_t = wq_t.astype(BF16)
        eye = jnp.eye(nh, dtype=F32)
        k0p = jnp.einsum('hid,hg->ihgd', peer_keys[i][:, 0], eye).reshape(N_KEYS * nh, nh * PEER_DHALF)
        k1p = jnp.einsum('hid,hg->ihgd', peer_keys[i][:, 1], eye).reshape(N_KEYS * nh, nh * PEER_DHALF)
        k0p = k0p.astype(BF16)
        k1p = k1p.astype(BF16)
        k1 = peer_keys[i][:, 1].astype(BF16)
        u_b = peer_u[i].astype(BF16)
        vt_b = peer_v[i].T.astype(BF16)
        mixer_w = (conv_w[i], sgu_g, sgu_wb, sgu_bias, sink_b, mix_g, w_out_b, n2g)

        pc_c, ps_c, q_c, kv_c = _in_call(xc, mod_c, n1g, cos_c, sin_c, w_aug)
        pc_l, ps_l, q_l, kv_l = _in_call(xl, mod_l, n1g, cos_l, sin_l, w_aug)
        xl, h2_l = _mix_call(xl, pc_l, ps_l, q_l, kv_l, kv_c, mod_l, *mixer_w, local=True)
        if not last:
            xc, h2_c = _mix_call(xc, pc_c, ps_c, q_c, kv_c, kv_c, mod_c, *mixer_w, local=False)
            xc = _peer_call(h2_c, xc, mod_c, fg, wq_t, k0p, k1p, k1, u_b, vt_b, final_norm=False)
        xl = _peer_call(h2_l, xl, mod_l, fg, wq_t, k0p, k1p, k1, u_b, vt_b, final_norm=last)
    return xl
```

```python
import functools
import math

import jax
import jax.numpy as jnp
from jax import lax
from jax.experimental import pallas as pl
from jax.experimental.pallas import tpu as pltpu

F32 = jnp.float32
BF16 = jnp.bfloat16

EPS = 1e-6
GRID_W = 64
D_CONV = 256
D_SGU = 256
SGU_HEADS = 4
SGU_CHUNK = 128
N_HEADS = 8
N_KV_HEADS = 2
HEAD_DIM = 64
D_ATTN = N_HEADS * HEAD_DIM
BLOCK = 128
ROPE_THETA = 10000.0
ROPE_FREQS = HEAD_DIM // 4
CONV_END = 3 * D_CONV
SGU_END = CONV_END + 2 * D_SGU
Q_END = SGU_END + D_ATTN
K_END = Q_END + N_KV_HEADS * HEAD_DIM
N_KEYS = 128
PEER_HEADS = 8
PEER_TOPK = 16
PEER_DHALF = 128

LANES = 128
SUBLANES = 8
VMEM_LIMIT_BYTES = 56 * 1024 * 1024

ROW_TILE = 512
PEER_TOKENS = 512
PEER_EXPERTS = 1024

_QO = SGU_END
_QR = _QO + D_ATTN
_KA = _QR + D_ATTN
_KB = _KA + 128
_KAR = _KB + 128
_KBR = _KAR + 128
_VA = _KBR + 128
_VB = _VA + 128
D_AUG = _VB + 128

_CAND = [(k, l) for k in range(PEER_TOPK) for l in range(PEER_TOPK) if (k + 1) * (l + 1) <= PEER_TOPK]


def _cparams(n_axes):
    return pltpu.CompilerParams(dimension_semantics=("arbitrary",) * n_axes,
                                vmem_limit_bytes=VMEM_LIMIT_BYTES)


def _gelu(x):
    c = math.sqrt(2.0 / math.pi)
    return 0.5 * x * (1.0 + jnp.tanh(c * (x + 0.044715 * (x * x * x))))


def _dot(a, b):
    return jnp.dot(a, b, preferred_element_type=F32)


def _dot_nt(a, b):
    return lax.dot_general(a, b, (((1,), (1,)), ((), ())), preferred_element_type=F32)


def _rms(x):
    return x * lax.rsqrt(jnp.mean(x * x, axis=-1, keepdims=True) + EPS)


def _mod_kernel(c_ref, w_ref, b_ref, o_ref):
    c = c_ref[...]
    sc = c / (1.0 + jnp.exp(-c))
    w = w_ref[...]
    c_hi = sc.astype(BF16)
    c_lo = (sc - c_hi.astype(F32)).astype(BF16)
    w_hi = w.astype(BF16)
    w_lo = (w - w_hi.astype(F32)).astype(BF16)
    o_ref[...] = _dot(c_hi, w_hi) + _dot(c_lo, w_hi) + _dot(c_hi, w_lo) + b_ref[...]


def _mod_call(cc, w_ada, b_ada):
    depth, d, n = w_ada.shape
    tn = 1536
    return pl.pallas_call(
        _mod_kernel,
        grid=(depth, n // tn),
        in_specs=[pl.BlockSpec((SUBLANES, d), lambda i, j: (0, 0)),
                  pl.BlockSpec((None, d, tn), lambda i, j: (i, 0, j)),
                  pl.BlockSpec((None, 1, tn), lambda i, j: (i, 0, j))],
        out_specs=pl.BlockSpec((None, SUBLANES, tn), lambda i, j: (i, 0, j)),
        out_shape=jax.ShapeDtypeStruct((depth, SUBLANES, n), F32),
        compiler_params=_cparams(2),
        name="adaln_mod",
    )(cc, w_ada, b_ada.reshape(depth, 1, n))


def _in_kernel(x_ref, mod_ref, g_ref, cos_ref, sin_ref, w_ref, pc_ref, ps_ref, q_ref, kv_ref):
    x = x_ref[...]
    h = _rms(x) * g_ref[...] * (1.0 + mod_ref[1:2, :]) + mod_ref[0:1, :]
    hb = h.astype(BF16)

    def proj(lo, hi):
        return _dot(hb, w_ref[:, lo:hi])

    pc_ref[...] = proj(0, CONV_END)
    ps_ref[...] = proj(CONV_END, SGU_END)
    cos = cos_ref[...]
    sin = sin_ref[...]
    cos4 = jnp.concatenate([cos] * 4, axis=1)
    sin4 = jnp.concatenate([sin] * 4, axis=1)
    scale = HEAD_DIM ** -0.5
    q = (proj(_QO, _QO + D_ATTN) * cos4 + proj(_QR, _QR + D_ATTN) * sin4) * scale
    q_ref[...] = q.astype(BF16)
    ka = proj(_KA, _KA + 128) * cos + proj(_KAR, _KAR + 128) * sin
    kb = proj(_KB, _KB + 128) * cos + proj(_KBR, _KBR + 128) * sin
    kv_ref[:, 0:128] = ka.astype(BF16)
    kv_ref[:, 128:256] = kb.astype(BF16)
    kv_ref[:, 256:512] = proj(_VA, _VA + 256).astype(BF16)


def _in_call(x, mod, norm_g, cos2, sin2, w_aug):
    b, r, d = x.shape
    tm = min(ROW_TILE, r)
    row = lambda n: pl.BlockSpec((None, tm, n), lambda i, t: (i, t, 0))
    return pl.pallas_call(
        _in_kernel,
        grid=(b, r // tm),
        in_specs=[row(d),
                  pl.BlockSpec((None, 6, d), lambda i, t: (i, 0, 0)),
                  pl.BlockSpec((1, d), lambda i, t: (0, 0)),
                  pl.BlockSpec((tm, 128), lambda i, t: (t, 0)),
                  pl.BlockSpec((tm, 128), lambda i, t: (t, 0)),
                  pl.BlockSpec((d, D_AUG), lambda i, t: (0, 0))],
        out_specs=[row(CONV_END), row(2 * D_SGU), row(D_ATTN), row(512)],
        out_shape=[jax.ShapeDtypeStruct((b, r, CONV_END), F32),
                   jax.ShapeDtypeStruct((b, r, 2 * D_SGU), F32),
                   jax.ShapeDtypeStruct((b, r, D_ATTN), BF16),
                   jax.ShapeDtypeStruct((b, r, 512), BF16)],
        compiler_params=_cparams(2),
        name="in_proj",
    )(x, mod, norm_g, cos2, sin2, w_aug)


def _fold_lanes(blocks, op):
    parts = [b[:, c:c + LANES] for b in blocks for c in range(0, b.shape[1], LANES)]
    out = parts[0]
    for p in parts[1:]:
        out = op(out, p)
    return out


def _mix_kernel(x_ref, pc_ref, pcp_ref, pcn_ref, ps_ref, q_ref, kv_ref, kvp_ref, kvn_ref, kvc_ref,
                mod_ref, convw_ref, sgug_ref, sguw_ref, sgub_ref, sink_ref, mixg_ref, wout_ref, n2g_ref,
                xo_ref, h2_ref, kvx_ref, attn_ref, sgu_ref, *, local):
    tq = x_ref.shape[0]
    nblk = tq // BLOCK
    t = pl.program_id(1)
    nt = pl.num_programs(1)

    pc = pc_ref[...]
    z = pc[:, D_CONV:2 * D_CONV] * pc[:, 2 * D_CONV:]
    z_before = pcp_ref[7:8, D_CONV:2 * D_CONV] * pcp_ref[7:8, 2 * D_CONV:]
    z_after = pcn_ref[0:1, D_CONV:2 * D_CONV] * pcn_ref[0:1, 2 * D_CONV:]
    z_before = z_before * (t > 0).astype(F32)
    z_after = z_after * (t < nt - 1).astype(F32)
    rows = lax.broadcasted_iota(jnp.int32, (tq, D_CONV), 0)
    z_prev = jnp.where(rows == 0, z_before, pltpu.roll(z, 1, axis=0))
    z_next = jnp.where(rows == tq - 1, z_after, pltpu.roll(z, tq - 1, axis=0))
    conv = pc[:, :D_CONV] * (z_prev * convw_ref[0:1, :] + z * convw_ref[1:2, :] + z_next * convw_ref[2:3, :])

    zg = _gelu(ps_ref[...])
    u = zg[:, :D_SGU]
    v = zg[:, D_SGU:]
    mu = jnp.mean(v, axis=-1, keepdims=True)
    vc = v - mu
    vn = vc * lax.rsqrt(jnp.mean(vc * vc, axis=-1, keepdims=True) + EPS) * sgug_ref[...]
    lane = lax.broadcasted_iota(jnp.int32, (BLOCK, LANES), 1)
    low = lane < HEAD_DIM
    for cb in range(nblk):
        pieces = []
        for a in range(SGU_HEADS // 2):
            vp = vn[cb * BLOCK:(cb + 1) * BLOCK, a * LANES:(a + 1) * LANES]
            v_lo = jnp.where(low, vp, 0.0).astype(BF16)
            v_hi = jnp.where(low, 0.0, vp).astype(BF16)
            pieces.append(_dot(sguw_ref[2 * a], v_lo) + _dot(sguw_ref[2 * a + 1], v_hi))
        s = jnp.concatenate(pieces, axis=1) + sgub_ref[...]
        sgu_ref[cb * BLOCK:(cb + 1) * BLOCK, :] = u[cb * BLOCK:(cb + 1) * BLOCK, :] * s

    if local:
        kvx_ref[0:BLOCK, :] = kvp_ref[...]
        kvx_ref[BLOCK:BLOCK + tq, :] = kv_ref[...]
        kvx_ref[BLOCK + tq:, :] = kvn_ref[...]
    kvc = kvc_ref[...]
    qi = lax.broadcasted_iota(jnp.int32, (BLOCK, 3 * BLOCK), 0)
    ko = lax.broadcasted_iota(jnp.int32, (BLOCK, 3 * BLOCK), 1)
    band = (ko >= qi) & (ko <= qi + 2 * BLOCK)

    def attend(jb, carry):
        r0 = pl.multiple_of(jb * BLOCK, BLOCK)
        if local:
            n = t * nblk + jb
            first_key = jnp.where(n > 0, 0, BLOCK)
            end_key = jnp.where(n < nt * nblk - 1, 3 * BLOCK, 2 * BLOCK)
            ok = band & (ko >= first_key) & (ko < end_key)
            kvl = kvx_ref[pl.ds(r0, 3 * BLOCK), :]
        heads = []
        for a in range(N_HEADS // 2):
            qp = q_ref[pl.ds(r0, BLOCK), a * LANES:(a + 1) * LANES]
            q_lo = jnp.where(low, qp, jnp.zeros_like(qp))
            q_hi = jnp.where(low, jnp.zeros_like(qp), qp)
            first = a < N_HEADS // 4
            for qh, even in ((q_lo, True), (q_hi, False)):
                natural = first == even
                ksel = slice(0, 128) if natural else slice(128, 256)
                vsel = slice(256, 384) if natural else slice(384, 512)
                s_list = [_dot_nt(qh, kvc[:, ksel])]
                v_list = [kvc[:, vsel]]
                if local:
                    s_list.append(jnp.where(ok, _dot_nt(qh, kvl[:, ksel]), -jnp.inf))
                    v_list.append(kvl[:, vsel])
                heads.append((s_list, v_list))
        probs = []
        for hd, (s_list, _) in enumerate(heads):
            sink = sink_ref[hd:hd + 1, 0:1]
            m = jnp.maximum(sink, jnp.max(_fold_lanes(s_list, jnp.maximum), axis=-1, keepdims=True))
            ps = [jnp.exp(s - m) for s in s_list]
            denom = jnp.exp(sink - m) + jnp.sum(_fold_lanes(ps, jnp.add), axis=-1, keepdims=True)
            probs.append(([p.astype(BF16) for p in ps], 1.0 / denom))
        outs = []
        for (ps, rden), (_, v_list) in zip(probs, heads):
            o = _dot(ps[0], v_list[0])
            for p, vv in zip(ps[1:], v_list[1:]):
                o = o + _dot(p, vv)
            outs.append(o * rden)
        for a in range(N_HEADS // 2):
            attn_ref[pl.ds(r0, BLOCK), a * LANES:(a + 1) * LANES] = jnp.where(low, outs[2 * a], outs[2 * a + 1])
        return carry

    lax.fori_loop(0, nblk, attend, 0)

    g = mixg_ref[...]
    yc = (_rms(conv) * g[:, :D_CONV]).astype(BF16)
    ys = (_rms(sgu_ref[...]) * g[:, D_CONV:D_CONV + D_SGU]).astype(BF16)
    ya = (_rms(attn_ref[...]) * g[:, D_CONV + D_SGU:]).astype(BF16)
    yl = (_dot(yc, wout_ref[0:D_CONV, :]) + _dot(ys, wout_ref[D_CONV:D_CONV + D_SGU, :])
          + _dot(ya, wout_ref[D_CONV + D_SGU:, :]))
    xn = x_ref[...] + mod_ref[2:3, :] * yl
    xo_ref[...] = xn
    h2 = _rms(xn) * n2g_ref[...] * (1.0 + mod_ref[4:5, :]) + mod_ref[3:4, :]
    h2_ref[...] = h2.astype(BF16)


def _mix_call(x, pc, ps, q, kv, kvc, mod, conv_w, sgu_g, sgu_w, sgu_b, sink_b, mix_g, w_out, n2g, *, local):
    b, r, d = x.shape
    c = kvc.shape[1]
    tq = min(ROW_TILE, r)
    nt = r // tq
    hb = tq // SUBLANES
    kb = tq // BLOCK
    row = lambda n: pl.BlockSpec((None, tq, n), lambda i, t: (i, t, 0))
    full = lambda shape: pl.BlockSpec(shape, lambda i, t: (0,) * len(shape))
    return pl.pallas_call(
        functools.partial(_mix_kernel, local=local),
        grid=(b, nt),
        in_specs=[row(d), row(CONV_END),
                  pl.BlockSpec((None, SUBLANES, CONV_END), lambda i, t: (i, jnp.maximum(t * hb - 1, 0), 0)),
                  pl.BlockSpec((None, SUBLANES, CONV_END), lambda i, t: (i, jnp.minimum((t + 1) * hb, nt * hb - 1), 0)),
                  row(2 * D_SGU), row(D_ATTN), row(512),
                  pl.BlockSpec((None, BLOCK, 512), lambda i, t: (i, jnp.maximum(t * kb - 1, 0), 0)),
                  pl.BlockSpec((None, BLOCK, 512), lambda i, t: (i, jnp.minimum((t + 1) * kb, nt * kb - 1), 0)),
                  pl.BlockSpec((None, c, 512), lambda i, t: (i, 0, 0)),
                  pl.BlockSpec((None, 6, d), lambda i, t: (i, 0, 0)),
                  full((3, D_CONV)), full((1, D_SGU)), full((SGU_HEADS, SGU_CHUNK, SGU_CHUNK)),
                  full((SGU_CHUNK, D_SGU)), full((N_HEADS, LANES)), full((1, d)), full((d, d)), full((1, d))],
        out_specs=[row(d), row(d)],
        out_shape=[jax.ShapeDtypeStruct((b, r, d), F32), jax.ShapeDtypeStruct((b, r, d), BF16)],
        scratch_shapes=[pltpu.VMEM((tq + 2 * BLOCK, 512), BF16),
                        pltpu.VMEM((tq, D_ATTN), F32),
                        pltpu.VMEM((tq, D_SGU), F32)],
        compiler_params=_cparams(2),
        name="mixers_local" if local else "mixers_ctx",
    )(x, pc, pc, pc, ps, q, kv, kv, kv, kvc, mod, conv_w, sgu_g, sgu_w, sgu_b, sink_b, mix_g, w_out, n2g)


def _oddeven_merge(lo, hi, r):
    step = r * 2
    if step < hi - lo:
        yield from _oddeven_merge(lo, hi, step)
        yield from _oddeven_merge(lo + r, hi, step)
        yield from [(i, i + r) for i in range(lo + r, hi - r, step)]
    else:
        yield (lo, lo + r)


def _oddeven_sort(lo, hi):
    if hi - lo >= 1:
        mid = lo + (hi - lo) // 2
        yield from _oddeven_sort(lo, mid)
        yield from _oddeven_sort(mid + 1, hi)
        yield from _oddeven_merge(lo, hi, 1)


_SORT16 = tuple(_oddeven_sort(0, PEER_TOPK - 1))


def _sort16(x):
    x = list(x)
    for i, j in _SORT16:
        x[i], x[j] = jnp.maximum(x[i], x[j]), jnp.minimum(x[i], x[j])
    return x


def _merge_top16(a, b):
    n = PEER_TOPK
    c = [jnp.maximum(a[i], b[n - 1 - i]) for i in range(n)]
    d = n // 2
    while d:
        for i in range(n):
            if not i & d:
                c[i], c[i + d] = jnp.maximum(c[i], c[i + d]), jnp.minimum(c[i], c[i + d])
        d //= 2
    return c


def _top16(load):
    def tree(lo, n):
        if n == PEER_TOPK:
            return _sort16([load(lo + i) for i in range(n)])
        return _merge_top16(tree(lo, n // 2), tree(lo + n // 2, n // 2))
    return tree(0, N_KEYS)


def _peer_select(h_ref, ws_ref, ht_ref, e0_ref, e1_ref, th_ref, s0_ref, s1_ref, a0_ref, b0_ref, rz_ref):
    tm = h_ref.shape[0]
    nh = PEER_HEADS
    ht_ref[...] = h_ref[...].astype(F32).T.astype(BF16)
    ht = ht_ref[...]
    s0_ref[...] = _dot(ws_ref[0], ht)
    s1 = _dot(ws_ref[1], ht)
    for lc in range(tm // LANES):
        s1_ref[lc] = s1[:, lc * LANES:(lc + 1) * LANES]

    rnd = lambda v: v.astype(BF16).astype(F32)

    def select(lc, carry):
        lanes = pl.ds(pl.multiple_of(lc * LANES, LANES), LANES)
        a = _top16(lambda i: s0_ref[i * nh:(i + 1) * nh, lanes])
        b = _top16(lambda j: s1_ref[lc, j * nh:(j + 1) * nh, :])
        ea = [jnp.exp(v - a[0]) for v in a]
        eb = [jnp.exp(v - b[0]) for v in b]
        cand = [ea[k] * eb[l] for k, l in _CAND]
        rest = cand[PEER_TOPK:]
        rest = rest + [jnp.full_like(cand[0], -1.0)] * (-len(rest) % PEER_TOPK)
        best = cand[:PEER_TOPK]
        for g in range(0, len(rest), PEER_TOPK):
            best = _merge_top16(best, _sort16(rest[g:g + PEER_TOPK]))
        top = best[PEER_TOPK - 1]
        zsum = jnp.zeros_like(top)
        for p in cand:
            zsum = zsum + jnp.where(p >= top, p, 0.0)
        rz = 1.0 / zsum
        ean = [rnd(v * rz) for v in ea]
        ebn = [rnd(v) for v in eb]
        thn = jnp.full_like(top, jnp.inf)
        for (k, l), p in zip(_CAND, cand):
            thn = jnp.minimum(thn, jnp.where(p >= top, rnd(ean[k] * ebn[l]), jnp.inf))
        for hh in range(nh):
            th_ref[hh, :, lanes] = jnp.broadcast_to(thn[hh:hh + 1, :], (2 * SUBLANES, LANES)).astype(BF16)
        a0_ref[:, lanes] = a[0]
        b0_ref[:, lanes] = b[0]
        rz_ref[:, lanes] = rz
        return carry

    lax.fori_loop(0, tm // LANES, select, 0)

    a0 = a0_ref[...]
    rz = rz_ref[...]
    for i in range(N_KEYS):
        e0_ref[i] = jnp.exp(s0_ref[i * nh:(i + 1) * nh, :] - a0) * rz
    for hh in range(nh):
        for lc in range(tm // LANES):
            lanes = slice(lc * LANES, (lc + 1) * LANES)
            s1_head = s1_ref[lc, pl.ds(hh, N_KEYS, stride=nh), :]
            e1_ref[hh, :, lanes] = jnp.exp(s1_head - b0_ref[hh:hh + 1, lanes]).astype(BF16)


def _gelu_sigmoid_form(x):
    k0 = -2.0 * math.sqrt(2.0 / math.pi) * math.log2(math.e)
    k1 = 0.044715 * k0
    return x / (1.0 + jnp.exp2(x * (x * x * k1 + k0)))


def _peer_gate(a_ref, row0, e0_ref, e1_ref, th_ref, hbuf_ref, lanes):
    n_lanes = lanes.stop - lanes.start
    pack = 2 * SUBLANES
    for ii in range(a_ref.shape[0] // N_KEYS):
        e0 = e0_ref[row0 + ii, :, lanes]
        e0r = [jnp.broadcast_to(e0[hh:hh + 1, :], (pack, n_lanes)).astype(BF16) for hh in range(PEER_HEADS)]
        for c in range(N_KEYS // pack):
            gate = None
            for hh in range(PEER_HEADS):
                p = e0r[hh] * e1_ref[hh, c * pack:(c + 1) * pack, lanes]
                sel = jnp.where(p >= th_ref[hh, :, lanes], p, jnp.zeros_like(p))
                gate = sel if gate is None else gate + sel
            r0 = ii * N_KEYS + c * pack
            act = _gelu_sigmoid_form(a_ref[r0:r0 + pack, lanes])
            hbuf_ref[r0:r0 + pack, lanes] = act * gate


def _peer_kernel(h_ref, x_ref, mod_ref, fg_ref, ws_ref, u0_ref, uodd_ref, uevn_ref, vt_ref, o_ref,
                 ht_ref, e0_ref, e1_ref, th_ref, s0_ref, s1_ref, a0_ref, b0_ref, rz_ref,
                 aevn_ref, aodd_ref, hevn_ref, hodd_ref, acc_ref, *, final_norm):
    e = pl.program_id(2)
    ne = pl.num_programs(2)
    tm = h_ref.shape[0]
    eb = u0_ref.shape[0]
    keys_per_block = eb // N_KEYS

    @pl.when(e == 0)
    def _():
        _peer_select(h_ref, ws_ref, ht_ref, e0_ref, e1_ref, th_ref, s0_ref, s1_ref, a0_ref, b0_ref, rz_ref)
        aevn_ref[...] = _dot(u0_ref[...], ht_ref[...]).astype(BF16)
        acc_ref[...] = jnp.zeros_like(acc_ref)

    n_split = 2 if tm % (2 * 2 * LANES) == 0 else 1
    lane_ranges = [slice(s * (tm // n_split), (s + 1) * (tm // n_split)) for s in range(n_split)]
    stages = ((uodd_ref, aodd_ref, aevn_ref, hevn_ref), (uevn_ref, aevn_ref, aodd_ref, hodd_ref))

    def next_scores(s, lanes):
        u_next, a_next, _, _ = stages[s]
        a_next[:, lanes] = _dot(u_next[...], ht_ref[:, lanes]).astype(BF16)

    def gates(s, lanes):
        _, _, a_cur, hbuf_ref = stages[s]
        _peer_gate(a_cur, (2 * e + s) * keys_per_block, e0_ref, e1_ref, th_ref, hbuf_ref, lanes)

    def values(s, lanes):
        hbuf_ref = stages[s][3]
        acc_ref[:, lanes] += _dot(vt_ref[:, s * eb:(s + 1) * eb], hbuf_ref[:, lanes])

    chains = [(s, lanes) for s in range(len(stages)) for lanes in lane_ranges]
    next_scores(*chains[0])
    for c, chain in enumerate(chains):
        if c + 1 < len(chains):
            next_scores(*chains[c + 1])
        gates(*chain)
        values(*chain)

    @pl.when(e == ne - 1)
    def _():
        y = x_ref[...] + mod_ref[5:6, :] * acc_ref[...].T
        if final_norm:
            y = _rms(y) * fg_ref[...]
        o_ref[...] = y


def _fold_kernel(k_ref, w_ref, o_ref):
    k = k_ref[...]
    w = w_ref[...]
    k_hi = k.astype(BF16)
    k_lo = (k - k_hi.astype(F32)).astype(BF16)
    w_hi = w.astype(BF16)
    w_lo = (w - w_hi.astype(F32)).astype(BF16)
    o_ref[...] = (_dot(k_hi, w_hi) + _dot(k_lo, w_hi) + _dot(k_hi, w_lo)).astype(BF16)


def _fold_call(keys_p, wq_t):
    _, m, k = keys_p.shape
    n = wq_t.shape[2]
    tn = 256
    return pl.pallas_call(
        _fold_kernel,
        grid=(2, n // tn),
        in_specs=[pl.BlockSpec((None, m, k), lambda p, j: (p, 0, 0)),
                  pl.BlockSpec((None, k, tn), lambda p, j: (p, 0, j))],
        out_specs=pl.BlockSpec((None, m, tn), lambda p, j: (p, 0, j)),
        out_shape=jax.ShapeDtypeStruct((2, m, n), BF16),
        compiler_params=_cparams(2),
        name="fold_keys",
    )(keys_p, wq_t)


def _peer_call(h2, x, mod, final_g, w_s, u_b, vt_b, *, final_norm):
    b, r, d = x.shape
    tm = min(PEER_TOKENS, r)
    eb = PEER_EXPERTS
    n_blocks = u_b.shape[0] // eb
    nh = PEER_HEADS
    row = lambda: pl.BlockSpec((None, tm, d), lambda i, t, e: (i, t, 0))
    full = lambda shape: pl.BlockSpec(shape, lambda i, t, e: (0,) * len(shape))
    return pl.pallas_call(
        functools.partial(_peer_kernel, final_norm=final_norm),
        grid=(b, r // tm, n_blocks // 2),
        in_specs=[row(), row(),
                  pl.BlockSpec((None, 6, d), lambda i, t, e: (i, 0, 0)),
                  full((1, d)), full(w_s.shape),
                  pl.BlockSpec((eb, d), lambda i, t, e: (0, 0)),
                  pl.BlockSpec((eb, d), lambda i, t, e: (2 * e + 1, 0)),
                  pl.BlockSpec((eb, d), lambda i, t, e: (jnp.minimum(2 * e + 2, n_blocks - 2), 0)),
                  pl.BlockSpec((d, 2 * eb), lambda i, t, e: (0, e))],
        out_specs=row(),
        out_shape=jax.ShapeDtypeStruct((b, r, d), F32),
        scratch_shapes=[pltpu.VMEM((d, tm), BF16),
                        pltpu.VMEM((N_KEYS, nh, tm), F32),
                        pltpu.VMEM((nh, N_KEYS, tm), BF16),
                        pltpu.VMEM((nh, 2 * SUBLANES, tm), BF16),
                        pltpu.VMEM((N_KEYS * nh, tm), F32),
                        pltpu.VMEM((tm // LANES, N_KEYS * nh, LANES), F32),
                        pltpu.VMEM((nh, tm), F32),
                        pltpu.VMEM((nh, tm), F32),
                        pltpu.VMEM((nh, tm), F32),
                        pltpu.VMEM((eb, tm), BF16),
                        pltpu.VMEM((eb, tm), BF16),
                        pltpu.VMEM((eb, tm), BF16),
                        pltpu.VMEM((eb, tm), BF16),
                        pltpu.VMEM((d, tm), F32)],
        compiler_params=_cparams(3),
        name="peer_final" if final_norm else "peer",
    )(h2, x, mod, final_g, w_s, u_b, u_b, u_b, vt_b)


def _rope_tables(length):
    rows = length // GRID_W
    row = jnp.repeat(jnp.arange(rows), GRID_W).astype(F32)
    col = jnp.tile(jnp.arange(GRID_W), rows).astype(F32)
    inv = ROPE_THETA ** (-jnp.arange(ROPE_FREQS, dtype=F32) / ROPE_FREQS)
    ar = row[:, None] * inv[None, :]
    ac = col[:, None] * inv[None, :]
    ang = jnp.concatenate([ar, ar, ac, ac, ar, ar, ac, ac], axis=-1)
    return jnp.cos(ang), jnp.sin(ang)


def _rot_cols(w, heads):
    r = w.reshape(w.shape[0], heads, 4, ROPE_FREQS)
    return jnp.stack([-r[:, :, 1], r[:, :, 0], -r[:, :, 3], r[:, :, 2]], axis=2).reshape(w.shape)


def _swap_heads(w):
    return jnp.concatenate([w[:, HEAD_DIM:], w[:, :HEAD_DIM]], axis=1)


def _augment_w_in(w):
    wq = w[:, SGU_END:Q_END]
    wk = w[:, Q_END:K_END]
    wv = w[:, K_END:]
    wkr = _rot_cols(wk, N_KV_HEADS)
    cols = [w[:, :SGU_END], wq, _rot_cols(wq, N_HEADS), wk, _swap_heads(wk), wkr, _swap_heads(wkr),
            wv, _swap_heads(wv)]
    return jnp.concatenate(cols, axis=1).astype(BF16)


def kernel(x, c, ctx, c_ctx, w_ada, b_ada, norm1_g, norm2_g, w_in, conv_w, sgu_norm_g, sgu_w, sgu_b,
           attn_sink, mix_norm_g, w_out, peer_wq, peer_keys, peer_u, peer_v, final_g):
    bsz, length, d = x.shape
    n_ctx = ctx.shape[1]
    depth = w_ada.shape[0]
    nh = PEER_HEADS

    cc = jnp.zeros((SUBLANES, d), F32).at[:bsz].set(c).at[bsz].set(c_ctx)
    mod = _mod_call(cc, w_ada, b_ada)

    cos_l, sin_l = _rope_tables(length)
    cos_c = jnp.ones((n_ctx, 2 * HEAD_DIM), F32)
    sin_c = jnp.zeros((n_ctx, 2 * HEAD_DIM), F32)
    fg = final_g.reshape(1, d)

    xl, xc = x, ctx
    for i in range(depth):
        last = i == depth - 1
        mod_l = mod[i, :bsz].reshape(bsz, 6, d)
        mod_c = jnp.broadcast_to(mod[i, bsz].reshape(1, 6, d), (bsz, 6, d))
        n1g = norm1_g[i].reshape(1, d)
        n2g = norm2_g[i].reshape(1, d)
        w_aug = _augment_w_in(w_in[i])
        sgu_g = sgu_norm_g[i].reshape(1, D_SGU)
        sgu_wb = sgu_w[i].astype(BF16)
        sgu_bias = jnp.repeat(sgu_b[i].T, D_SGU // SGU_HEADS, axis=1)
        sink_b = jnp.broadcast_to(attn_sink[i][:, None], (N_HEADS, LANES))
        mix_g = mix_norm_g[i].reshape(1, d)
        w_out_b = w_out[i].astype(BF16)
        wq_t =peer_wq[i].reshape(d, nh, 2, PEER_DHALF).transpose(2, 1, 3, 0).reshape(2 * nh * PEER_DHALF, d)
        keys_p = jnp.einsum('hpid,hg->pihgd', peer_keys[i], jnp.eye(nh, dtype=F32))
        keys_p = keys_p.reshape(2, N_KEYS * nh, nh * PEER_DHALF)
        w_s = _fold_call(keys_p, wq_t.reshape(2, nh * PEER_DHALF, d))
        u_b = peer_u[i].astype(BF16)
        vt_b = peer_v[i].T.astype(BF16)
        mixer_w = (conv_w[i], sgu_g, sgu_wb, sgu_bias, sink_b, mix_g, w_out_b, n2g)

        pc_c, ps_c, q_c, kv_c = _in_call(xc, mod_c, n1g, cos_c, sin_c, w_aug)
        pc_l, ps_l, q_l, kv_l = _in_call(xl, mod_l, n1g, cos_l, sin_l, w_aug)
        xl, h2_l = _mix_call(xl, pc_l, ps_l, q_l, kv_l, kv_c, mod_l, *mixer_w, local=True)
        if not last:
            xc, h2_c = _mix_call(xc, pc_c, ps_c, q_c, kv_c, kv_c, mod_c, *mixer_w, local=False)
            xc = _peer_call(h2_c, xc, mod_c, fg, w_s, u_b, vt_b, final_norm=False)
        xl = _peer_call(h2_l, xl, mod_l, fg, w_s, u_b, vt_b, final_norm=last)
    return xl
```

```python
import functools
import math

import jax
import jax.numpy as jnp
from jax import lax
from jax.experimental import pallas as pl
from jax.experimental.pallas import tpu as pltpu

F32 = jnp.float32
BF16 = jnp.bfloat16

EPS = 1e-6
GRID_W = 64
D_CONV = 256
D_SGU = 256
SGU_HEADS = 4
SGU_CHUNK = 128
N_HEADS = 8
N_KV_HEADS = 2
HEAD_DIM = 64
D_ATTN = N_HEADS * HEAD_DIM
BLOCK = 128
ROPE_THETA = 10000.0
ROPE_FREQS = HEAD_DIM // 4
CONV_END = 3 * D_CONV
SGU_END = CONV_END + 2 * D_SGU
Q_END = SGU_END + D_ATTN
K_END = Q_END + N_KV_HEADS * HEAD_DIM
N_KEYS = 128
PEER_HEADS = 8
PEER_TOPK = 16
PEER_DHALF = 128

LANES = 128
SUBLANES = 8
VMEM_LIMIT_BYTES = 56 * 1024 * 1024

ROW_TILE = 512
PEER_TOKENS = 512
PEER_EXPERTS = 1024

_QO = SGU_END
_QR = _QO + D_ATTN
_KA = _QR + D_ATTN
_KB = _KA + 128
_KAR = _KB + 128
_KBR = _KAR + 128
_VA = _KBR + 128
_VB = _VA + 128
D_AUG = _VB + 128

_CAND = [(k, l) for k in range(PEER_TOPK) for l in range(PEER_TOPK) if (k + 1) * (l + 1) <= PEER_TOPK]


def _cparams(n_axes):
    return pltpu.CompilerParams(dimension_semantics=("arbitrary",) * n_axes,
                                vmem_limit_bytes=VMEM_LIMIT_BYTES)


def _gelu(x):
    c = math.sqrt(2.0 / math.pi)
    return 0.5 * x * (1.0 + jnp.tanh(c * (x + 0.044715 * (x * x * x))))


def _dot(a, b):
    return jnp.dot(a, b, preferred_element_type=F32)


def _dot_nt(a, b):
    return lax.dot_general(a, b, (((1,), (1,)), ((), ())), preferred_element_type=F32)


def _pack_rows(w):
    m, k = w.shape
    return lax.bitcast_convert_type(w.reshape(m // 2, 2, k).transpose(0, 2, 1), jnp.int32)


def _unpack_rows(x):
    return pltpu.bitcast(x, BF16)


def _rms(x):
    return x * lax.rsqrt(jnp.mean(x * x, axis=-1, keepdims=True) + EPS)


def _mod_kernel(c_ref, w_ref, b_ref, o_ref):
    c = c_ref[...]
    sc = c / (1.0 + jnp.exp(-c))
    w = w_ref[...]
    c_hi = sc.astype(BF16)
    c_lo = (sc - c_hi.astype(F32)).astype(BF16)
    w_hi = w.astype(BF16)
    w_lo = (w - w_hi.astype(F32)).astype(BF16)
    o_ref[...] = _dot(c_hi, w_hi) + _dot(c_lo, w_hi) + _dot(c_hi, w_lo) + b_ref[...]


def _mod_call(cc, w_ada, b_ada):
    depth, d, n = w_ada.shape
    tn = 1536
    return pl.pallas_call(
        _mod_kernel,
        grid=(depth, n // tn),
        in_specs=[pl.BlockSpec((SUBLANES, d), lambda i, j: (0, 0)),
                  pl.BlockSpec((None, d, tn), lambda i, j: (i, 0, j)),
                  pl.BlockSpec((None, 1, tn), lambda i, j: (i, 0, j))],
        out_specs=pl.BlockSpec((None, SUBLANES, tn), lambda i, j: (i, 0, j)),
        out_shape=jax.ShapeDtypeStruct((depth, SUBLANES, n), F32),
        compiler_params=_cparams(2),
        name="adaln_mod",
    )(cc, w_ada, b_ada.reshape(depth, 1, n))


def _in_kernel(x_ref, mod_ref, g_ref, cos_ref, sin_ref, w_ref, pc_ref, ps_ref, q_ref, kv_ref):
    x = x_ref[...]
    h = _rms(x) * g_ref[...] * (1.0 + mod_ref[1:2, :]) + mod_ref[0:1, :]
    hb = h.astype(BF16)

    def proj(lo, hi):
        return _dot(hb, w_ref[:, lo:hi])

    pc_ref[...] = proj(0, CONV_END)
    ps_ref[...] = proj(CONV_END, SGU_END)
    cos = cos_ref[...]
    sin = sin_ref[...]
    cos4 = jnp.concatenate([cos] * 4, axis=1)
    sin4 = jnp.concatenate([sin] * 4, axis=1)
    scale = HEAD_DIM ** -0.5
    q = (proj(_QO, _QO + D_ATTN) * cos4 + proj(_QR, _QR + D_ATTN) * sin4) * scale
    q_ref[...] = q.astype(BF16)
    ka = proj(_KA, _KA + 128) * cos + proj(_KAR, _KAR + 128) * sin
    kb = proj(_KB, _KB + 128) * cos + proj(_KBR, _KBR + 128) * sin
    kv_ref[:, 0:128] = ka.astype(BF16)
    kv_ref[:, 128:256] = kb.astype(BF16)
    kv_ref[:, 256:512] = proj(_VA, _VA + 256).astype(BF16)


def _in_call(x, mod, norm_g, cos2, sin2, w_aug):
    b, r, d = x.shape
    tm = min(ROW_TILE, r)
    row = lambda n: pl.BlockSpec((None, tm, n), lambda i, t: (i, t, 0))
    return pl.pallas_call(
        _in_kernel,
        grid=(b, r // tm),
        in_specs=[row(d),
                  pl.BlockSpec((None, 6, d), lambda i, t: (i, 0, 0)),
                  pl.BlockSpec((1, d), lambda i, t: (0, 0)),
                  pl.BlockSpec((tm, 128), lambda i, t: (t, 0)),
                  pl.BlockSpec((tm, 128), lambda i, t: (t, 0)),
                  pl.BlockSpec((d, D_AUG), lambda i, t: (0, 0))],
        out_specs=[row(CONV_END), row(2 * D_SGU), row(D_ATTN), row(512)],
        out_shape=[jax.ShapeDtypeStruct((b, r, CONV_END), F32),
                   jax.ShapeDtypeStruct((b, r, 2 * D_SGU), F32),
                   jax.ShapeDtypeStruct((b, r, D_ATTN), BF16),
                   jax.ShapeDtypeStruct((b, r, 512), BF16)],
        compiler_params=_cparams(2),
        name="in_proj",
    )(x, mod, norm_g, cos2, sin2, w_aug)


def _fold_lanes(blocks, op):
    parts = [b[:, c:c + LANES] for b in blocks for c in range(0, b.shape[1], LANES)]
    out = parts[0]
    for p in parts[1:]:
        out = op(out, p)
    return out


def _mix_kernel(x_ref, pc_ref, pcp_ref, pcn_ref, ps_ref, q_ref, kv_ref, kvp_ref, kvn_ref, kvc_ref,
                mod_ref, convw_ref, sgug_ref, sguw_ref, sgub_ref, sink_ref, mixg_ref, wout_ref, n2g_ref,
                xo_ref, h2_ref, kvx_ref, attn_ref, sgu_ref, *, local):
    tq = x_ref.shape[0]
    nblk = tq // BLOCK
    t = pl.program_id(1)
    nt = pl.num_programs(1)

    pc = pc_ref[...]
    z = pc[:, D_CONV:2 * D_CONV] * pc[:, 2 * D_CONV:]
    z_before = pcp_ref[7:8, D_CONV:2 * D_CONV] * pcp_ref[7:8, 2 * D_CONV:]
    z_after = pcn_ref[0:1, D_CONV:2 * D_CONV] * pcn_ref[0:1, 2 * D_CONV:]
    z_before = z_before * (t > 0).astype(F32)
    z_after = z_after * (t < nt - 1).astype(F32)
    rows = lax.broadcasted_iota(jnp.int32, (tq, D_CONV), 0)
    z_prev = jnp.where(rows == 0, z_before, pltpu.roll(z, 1, axis=0))
    z_next = jnp.where(rows == tq - 1, z_after, pltpu.roll(z, tq - 1, axis=0))
    conv = pc[:, :D_CONV] * (z_prev * convw_ref[0:1, :] + z * convw_ref[1:2, :] + z_next * convw_ref[2:3, :])

    zg = _gelu(ps_ref[...])
    u = zg[:, :D_SGU]
    v = zg[:, D_SGU:]
    mu = jnp.mean(v, axis=-1, keepdims=True)
    vc = v - mu
    vn = vc * lax.rsqrt(jnp.mean(vc * vc, axis=-1, keepdims=True) + EPS) * sgug_ref[...]
    lane = lax.broadcasted_iota(jnp.int32, (BLOCK, LANES), 1)
    low = lane < HEAD_DIM
    for cb in range(nblk):
        pieces = []
        for a in range(SGU_HEADS // 2):
            vp = vn[cb * BLOCK:(cb + 1) * BLOCK, a * LANES:(a + 1) * LANES]
            v_lo = jnp.where(low, vp, 0.0).astype(BF16)
            v_hi = jnp.where(low, 0.0, vp).astype(BF16)
            pieces.append(_dot(sguw_ref[2 * a], v_lo) + _dot(sguw_ref[2 * a + 1], v_hi))
        s = jnp.concatenate(pieces, axis=1) + sgub_ref[...]
        sgu_ref[cb * BLOCK:(cb + 1) * BLOCK, :] = u[cb * BLOCK:(cb + 1) * BLOCK, :] * s

    if local:
        kvx_ref[0:BLOCK, :] = kvp_ref[...]
        kvx_ref[BLOCK:BLOCK + tq, :] = kv_ref[...]
        kvx_ref[BLOCK + tq:, :] = kvn_ref[...]
    kvc = kvc_ref[...]
    qi = lax.broadcasted_iota(jnp.int32, (BLOCK, 3 * BLOCK), 0)
    ko = lax.broadcasted_iota(jnp.int32, (BLOCK, 3 * BLOCK), 1)
    band = (ko >= qi) & (ko <= qi + 2 * BLOCK)

    def attend(jb, carry):
        r0 = pl.multiple_of(jb * BLOCK, BLOCK)
        if local:
            n = t * nblk + jb
            first_key = jnp.where(n > 0, 0, BLOCK)
            end_key = jnp.where(n < nt * nblk - 1, 3 * BLOCK, 2 * BLOCK)
            ok = band & (ko >= first_key) & (ko < end_key)
            kvl = kvx_ref[pl.ds(r0, 3 * BLOCK), :]
        heads = []
        for a in range(N_HEADS // 2):
            qp = q_ref[pl.ds(r0, BLOCK), a * LANES:(a + 1) * LANES]
            q_lo = jnp.where(low, qp, jnp.zeros_like(qp))
            q_hi = jnp.where(low, jnp.zeros_like(qp), qp)
            first = a < N_HEADS // 4
            for qh, even in ((q_lo, True), (q_hi, False)):
                natural = first == even
                ksel = slice(0, 128) if natural else slice(128, 256)
                vsel = slice(256, 384) if natural else slice(384, 512)
                s_list = [_dot_nt(qh, kvc[:, ksel])]
                v_list = [kvc[:, vsel]]
                if local:
                    s_list.append(jnp.where(ok, _dot_nt(qh, kvl[:, ksel]), -jnp.inf))
                    v_list.append(kvl[:, vsel])
                heads.append((s_list, v_list))
        probs = []
        for hd, (s_list, _) in enumerate(heads):
            sink = sink_ref[hd:hd + 1, 0:1]
            m = jnp.maximum(sink, jnp.max(_fold_lanes(s_list, jnp.maximum), axis=-1, keepdims=True))
            ps = [jnp.exp(s - m) for s in s_list]
            denom = jnp.exp(sink - m) + jnp.sum(_fold_lanes(ps, jnp.add), axis=-1, keepdims=True)
            probs.append(([p.astype(BF16) for p in ps], 1.0 / denom))
        outs = []
        for (ps, rden), (_, v_list) in zip(probs, heads):
            o = _dot(ps[0], v_list[0])
            for p, vv in zip(ps[1:], v_list[1:]):
                o = o + _dot(p, vv)
            outs.append(o * rden)
        for a in range(N_HEADS // 2):
            attn_ref[pl.ds(r0, BLOCK), a * LANES:(a + 1) * LANES] = jnp.where(low, outs[2 * a], outs[2 * a + 1])
        return carry

    lax.fori_loop(0, nblk, attend, 0)

    g = mixg_ref[...]
    yc = (_rms(conv) * g[:, :D_CONV]).astype(BF16)
    ys = (_rms(sgu_ref[...]) * g[:, D_CONV:D_CONV + D_SGU]).astype(BF16)
    ya = (_rms(attn_ref[...]) * g[:, D_CONV + D_SGU:]).astype(BF16)
    yl = (_dot(yc, wout_ref[0:D_CONV, :]) + _dot(ys, wout_ref[D_CONV:D_CONV + D_SGU, :])
          + _dot(ya, wout_ref[D_CONV + D_SGU:, :]))
    xn = x_ref[...] + mod_ref[2:3, :] * yl
    xo_ref[...] = xn
    h2 = _rms(xn) * n2g_ref[...] * (1.0 + mod_ref[4:5, :]) + mod_ref[3:4, :]
    h2_ref[...] = h2.astype(BF16)


def _mix_call(x, pc, ps, q, kv, kvc, mod, conv_w, sgu_g, sgu_w, sgu_b, sink_b, mix_g, w_out, n2g, *, local):
    b, r, d = x.shape
    c = kvc.shape[1]
    tq = min(ROW_TILE, r)
    nt = r // tq
    hb = tq // SUBLANES
    kb = tq // BLOCK
    row = lambda n: pl.BlockSpec((None, tq, n), lambda i, t: (i, t, 0))
    full = lambda shape: pl.BlockSpec(shape, lambda i, t: (0,) * len(shape))
    return pl.pallas_call(
        functools.partial(_mix_kernel, local=local),
        grid=(b, nt),
        in_specs=[row(d), row(CONV_END),
                  pl.BlockSpec((None, SUBLANES, CONV_END), lambda i, t: (i, jnp.maximum(t * hb - 1, 0), 0)),
                  pl.BlockSpec((None, SUBLANES, CONV_END), lambda i, t: (i, jnp.minimum((t + 1) * hb, nt * hb - 1), 0)),
                  row(2 * D_SGU), row(D_ATTN), row(512),
                  pl.BlockSpec((None, BLOCK, 512), lambda i, t: (i, jnp.maximum(t * kb - 1, 0), 0)),
                  pl.BlockSpec((None, BLOCK, 512), lambda i, t: (i, jnp.minimum((t + 1) * kb, nt * kb - 1), 0)),
                  pl.BlockSpec((None, c, 512), lambda i, t: (i, 0, 0)),
                  pl.BlockSpec((None, 6, d), lambda i, t: (i, 0, 0)),
                  full((3, D_CONV)), full((1, D_SGU)), full((SGU_HEADS, SGU_CHUNK, SGU_CHUNK)),
                  full((SGU_CHUNK, D_SGU)), full((N_HEADS, LANES)), full((1, d)), full((d, d)), full((1, d))],
        out_specs=[row(d), row(d)],
        out_shape=[jax.ShapeDtypeStruct((b, r, d), F32), jax.ShapeDtypeStruct((b, r, d), BF16)],
        scratch_shapes=[pltpu.VMEM((tq + 2 * BLOCK, 512), BF16),
                        pltpu.VMEM((tq, D_ATTN), F32),
                        pltpu.VMEM((tq, D_SGU), F32)],
        compiler_params=_cparams(2),
        name="mixers_local" if local else "mixers_ctx",
    )(x, pc, pc, pc, ps, q, kv, kv, kv, kvc, mod, conv_w, sgu_g, sgu_w, sgu_b, sink_b, mix_g, w_out, n2g)


def _oddeven_merge(lo, hi, r):
    step = r * 2
    if step < hi - lo:
        yield from _oddeven_merge(lo, hi, step)
        yield from _oddeven_merge(lo + r, hi, step)
        yield from [(i, i + r) for i in range(lo + r, hi - r, step)]
    else:
        yield (lo, lo + r)


def _oddeven_sort(lo, hi):
    if hi - lo >= 1:
        mid = lo + (hi - lo) // 2
        yield from _oddeven_sort(lo, mid)
        yield from _oddeven_sort(mid + 1, hi)
        yield from _oddeven_merge(lo, hi, 1)


_SORT16 = tuple(_oddeven_sort(0, PEER_TOPK - 1))


def _sort16(x):
    x = list(x)
    for i, j in _SORT16:
        x[i], x[j] = jnp.maximum(x[i], x[j]), jnp.minimum(x[i], x[j])
    return x


def _merge_top16(a, b):
    n = PEER_TOPK
    c = [jnp.maximum(a[i], b[n - 1 - i]) for i in range(n)]
    d = n // 2
    while d:
        for i in range(n):
            if not i & d:
                c[i], c[i + d] = jnp.maximum(c[i], c[i + d]), jnp.minimum(c[i], c[i + d])
        d //= 2
    return c


def _top16(load):
    def tree(lo, n):
        if n == PEER_TOPK:
            return _sort16([load(lo + i) for i in range(n)])
        return _merge_top16(tree(lo, n // 2), tree(lo + n // 2, n // 2))
    return tree(0, N_KEYS)


def _peer_select(h_ref, ws_ref, ht_ref, e0_ref, e1_ref, th_ref, s0_ref, s1_ref, a0_ref, b0_ref, rz_ref):
    tm = h_ref.shape[0]
    nh = PEER_HEADS
    ht_ref[...] = h_ref[...].astype(F32).T.astype(BF16)
    ht = ht_ref[...]
    s0_ref[...] = _dot(_unpack_rows(ws_ref[0]), ht)
    s1 = _dot(_unpack_rows(ws_ref[1]), ht)
    for lc in range(tm // LANES):
        s1_ref[lc] = s1[:, lc * LANES:(lc + 1) * LANES]

    rnd = lambda v: v.astype(BF16).astype(F32)

    def select(lc, carry):
        lanes = pl.ds(pl.multiple_of(lc * LANES, LANES), LANES)
        a = _top16(lambda i: s0_ref[i * nh:(i + 1) * nh, lanes])
        b = _top16(lambda j: s1_ref[lc, j * nh:(j + 1) * nh, :])
        ea = [jnp.exp(v - a[0]) for v in a]
        eb = [jnp.exp(v - b[0]) for v in b]
        cand = [ea[k] * eb[l] for k, l in _CAND]
        rest = cand[PEER_TOPK:]
        rest = rest + [jnp.full_like(cand[0], -1.0)] * (-len(rest) % PEER_TOPK)
        best = cand[:PEER_TOPK]
        for g in range(0, len(rest), PEER_TOPK):
            best = _merge_top16(best, _sort16(rest[g:g + PEER_TOPK]))
        top = best[PEER_TOPK - 1]
        zsum = jnp.zeros_like(top)
        for p in cand:
            zsum = zsum + jnp.where(p >= top, p, 0.0)
        rz = 1.0 / zsum
        ean = [rnd(v * rz) for v in ea]
        ebn = [rnd(v) for v in eb]
        thn = jnp.full_like(top, jnp.inf)
        for (k, l), p in zip(_CAND, cand):
            thn = jnp.minimum(thn, jnp.where(p >= top, rnd(ean[k] * ebn[l]), jnp.inf))
        for hh in range(nh):
            th_ref[hh, :, lanes] = jnp.broadcast_to(thn[hh:hh + 1, :], (2 * SUBLANES, LANES)).astype(BF16)
        a0_ref[:, lanes] = a[0]
        b0_ref[:, lanes] = b[0]
        rz_ref[:, lanes] = rz
        return carry

    lax.fori_loop(0, tm // LANES, select, 0)

    a0 = a0_ref[...]
    rz = rz_ref[...]
    for i in range(N_KEYS):
        e0_ref[i] = jnp.exp(s0_ref[i * nh:(i + 1) * nh, :] - a0) * rz
    for hh in range(nh):
        for lc in range(tm // LANES):
            lanes = slice(lc * LANES, (lc + 1) * LANES)
            s1_head = s1_ref[lc, pl.ds(hh, N_KEYS, stride=nh), :]
            e1_ref[hh, :, lanes] = jnp.exp(s1_head - b0_ref[hh:hh + 1, lanes]).astype(BF16)


def _gelu_sigmoid_form(x):
    k0 = -2.0 * math.sqrt(2.0 / math.pi) * math.log2(math.e)
    k1 = 0.044715 * k0
    return x / (1.0 + jnp.exp2(x * (x * x * k1 + k0)))


def _peer_gate(a_ref, row0, e0_ref, e1_ref, th_ref, hbuf_ref, lanes):
    n_lanes = lanes.stop - lanes.start
    pack = 2 * SUBLANES
    for ii in range(a_ref.shape[0] // N_KEYS):
        e0 = e0_ref[row0 + ii, :, lanes]
        e0r = [jnp.broadcast_to(e0[hh:hh + 1, :], (pack, n_lanes)).astype(BF16) for hh in range(PEER_HEADS)]
        for c in range(N_KEYS // pack):
            gate = None
            for hh in range(PEER_HEADS):
                p = e0r[hh] * e1_ref[hh, c * pack:(c + 1) * pack, lanes]
                sel = jnp.where(p >= th_ref[hh, :, lanes], p, jnp.zeros_like(p))
                gate = sel if gate is None else gate + sel
            r0 = ii * N_KEYS + c * pack
            act = _gelu_sigmoid_form(a_ref[r0:r0 + pack, lanes])
            hbuf_ref[r0:r0 + pack, lanes] = act * gate


def _peer_kernel(h_ref, x_ref, mod_ref, fg_ref, ws_ref, u0_ref, uodd_ref, uevn_ref, vt_ref, o_ref,
                 ht_ref, e0_ref, e1_ref, th_ref, s0_ref, s1_ref, a0_ref, b0_ref, rz_ref,
                 aevn_ref, aodd_ref, hevn_ref, hodd_ref, acc_ref, *, final_norm):
    e = pl.program_id(2)
    ne = pl.num_programs(2)
    tm = h_ref.shape[0]
    eb = 2 * u0_ref.shape[0]
    keys_per_block = eb // N_KEYS

    @pl.when(e == 0)
    def _():
        _peer_select(h_ref, ws_ref, ht_ref, e0_ref, e1_ref, th_ref, s0_ref, s1_ref, a0_ref, b0_ref, rz_ref)
        aevn_ref[...] = _dot(_unpack_rows(u0_ref[...]), ht_ref[...]).astype(BF16)
        acc_ref[...] = jnp.zeros_like(acc_ref)

    n_split = 2 if tm % (2 * 2 * LANES) == 0 else 1
    lane_ranges = [slice(s * (tm // n_split), (s + 1) * (tm // n_split)) for s in range(n_split)]
    stages = ((uodd_ref, aodd_ref, aevn_ref, hevn_ref), (uevn_ref, aevn_ref, aodd_ref, hodd_ref))

    def next_scores(s, lanes):
        u_next, a_next, _, _ = stages[s]
        a_next[:, lanes] = _dot(_unpack_rows(u_next[...]), ht_ref[:, lanes]).astype(BF16)

    def gates(s, lanes):
        _, _, a_cur, hbuf_ref = stages[s]
        _peer_gate(a_cur, (2 * e + s) * keys_per_block, e0_ref, e1_ref, th_ref, hbuf_ref, lanes)

    def values(s, lanes):
        hbuf_ref = stages[s][3]
        vt = _unpack_rows(vt_ref[:, s * eb:(s + 1) * eb])
        acc_ref[:, lanes] += _dot(vt, hbuf_ref[:, lanes])

    chains = [(s, lanes) for s in range(len(stages)) for lanes in lane_ranges]
    next_scores(*chains[0])
    for c, chain in enumerate(chains):
        if c + 1 < len(chains):
            next_scores(*chains[c + 1])
        gates(*chain)
        values(*chain)

    @pl.when(e == ne - 1)
    def _():
        y = x_ref[...] + mod_ref[5:6, :] * acc_ref[...].T
        if final_norm:
            y = _rms(y) * fg_ref[...]
        o_ref[...] = y


def _fold_kernel(k_ref, w_ref, o_ref):
    k = k_ref[...]
    w = w_ref[...]
    k_hi = k.astype(BF16)
    k_lo = (k - k_hi.astype(F32)).astype(BF16)
    w_hi = w.astype(BF16)
    w_lo = (w - w_hi.astype(F32)).astype(BF16)
    o_ref[...] = (_dot(k_hi, w_hi) + _dot(k_lo, w_hi) + _dot(k_hi, w_lo)).astype(BF16)


def _fold_call(keys_p, wq_t):
    _, m, k = keys_p.shape
    n = wq_t.shape[2]
    tn = 256
    return pl.pallas_call(
        _fold_kernel,
        grid=(2, n // tn),
        in_specs=[pl.BlockSpec((None, m, k), lambda p, j: (p, 0, 0)),
                  pl.BlockSpec((None, k, tn), lambda p, j: (p, 0, j))],
        out_specs=pl.BlockSpec((None, m, tn), lambda p, j: (p, 0, j)),
        out_shape=jax.ShapeDtypeStruct((2, m, n), BF16),
        compiler_params=_cparams(2),
        name="fold_keys",
    )(keys_p, wq_t)


def _peer_call(h2, x, mod, final_g, w_s, u_b, vt_b, *, final_norm):
    b, r, d = x.shape
    tm = min(PEER_TOKENS, r)
    eb = PEER_EXPERTS
    n_blocks = 2 * u_b.shape[0] // eb
    nh = PEER_HEADS
    row = lambda: pl.BlockSpec((None, tm, d), lambda i, t, e: (i, t, 0))
    full = lambda shape: pl.BlockSpec(shape, lambda i, t, e: (0,) * len(shape))
    return pl.pallas_call(
        functools.partial(_peer_kernel, final_norm=final_norm),
        grid=(b, r // tm, n_blocks // 2),
        in_specs=[row(), row(),
                  pl.BlockSpec((None, 6, d), lambda i, t, e: (i, 0, 0)),
                  full((1, d)), full(w_s.shape),
                  pl.BlockSpec((eb // 2, d), lambda i, t, e: (0, 0)),
                  pl.BlockSpec((eb // 2, d), lambda i, t, e: (2 * e + 1, 0)),
                  pl.BlockSpec((eb // 2, d), lambda i, t, e: (jnp.minimum(2 * e + 2, n_blocks - 2), 0)),
                  pl.BlockSpec((d // 2, 2 * eb), lambda i, t, e: (0, e))],
        out_specs=row(),
        out_shape=jax.ShapeDtypeStruct((b, r, d), F32),
        scratch_shapes=[pltpu.VMEM((d, tm), BF16),
                        pltpu.VMEM((N_KEYS, nh, tm), F32),
                        pltpu.VMEM((nh, N_KEYS, tm), BF16),
                        pltpu.VMEM((nh, 2 * SUBLANES, tm), BF16),
                        pltpu.VMEM((N_KEYS * nh, tm), F32),
                        pltpu.VMEM((tm // LANES, N_KEYS * nh, LANES), F32),
                        pltpu.VMEM((nh, tm), F32),
                        pltpu.VMEM((nh, tm), F32),
                        pltpu.VMEM((nh, tm), F32),
                        pltpu.VMEM((eb, tm), BF16),
                        pltpu.VMEM((eb, tm), BF16),
                        pltpu.VMEM((eb, tm), BF16),
                        pltpu.VMEM((eb, tm), BF16),
                        pltpu.VMEM((d, tm), F32)],
        compiler_params=_cparams(3),
        name="peer_final" if final_norm else "peer",
    )(h2, x, mod, final_g, w_s, u_b, u_b, u_b, vt_b)


def _rope_tables(length):
    rows = length // GRID_W
    row = jnp.repeat(jnp.arange(rows), GRID_W).astype(F32)
    col = jnp.tile(jnp.arange(GRID_W), rows).astype(F32)
    inv = ROPE_THETA ** (-jnp.arange(ROPE_FREQS, dtype=F32) / ROPE_FREQS)
    ar = row[:, None] * inv[None, :]
    ac = col[:, None] * inv[None, :]
    ang = jnp.concatenate([ar, ar, ac, ac, ar, ar, ac, ac], axis=-1)
    return jnp.cos(ang), jnp.sin(ang)


def _rot_cols(w, heads):
    r = w.reshape(w.shape[0], heads, 4, ROPE_FREQS)
    return jnp.stack([-r[:, :, 1], r[:, :, 0], -r[:, :, 3], r[:, :, 2]], axis=2).reshape(w.shape)


def _swap_heads(w):
    return jnp.concatenate([w[:, HEAD_DIM:], w[:, :HEAD_DIM]], axis=1)


def _augment_w_in(w):
    wq = w[:, SGU_END:Q_END]
    wk = w[:, Q_END:K_END]
    wv = w[:, K_END:]
    wkr = _rot_cols(wk, N_KV_HEADS)
    cols = [w[:, :SGU_END], wq, _rot_cols(wq, N_HEADS), wk, _swap_heads(wk), wkr, _swap_heads(wkr),
            wv, _swap_heads(wv)]
    return jnp.concatenate(cols, axis=1).astype(BF16)


def kernel(x, c, ctx, c_ctx, w_ada, b_ada, norm1_g, norm2_g, w_in, conv_w, sgu_norm_g, sgu_w, sgu_b,
           attn_sink, mix_norm_g, w_out, peer_wq, peer_keys, peer_u, peer_v, final_g):
    bsz, length, d = x.shape
    n_ctx = ctx.shape[1]
    depth = w_ada.shape[0]
    nh = PEER_HEADS

    cc = jnp.zeros((SUBLANES, d), F32).at[:bsz].set(c).at[bsz].set(c_ctx)
    mod = _mod_call(cc, w_ada, b_ada)

    cos_l, sin_l = _rope_tables(length)
    cos_c = jnp.ones((n_ctx, 2 * HEAD_DIM), F32)
    sin_c = jnp.zeros((n_ctx, 2 * HEAD_DIM), F32)
    fg = final_g.reshape(1, d)

    xl, xc = x, ctx
    for i in range(depth):
        last = i == depth - 1
        mod_l = mod[i, :bsz].reshape(bsz, 6, d)
        mod_c = jnp.broadcast_to(mod[i, bsz].reshape(1, 6, d), (bsz, 6, d))
        n1g = norm1_g[i].reshape(1, d)
        n2g = norm2_g[i].reshape(1, d)
        w_aug = _augment_w_in(w_in[i])
        sgu_g = sgu_norm_g[i].reshape(1, D_SGU)
        sgu_wb = sgu_w[i].astype(BF16)
        sgu_bias = jnp.repeat(sgu_b[i].T, D_SGU // SGU_HEADS, axis=1)
        sink_b = jnp.broadcast_to(attn_sink[i][:, None], (N_HEADS, LANES))
        mix_g = mix_norm_g[i].reshape(1, d)
        w_out_b = w_out[i].astype(BF16)
        wq_t =peer_wq[i].reshape(d, nh, 2, PEER_DHALF).transpose(2, 1, 3, 0).reshape(2 * nh * PEER_DHALF, d)
        keys_p = jnp.einsum('hpid,hg->pihgd', peer_keys[i], jnp.eye(nh, dtype=F32))
        keys_p = keys_p.reshape(2, N_KEYS * nh, nh * PEER_DHALF)
        w_s = _fold_call(keys_p, wq_t.reshape(2, nh * PEER_DHALF, d))
        w_s = jnp.stack([_pack_rows(w_s[0]), _pack_rows(w_s[1])])
        u_b = _pack_rows(peer_u[i].astype(BF16))
        vt_b = _pack_rows(peer_v[i].T.astype(BF16))
        mixer_w = (conv_w[i], sgu_g, sgu_wb, sgu_bias, sink_b, mix_g, w_out_b, n2g)

        pc_c, ps_c, q_c, kv_c = _in_call(xc, mod_c, n1g, cos_c, sin_c, w_aug)
        pc_l, ps_l, q_l, kv_l = _in_call(xl, mod_l, n1g, cos_l, sin_l, w_aug)
        xl, h2_l = _mix_call(xl, pc_l, ps_l, q_l, kv_l, kv_c, mod_l, *mixer_w, local=True)
        if not last:
            xc, h2_c = _mix_call(xc, pc_c, ps_c, q_c, kv_c, kv_c, mod_c, *mixer_w, local=False)
            xc = _peer_call(h2_c, xc, mod_c, fg, w_s, u_b, vt_b, final_norm=False)
        xl = _peer_call(h2_l, xl, mod_l, fg, w_s, u_b, vt_b, final_norm=last)
    return xl
```

```python
import functools
import math

import jax
import jax.numpy as jnp
from jax import lax
from jax.experimental import pallas as pl
from jax.experimental.pallas import tpu as pltpu

F32 = jnp.float32
BF16 = jnp.bfloat16

EPS = 1e-6
GRID_W = 64
D_CONV = 256
D_SGU = 256
SGU_HEADS = 4
SGU_CHUNK = 128
N_HEADS = 8
N_KV_HEADS = 2
HEAD_DIM = 64
D_ATTN = N_HEADS * HEAD_DIM
BLOCK = 128
ROPE_THETA = 10000.0
ROPE_FREQS = HEAD_DIM // 4
CONV_END = 3 * D_CONV
SGU_END = CONV_END + 2 * D_SGU
Q_END = SGU_END + D_ATTN
K_END = Q_END + N_KV_HEADS * HEAD_DIM
N_KEYS = 128
PEER_HEADS = 8
PEER_TOPK = 16
PEER_DHALF = 128

LANES = 128
SUBLANES = 8
VMEM_LIMIT_BYTES = 56 * 1024 * 1024

ROW_TILE = 512
PEER_TOKENS = 512
PEER_EXPERTS = 1024

_QO = SGU_END
_QR = _QO + D_ATTN
_KA = _QR + D_ATTN
_KB = _KA + 128
_KAR = _KB + 128
_KBR = _KAR + 128
_VA = _KBR + 128
_VB = _VA + 128
D_AUG = _VB + 128

_CAND = [(k, l) for k in range(PEER_TOPK) for l in range(PEER_TOPK) if (k + 1) * (l + 1) <= PEER_TOPK]


def _cparams(n_axes):
    return pltpu.CompilerParams(dimension_semantics=("arbitrary",) * n_axes,
                                vmem_limit_bytes=VMEM_LIMIT_BYTES)


def _gelu(x):
    c = math.sqrt(2.0 / math.pi)
    return 0.5 * x * (1.0 + jnp.tanh(c * (x + 0.044715 * (x * x * x))))


def _dot(a, b):
    return jnp.dot(a, b, preferred_element_type=F32)


def _dot_nt(a, b):
    return lax.dot_general(a, b, (((1,), (1,)), ((), ())), preferred_element_type=F32)


def _pack_rows_in_kernel(w):
    return pltpu.bitcast(w, jnp.int32)


def _unpack_rows(x):
    return pltpu.bitcast(x, BF16)


def _rms(x):
    return x * lax.rsqrt(jnp.mean(x * x, axis=-1, keepdims=True) + EPS)


def _mod_kernel(c_ref, w_ref, b_ref, o_ref):
    c = c_ref[...]
    sc = c / (1.0 + jnp.exp(-c))
    w = w_ref[...]
    c_hi = sc.astype(BF16)
    c_lo = (sc - c_hi.astype(F32)).astype(BF16)
    w_hi = w.astype(BF16)
    w_lo = (w - w_hi.astype(F32)).astype(BF16)
    o_ref[...] = _dot(c_hi, w_hi) + _dot(c_lo, w_hi) + _dot(c_hi, w_lo) + b_ref[...]


def _mod_call(cc, w_ada, b_ada):
    depth, d, n = w_ada.shape
    tn = 1536
    return pl.pallas_call(
        _mod_kernel,
        grid=(depth, n // tn),
        in_specs=[pl.BlockSpec((SUBLANES, d), lambda i, j: (0, 0)),
                  pl.BlockSpec((None, d, tn), lambda i, j: (i, 0, j)),
                  pl.BlockSpec((None, 1, tn), lambda i, j: (i, 0, j))],
        out_specs=pl.BlockSpec((None, SUBLANES, tn), lambda i, j: (i, 0, j)),
        out_shape=jax.ShapeDtypeStruct((depth, SUBLANES, n), F32),
        compiler_params=_cparams(2),
        name="adaln_mod",
    )(cc, w_ada, b_ada.reshape(depth, 1, n))


def _in_kernel(x_ref, mod_ref, g_ref, cos_ref, sin_ref, w_ref, pc_ref, ps_ref, q_ref, kv_ref):
    x = x_ref[...]
    h = _rms(x) * g_ref[...] * (1.0 + mod_ref[1:2, :]) + mod_ref[0:1, :]
    hb = h.astype(BF16)

    def proj(lo, hi):
        return _dot(hb, w_ref[:, lo:hi])

    pc_ref[...] = proj(0, CONV_END)
    ps_ref[...] = proj(CONV_END, SGU_END)
    cos = cos_ref[...]
    sin = sin_ref[...]
    cos4 = jnp.concatenate([cos] * 4, axis=1)
    sin4 = jnp.concatenate([sin] * 4, axis=1)
    scale = HEAD_DIM ** -0.5
    q = (proj(_QO, _QO + D_ATTN) * cos4 + proj(_QR, _QR + D_ATTN) * sin4) * scale
    q_ref[...] = q.astype(BF16)
    ka = proj(_KA, _KA + 128) * cos + proj(_KAR, _KAR + 128) * sin
    kb = proj(_KB, _KB + 128) * cos + proj(_KBR, _KBR + 128) * sin
    kv_ref[:, 0:128] = ka.astype(BF16)
    kv_ref[:, 128:256] = kb.astype(BF16)
    kv_ref[:, 256:512] = proj(_VA, _VA + 256).astype(BF16)


def _in_call(x, mod, norm_g, cos2, sin2, w_aug):
    b, r, d = x.shape
    tm = min(ROW_TILE, r)
    row = lambda n: pl.BlockSpec((None, tm, n), lambda i, t: (i, t, 0))
    return pl.pallas_call(
        _in_kernel,
        grid=(b, r // tm),
        in_specs=[row(d),
                  pl.BlockSpec((None, 6, d), lambda i, t: (i, 0, 0)),
                  pl.BlockSpec((1, d), lambda i, t: (0, 0)),
                  pl.BlockSpec((tm, 128), lambda i, t: (t, 0)),
                  pl.BlockSpec((tm, 128), lambda i, t: (t, 0)),
                  pl.BlockSpec((d, D_AUG), lambda i, t: (0, 0))],
        out_specs=[row(CONV_END), row(2 * D_SGU), row(D_ATTN), row(512)],
        out_shape=[jax.ShapeDtypeStruct((b, r, CONV_END), F32),
                   jax.ShapeDtypeStruct((b, r, 2 * D_SGU), F32),
                   jax.ShapeDtypeStruct((b, r, D_ATTN), BF16),
                   jax.ShapeDtypeStruct((b, r, 512), BF16)],
        compiler_params=_cparams(2),
        name="in_proj",
    )(x, mod, norm_g, cos2, sin2, w_aug)


def _fold_lanes(blocks, op):
    parts = [b[:, c:c + LANES] for b in blocks for c in range(0, b.shape[1], LANES)]
    out = parts[0]
    for p in parts[1:]:
        out = op(out, p)
    return out


def _mix_kernel(x_ref, pc_ref, pcp_ref, pcn_ref, ps_ref, q_ref, kv_ref, kvp_ref, kvn_ref, kvc_ref,
                mod_ref, convw_ref, sgug_ref, sguw_ref, sgub_ref, sink_ref, mixg_ref, wout_ref, n2g_ref,
                xo_ref, h2_ref, kvx_ref, attn_ref, sgu_ref, *, local):
    tq = x_ref.shape[0]
    nblk = tq // BLOCK
    t = pl.program_id(1)
    nt = pl.num_programs(1)

    pc = pc_ref[...]
    z = pc[:, D_CONV:2 * D_CONV] * pc[:, 2 * D_CONV:]
    z_before = pcp_ref[7:8, D_CONV:2 * D_CONV] * pcp_ref[7:8, 2 * D_CONV:]
    z_after = pcn_ref[0:1, D_CONV:2 * D_CONV] * pcn_ref[0:1, 2 * D_CONV:]
    z_before = z_before * (t > 0).astype(F32)
    z_after = z_after * (t < nt - 1).astype(F32)
    rows = lax.broadcasted_iota(jnp.int32, (tq, D_CONV), 0)
    z_prev = jnp.where(rows == 0, z_before, pltpu.roll(z, 1, axis=0))
    z_next = jnp.where(rows == tq - 1, z_after, pltpu.roll(z, tq - 1, axis=0))
    conv = pc[:, :D_CONV] * (z_prev * convw_ref[0:1, :] + z * convw_ref[1:2, :] + z_next * convw_ref[2:3, :])

    zg = _gelu(ps_ref[...])
    u = zg[:, :D_SGU]
    v = zg[:, D_SGU:]
    mu = jnp.mean(v, axis=-1, keepdims=True)
    vc = v - mu
    vn = vc * lax.rsqrt(jnp.mean(vc * vc, axis=-1, keepdims=True) + EPS) * sgug_ref[...]
    lane = lax.broadcasted_iota(jnp.int32, (BLOCK, LANES), 1)
    low = lane < HEAD_DIM
    for cb in range(nblk):
        pieces = []
        for a in range(SGU_HEADS // 2):
            vp = vn[cb * BLOCK:(cb + 1) * BLOCK, a * LANES:(a + 1) * LANES]
            v_lo = jnp.where(low, vp, 0.0).astype(BF16)
            v_hi = jnp.where(low, 0.0, vp).astype(BF16)
            pieces.append(_dot(sguw_ref[2 * a], v_lo) + _dot(sguw_ref[2 * a + 1], v_hi))
        s = jnp.concatenate(pieces, axis=1) + sgub_ref[...]
        sgu_ref[cb * BLOCK:(cb + 1) * BLOCK, :] = u[cb * BLOCK:(cb + 1) * BLOCK, :] * s

    if local:
        kvx_ref[0:BLOCK, :] = kvp_ref[...]
        kvx_ref[BLOCK:BLOCK + tq, :] = kv_ref[...]
        kvx_ref[BLOCK + tq:, :] = kvn_ref[...]
    kvc = kvc_ref[...]
    qi = lax.broadcasted_iota(jnp.int32, (BLOCK, 3 * BLOCK), 0)
    ko = lax.broadcasted_iota(jnp.int32, (BLOCK, 3 * BLOCK), 1)
    band = (ko >= qi) & (ko <= qi + 2 * BLOCK)

    def attend(jb, carry):
        r0 = pl.multiple_of(jb * BLOCK, BLOCK)
        if local:
            n = t * nblk + jb
            first_key = jnp.where(n > 0, 0, BLOCK)
            end_key = jnp.where(n < nt * nblk - 1, 3 * BLOCK, 2 * BLOCK)
            ok = band & (ko >= first_key) & (ko < end_key)
            kvl = kvx_ref[pl.ds(r0, 3 * BLOCK), :]
        heads = []
        for a in range(N_HEADS // 2):
            qp = q_ref[pl.ds(r0, BLOCK), a * LANES:(a + 1) * LANES]
            q_lo = jnp.where(low, qp, jnp.zeros_like(qp))
            q_hi = jnp.where(low, jnp.zeros_like(qp), qp)
            first = a < N_HEADS // 4
            for qh, even in ((q_lo, True), (q_hi, False)):
                natural = first == even
                ksel = slice(0, 128) if natural else slice(128, 256)
                vsel = slice(256, 384) if natural else slice(384, 512)
                s_list = [_dot_nt(qh, kvc[:, ksel])]
                v_list = [kvc[:, vsel]]
                if local:
                    s_list.append(jnp.where(ok, _dot_nt(qh, kvl[:, ksel]), -jnp.inf))
                    v_list.append(kvl[:, vsel])
                heads.append((s_list, v_list))
        probs = []
        for hd, (s_list, _) in enumerate(heads):
            sink = sink_ref[hd:hd + 1, 0:1]
            m = jnp.maximum(sink, jnp.max(_fold_lanes(s_list, jnp.maximum), axis=-1, keepdims=True))
            ps = [jnp.exp(s - m) for s in s_list]
            denom = jnp.exp(sink - m) + jnp.sum(_fold_lanes(ps, jnp.add), axis=-1, keepdims=True)
            probs.append(([p.astype(BF16) for p in ps], 1.0 / denom))
        outs = []
        for (ps, rden), (_, v_list) in zip(probs, heads):
            o = _dot(ps[0], v_list[0])
            for p, vv in zip(ps[1:], v_list[1:]):
                o = o + _dot(p, vv)
            outs.append(o * rden)
        for a in range(N_HEADS // 2):
            attn_ref[pl.ds(r0, BLOCK), a * LANES:(a + 1) * LANES] = jnp.where(low, outs[2 * a], outs[2 * a + 1])
        return carry

    lax.fori_loop(0, nblk, attend, 0)

    g = mixg_ref[...]
    yc = (_rms(conv) * g[:, :D_CONV]).astype(BF16)
    ys = (_rms(sgu_ref[...]) * g[:, D_CONV:D_CONV + D_SGU]).astype(BF16)
    ya = (_rms(attn_ref[...]) * g[:, D_CONV + D_SGU:]).astype(BF16)
    yl = (_dot(yc, wout_ref[0:D_CONV, :]) + _dot(ys, wout_ref[D_CONV:D_CONV + D_SGU, :])
          + _dot(ya, wout_ref[D_CONV + D_SGU:, :]))
    xn = x_ref[...] + mod_ref[2:3, :] * yl
    xo_ref[...] = xn
    h2 = _rms(xn) * n2g_ref[...] * (1.0 + mod_ref[4:5, :]) + mod_ref[3:4, :]
    h2_ref[...] = h2.astype(BF16)


def _mix_call(x, pc, ps, q, kv, kvc, mod, conv_w, sgu_g, sgu_w, sgu_b, sink_b, mix_g, w_out, n2g, *, local):
    b, r, d = x.shape
    c = kvc.shape[1]
    tq = min(ROW_TILE, r)
    nt = r // tq
    hb = tq // SUBLANES
    kb = tq // BLOCK
    row = lambda n: pl.BlockSpec((None, tq, n), lambda i, t: (i, t, 0))
    full = lambda shape: pl.BlockSpec(shape, lambda i, t: (0,) * len(shape))
    return pl.pallas_call(
        functools.partial(_mix_kernel, local=local),
        grid=(b, nt),
        in_specs=[row(d), row(CONV_END),
                  pl.BlockSpec((None, SUBLANES, CONV_END), lambda i, t: (i, jnp.maximum(t * hb - 1, 0), 0)),
                  pl.BlockSpec((None, SUBLANES, CONV_END), lambda i, t: (i, jnp.minimum((t + 1) * hb, nt * hb - 1), 0)),
                  row(2 * D_SGU), row(D_ATTN), row(512),
                  pl.BlockSpec((None, BLOCK, 512), lambda i, t: (i, jnp.maximum(t * kb - 1, 0), 0)),
                  pl.BlockSpec((None, BLOCK, 512), lambda i, t: (i, jnp.minimum((t + 1) * kb, nt * kb - 1), 0)),
                  pl.BlockSpec((None, c, 512), lambda i, t: (i, 0, 0)),
                  pl.BlockSpec((None, 6, d), lambda i, t: (i, 0, 0)),
                  full((3, D_CONV)), full((1, D_SGU)), full((SGU_HEADS, SGU_CHUNK, SGU_CHUNK)),
                  full((SGU_CHUNK, D_SGU)), full((N_HEADS, LANES)), full((1, d)), full((d, d)), full((1, d))],
        out_specs=[row(d), row(d)],
        out_shape=[jax.ShapeDtypeStruct((b, r, d), F32), jax.ShapeDtypeStruct((b, r, d), BF16)],
        scratch_shapes=[pltpu.VMEM((tq + 2 * BLOCK, 512), BF16),
                        pltpu.VMEM((tq, D_ATTN), F32),
                        pltpu.VMEM((tq, D_SGU), F32)],
        compiler_params=_cparams(2),
        name="mixers_local" if local else "mixers_ctx",
    )(x, pc, pc, pc, ps, q, kv, kv, kv, kvc, mod, conv_w, sgu_g, sgu_w, sgu_b, sink_b, mix_g, w_out, n2g)


def _oddeven_merge(lo, hi, r):
    step = r * 2
    if step < hi - lo:
        yield from _oddeven_merge(lo, hi, step)
        yield from _oddeven_merge(lo + r, hi, step)
        yield from [(i, i + r) for i in range(lo + r, hi - r, step)]
    else:
        yield (lo, lo + r)


def _oddeven_sort(lo, hi):
    if hi - lo >= 1:
        mid = lo + (hi - lo) // 2
        yield from _oddeven_sort(lo, mid)
        yield from _oddeven_sort(mid + 1, hi)
        yield from _oddeven_merge(lo, hi, 1)


_SORT16 = tuple(_oddeven_sort(0, PEER_TOPK - 1))


def _sort16(x):
    x = list(x)
    for i, j in _SORT16:
        x[i], x[j] = jnp.maximum(x[i], x[j]), jnp.minimum(x[i], x[j])
    return x


def _merge_top16(a, b):
    n = PEER_TOPK
    c = [jnp.maximum(a[i], b[n - 1 - i]) for i in range(n)]
    d = n // 2
    while d:
        for i in range(n):
            if not i & d:
                c[i], c[i + d] = jnp.maximum(c[i], c[i + d]), jnp.minimum(c[i], c[i + d])
        d //= 2
    return c


def _top16(load):
    def tree(lo, n):
        if n == PEER_TOPK:
            return _sort16([load(lo + i) for i in range(n)])
        return _merge_top16(tree(lo, n // 2), tree(lo + n // 2, n // 2))
    return tree(0, N_KEYS)


def _peer_select(h_ref, ws_ref, ht_ref, e0_ref, e1_ref, th_ref, s0_ref, s1_ref, a0_ref, b0_ref, rz_ref):
    tm = h_ref.shape[0]
    nh = PEER_HEADS
    ht_ref[...] = h_ref[...].astype(F32).T.astype(BF16)
    ht = ht_ref[...]
    s0_ref[...] = _dot(_unpack_rows(ws_ref[0]), ht)
    s1 = _dot(_unpack_rows(ws_ref[1]), ht)
    for lc in range(tm // LANES):
        s1_ref[lc] = s1[:, lc * LANES:(lc + 1) * LANES]

    rnd = lambda v: v.astype(BF16).astype(F32)

    def select(lc, carry):
        lanes = pl.ds(pl.multiple_of(lc * LANES, LANES), LANES)
        a = _top16(lambda i: s0_ref[i * nh:(i + 1) * nh, lanes])
        b = _top16(lambda j: s1_ref[lc, j * nh:(j + 1) * nh, :])
        ea = [jnp.exp(v - a[0]) for v in a]
        eb = [jnp.exp(v - b[0]) for v in b]
        cand = [ea[k] * eb[l] for k, l in _CAND]
        rest = cand[PEER_TOPK:]
        rest = rest + [jnp.full_like(cand[0], -1.0)] * (-len(rest) % PEER_TOPK)
        best = cand[:PEER_TOPK]
        for g in range(0, len(rest), PEER_TOPK):
            best = _merge_top16(best, _sort16(rest[g:g + PEER_TOPK]))
        top = best[PEER_TOPK - 1]
        zsum = jnp.zeros_like(top)
        for p in cand:
            zsum = zsum + jnp.where(p >= top, p, 0.0)
        rz = 1.0 / zsum
        ean = [rnd(v * rz) for v in ea]
        ebn = [rnd(v) for v in eb]
        thn = jnp.full_like(top, jnp.inf)
        for (k, l), p in zip(_CAND, cand):
            thn = jnp.minimum(thn, jnp.where(p >= top, rnd(ean[k] * ebn[l]), jnp.inf))
        for hh in range(nh):
            th_ref[hh, :, lanes] = jnp.broadcast_to(thn[hh:hh + 1, :], (2 * SUBLANES, LANES)).astype(BF16)
        a0_ref[:, lanes] = a[0]
        b0_ref[:, lanes] = b[0]
        rz_ref[:, lanes] = rz
        return carry

    lax.fori_loop(0, tm // LANES, select, 0)

    a0 = a0_ref[...]
    rz = rz_ref[...]
    for i in range(N_KEYS):
        e0_ref[i] = jnp.exp(s0_ref[i * nh:(i + 1) * nh, :] - a0) * rz
    for hh in range(nh):
        for lc in range(tm // LANES):
            lanes = slice(lc * LANES, (lc + 1) * LANES)
            s1_head = s1_ref[lc, pl.ds(hh, N_KEYS, stride=nh), :]
            e1_ref[hh, :, lanes] = jnp.exp(s1_head - b0_ref[hh:hh + 1, lanes]).astype(BF16)


def _gelu_sigmoid_form(x):
    k0 = -2.0 * math.sqrt(2.0 / math.pi) * math.log2(math.e)
    k1 = 0.044715 * k0
    return x / (1.0 + jnp.exp2(x * (x * x * k1 + k0)))


def _peer_gate(a_ref, row0, e0_ref, e1_ref, th_ref, hbuf_ref, lanes):
    n_lanes = lanes.stop - lanes.start
    pack = 2 * SUBLANES
    for ii in range(a_ref.shape[0] // N_KEYS):
        e0 = e0_ref[row0 + ii, :, lanes]
        e0r = [jnp.broadcast_to(e0[hh:hh + 1, :], (pack, n_lanes)).astype(BF16) for hh in range(PEER_HEADS)]
        for c in range(N_KEYS // pack):
            gate = None
            for hh in range(PEER_HEADS):
                p = e0r[hh] * e1_ref[hh, c * pack:(c + 1) * pack, lanes]
                sel = jnp.where(p >= th_ref[hh, :, lanes], p, jnp.zeros_like(p))
                gate = sel if gate is None else gate + sel
            r0 = ii * N_KEYS + c * pack
            act = _gelu_sigmoid_form(a_ref[r0:r0 + pack, lanes])
            hbuf_ref[r0:r0 + pack, lanes] = act * gate


def _peer_kernel(h_ref, x_ref, mod_ref, fg_ref, ws_ref, u0_ref, uodd_ref, uevn_ref, vt_ref, o_ref,
                 ht_ref, e0_ref, e1_ref, th_ref, s0_ref, s1_ref, a0_ref, b0_ref, rz_ref,
                 aevn_ref, aodd_ref, hevn_ref, hodd_ref, acc_ref, *, final_norm):
    e = pl.program_id(2)
    ne = pl.num_programs(2)
    tm = h_ref.shape[0]
    eb = 2 * u0_ref.shape[0]
    keys_per_block = eb // N_KEYS

    @pl.when(e == 0)
    def _():
        _peer_select(h_ref, ws_ref, ht_ref, e0_ref, e1_ref, th_ref, s0_ref, s1_ref, a0_ref, b0_ref, rz_ref)
        aevn_ref[...] = _dot(_unpack_rows(u0_ref[...]), ht_ref[...]).astype(BF16)
        acc_ref[...] = jnp.zeros_like(acc_ref)

    n_split = 2 if tm % (2 * 2 * LANES) == 0 else 1
    lane_ranges = [slice(s * (tm // n_split), (s + 1) * (tm // n_split)) for s in range(n_split)]
    stages = ((uodd_ref, aodd_ref, aevn_ref, hevn_ref), (uevn_ref, aevn_ref, aodd_ref, hodd_ref))

    def next_scores(s, lanes):
        u_next, a_next, _, _ = stages[s]
        a_next[:, lanes] = _dot(_unpack_rows(u_next[...]), ht_ref[:, lanes]).astype(BF16)

    def gates(s, lanes):
        _, _, a_cur, hbuf_ref = stages[s]
        _peer_gate(a_cur, (2 * e + s) * keys_per_block, e0_ref, e1_ref, th_ref, hbuf_ref, lanes)

    def values(s, lanes):
        hbuf_ref = stages[s][3]
        vt = _unpack_rows(vt_ref[:, s * eb:(s + 1) * eb])
        acc_ref[:, lanes] += _dot(vt, hbuf_ref[:, lanes])

    chains = [(s, lanes) for s in range(len(stages)) for lanes in lane_ranges]
    next_scores(*chains[0])
    for c, chain in enumerate(chains):
        if c + 1 < len(chains):
            next_scores(*chains[c + 1])
        gates(*chain)
        values(*chain)

    @pl.when(e == ne - 1)
    def _():
        y = x_ref[...] + mod_ref[5:6, :] * acc_ref[...].T
        if final_norm:
            y = _rms(y) * fg_ref[...]
        o_ref[...] = y


def _tables_kernel(u_ref, v_ref, up_ref, vtp_ref):
    up_ref[...] = _pack_rows_in_kernel(u_ref[...].astype(BF16))
    vtp_ref[...] = _pack_rows_in_kernel(v_ref[...].T.astype(BF16))


def _tables_call(peer_u, peer_v):
    depth, n_exp, d = peer_u.shape
    eb = PEER_EXPERTS
    return pl.pallas_call(
        _tables_kernel,
        grid=(depth, n_exp // eb),
        in_specs=[pl.BlockSpec((None, eb, d), lambda i, e: (i, e, 0)),
                  pl.BlockSpec((None, eb, d), lambda i, e: (i, e, 0))],
        out_specs=[pl.BlockSpec((None, eb // 2, d), lambda i, e: (i, e, 0)),
                   pl.BlockSpec((None, d // 2, eb), lambda i, e: (i, 0, e))],
        out_shape=[jax.ShapeDtypeStruct((depth, n_exp // 2, d), jnp.int32),
                   jax.ShapeDtypeStruct((depth, d // 2, n_exp), jnp.int32)],
        compiler_params=_cparams(2),
        name="pack_tables",
    )(peer_u, peer_v)


def _fold_kernel(k_ref, w_ref, o_ref):
    k = k_ref[...]
    w = w_ref[...]
    k_hi = k.astype(BF16)
    k_lo = (k - k_hi.astype(F32)).astype(BF16)
    w_hi = w.astype(BF16)
    w_lo = (w - w_hi.astype(F32)).astype(BF16)
    o_ref[...] = _pack_rows_in_kernel((_dot(k_hi, w_hi) + _dot(k_lo, w_hi) + _dot(k_hi, w_lo)).astype(BF16))


def _fold_call(keys_p, wq_t):
    _, m, k = keys_p.shape
    n = wq_t.shape[2]
    tn = 256
    return pl.pallas_call(
        _fold_kernel,
        grid=(2, n // tn),
        in_specs=[pl.BlockSpec((None, m, k), lambda p, j: (p, 0, 0)),
                  pl.BlockSpec((None, k, tn), lambda p, j: (p, 0, j))],
        out_specs=pl.BlockSpec((None, m // 2, tn), lambda p, j: (p, 0, j)),
        out_shape=jax.ShapeDtypeStruct((2, m // 2, n), jnp.int32),
        compiler_params=_cparams(2),
        name="fold_keys",
    )(keys_p, wq_t)


def _peer_call(h2, x, mod, final_g, w_s, u_b, vt_b, *, final_norm):
    b, r, d = x.shape
    tm = min(PEER_TOKENS, r)
    eb = PEER_EXPERTS
    n_blocks = 2 * u_b.shape[0] // eb
    nh = PEER_HEADS
    row = lambda: pl.BlockSpec((None, tm, d), lambda i, t, e: (i, t, 0))
    full = lambda shape: pl.BlockSpec(shape, lambda i, t, e: (0,) * len(shape))
    return pl.pallas_call(
        functools.partial(_peer_kernel, final_norm=final_norm),
        grid=(b, r // tm, n_blocks // 2),
        in_specs=[row(), row(),
                  pl.BlockSpec((None, 6, d), lambda i, t, e: (i, 0, 0)),
                  full((1, d)), full(w_s.shape),
                  pl.BlockSpec((eb // 2, d), lambda i, t, e: (0, 0)),
                  pl.BlockSpec((eb // 2, d), lambda i, t, e: (2 * e + 1, 0)),
                  pl.BlockSpec((eb // 2, d), lambda i, t, e: (jnp.minimum(2 * e + 2, n_blocks - 2), 0)),
                  pl.BlockSpec((d // 2, 2 * eb), lambda i, t, e: (0, e))],
        out_specs=row(),
        out_shape=jax.ShapeDtypeStruct((b, r, d), F32),
        scratch_shapes=[pltpu.VMEM((d, tm), BF16),
                        pltpu.VMEM((N_KEYS, nh, tm), F32),
                        pltpu.VMEM((nh, N_KEYS, tm), BF16),
                        pltpu.VMEM((nh, 2 * SUBLANES, tm), BF16),
                        pltpu.VMEM((N_KEYS * nh, tm), F32),
                        pltpu.VMEM((tm // LANES, N_KEYS * nh, LANES), F32),
                        pltpu.VMEM((nh, tm), F32),
                        pltpu.VMEM((nh, tm), F32),
                        pltpu.VMEM((nh, tm), F32),
                        pltpu.VMEM((eb, tm), BF16),
                        pltpu.VMEM((eb, tm), BF16),
                        pltpu.VMEM((eb, tm), BF16),
                        pltpu.VMEM((eb, tm), BF16),
                        pltpu.VMEM((d, tm), F32)],
        compiler_params=_cparams(3),
        name="peer_final" if final_norm else "peer",
    )(h2, x, mod, final_g, w_s, u_b, u_b, u_b, vt_b)


def _rope_tables(length):
    rows = length // GRID_W
    row = jnp.repeat(jnp.arange(rows), GRID_W).astype(F32)
    col = jnp.tile(jnp.arange(GRID_W), rows).astype(F32)
    inv = ROPE_THETA ** (-jnp.arange(ROPE_FREQS, dtype=F32) / ROPE_FREQS)
    ar = row[:, None] * inv[None, :]
    ac = col[:, None] * inv[None, :]
    ang = jnp.concatenate([ar, ar, ac, ac, ar, ar, ac, ac], axis=-1)
    return jnp.cos(ang), jnp.sin(ang)


def _rot_cols(w, heads):
    r = w.reshape(w.shape[0], heads, 4, ROPE_FREQS)
    return jnp.stack([-r[:, :, 1], r[:, :, 0], -r[:, :, 3], r[:, :, 2]], axis=2).reshape(w.shape)


def _swap_heads(w):
    return jnp.concatenate([w[:, HEAD_DIM:], w[:, :HEAD_DIM]], axis=1)


def _augment_w_in(w):
    wq = w[:, SGU_END:Q_END]
    wk = w[:, Q_END:K_END]
    wv = w[:, K_END:]
    wkr = _rot_cols(wk, N_KV_HEADS)
    cols = [w[:, :SGU_END], wq, _rot_cols(wq, N_HEADS), wk, _swap_heads(wk), wkr, _swap_heads(wkr),
            wv, _swap_heads(wv)]
    return jnp.concatenate(cols, axis=1).astype(BF16)


def kernel(x, c, ctx, c_ctx, w_ada, b_ada, norm1_g, norm2_g, w_in, conv_w, sgu_norm_g, sgu_w, sgu_b,
           attn_sink, mix_norm_g, w_out, peer_wq, peer_keys, peer_u, peer_v, final_g):
    bsz, length, d = x.shape
    n_ctx = ctx.shape[1]
    depth = w_ada.shape[0]
    nh = PEER_HEADS

    cc = jnp.zeros((SUBLANES, d), F32).at[:bsz].set(c).at[bsz].set(c_ctx)
    mod = _mod_call(cc, w_ada, b_ada)

    cos_l, sin_l = _rope_tables(length)
    cos_c = jnp.ones((n_ctx, 2 * HEAD_DIM), F32)
    sin_c = jnp.zeros((n_ctx, 2 * HEAD_DIM), F32)
    fg = final_g.reshape(1, d)
    u_packed, vt_packed = _tables_call(peer_u, peer_v)

    xl, xc = x, ctx
    for i in range(depth):
        last = i == depth - 1
        mod_l = mod[i, :bsz].reshape(bsz, 6, d)
        mod_c = jnp.broadcast_to(mod[i, bsz].reshape(1, 6, d), (bsz, 6, d))
        n1g = norm1_g[i].reshape(1, d)
        n2g = norm2_g[i].reshape(1, d)
        w_aug = _augment_w_in(w_in[i])
        sgu_g = sgu_norm_g[i].reshape(1, D_SGU)
        sgu_wb = sgu_w[i].astype(BF16)
        sgu_bias = jnp.repeat(sgu_b[i].T, D_SGU // SGU_HEADS, axis=1)
        sink_b = jnp.broadcast_to(attn_sink[i][:, None], (N_HEADS, LANES))
        mix_g = mix_norm_g[i].reshape(1, d)
        w_out_b = w_out[i].astype(BF16)
        wq_t =peer_wq[i].reshape(d, nh, 2, PEER_DHALF).transpose(2, 1, 3, 0).reshape(2 * nh * PEER_DHALF, d)
        keys_p = jnp.einsum('hpid,hg->pihgd', peer_keys[i], jnp.eye(nh, dtype=F32))
        keys_p = keys_p.reshape(2, N_KEYS * nh, nh * PEER_DHALF)
        w_s = _fold_call(keys_p, wq_t.reshape(2, nh * PEER_DHALF, d))
        u_b = u_packed[i]
        vt_b = vt_packed[i]
        mixer_w = (conv_w[i], sgu_g, sgu_wb, sgu_bias, sink_b, mix_g, w_out_b, n2g)

        pc_c, ps_c, q_c, kv_c = _in_call(xc, mod_c, n1g, cos_c, sin_c, w_aug)
        pc_l, ps_l, q_l, kv_l = _in_call(xl, mod_l, n1g, cos_l, sin_l, w_aug)
        xl, h2_l = _mix_call(xl, pc_l, ps_l, q_l, kv_l, kv_c, mod_l, *mixer_w, local=True)
        if not last:
            xc, h2_c = _mix_call(xc, pc_c, ps_c, q_c, kv_c, kv_c, mod_c, *mixer_w, local=False)
            xc = _peer_call(h2_c, xc, mod_c, fg, w_s, u_b, vt_b, final_norm=False)
        xl = _peer_call(h2_l, xl, mod_l, fg, w_s, u_b, vt_b, final_norm=last)
    return xl
```

```python
import functools
import math

import jax
import jax.numpy as jnp
from jax import lax
from jax.experimental import pallas as pl
from jax.experimental.pallas import tpu as pltpu

F32 = jnp.float32
BF16 = jnp.bfloat16

EPS = 1e-6
GRID_W = 64
D_CONV = 256
D_SGU = 256
SGU_HEADS = 4
SGU_CHUNK = 128
N_HEADS = 8
N_KV_HEADS = 2
HEAD_DIM = 64
D_ATTN = N_HEADS * HEAD_DIM
BLOCK = 128
ROPE_THETA = 10000.0
ROPE_FREQS = HEAD_DIM // 4
CONV_END = 3 * D_CONV
SGU_END = CONV_END + 2 * D_SGU
Q_END = SGU_END + D_ATTN
K_END = Q_END + N_KV_HEADS * HEAD_DIM
N_KEYS = 128
PEER_HEADS = 8
PEER_TOPK = 16
PEER_DHALF = 128

LANES = 128
SUBLANES = 8
VMEM_LIMIT_BYTES = 56 * 1024 * 1024

ROW_TILE = 512
PEER_TOKENS = 512
PEER_EXPERTS = 1024

_QO = SGU_END
_QR = _QO + D_ATTN
_KA = _QR + D_ATTN
_KB = _KA + 128
_KAR = _KB + 128
_KBR = _KAR + 128
_VA = _KBR + 128
_VB = _VA + 128
D_AUG = _VB + 128

_CAND = [(k, l) for k in range(PEER_TOPK) for l in range(PEER_TOPK) if (k + 1) * (l + 1) <= PEER_TOPK]


def _cparams(n_axes):
    return pltpu.CompilerParams(dimension_semantics=("arbitrary",) * n_axes,
                                vmem_limit_bytes=VMEM_LIMIT_BYTES)


def _gelu(x):
    c = math.sqrt(2.0 / math.pi)
    return 0.5 * x * (1.0 + jnp.tanh(c * (x + 0.044715 * (x * x * x))))


def _dot(a, b):
    return jnp.dot(a, b, preferred_element_type=F32)


def _dot_nt(a, b):
    return lax.dot_general(a, b, (((1,), (1,)), ((), ())), preferred_element_type=F32)


def _pack_rows_in_kernel(w):
    return pltpu.bitcast(w, jnp.int32)


def _unpack_rows(x):
    return pltpu.bitcast(x, BF16)


def _rms(x):
    return x * lax.rsqrt(jnp.mean(x * x, axis=-1, keepdims=True) + EPS)


def _mod_kernel(c_ref, w_ref, b_ref, o_ref):
    c = c_ref[...]
    sc = c / (1.0 + jnp.exp(-c))
    w = w_ref[...]
    c_hi = sc.astype(BF16)
    c_lo = (sc - c_hi.astype(F32)).astype(BF16)
    w_hi = w.astype(BF16)
    w_lo = (w - w_hi.astype(F32)).astype(BF16)
    o_ref[...] = _dot(c_hi, w_hi) + _dot(c_lo, w_hi) + _dot(c_hi, w_lo) + b_ref[...]


def _mod_call(cc, w_ada, b_ada):
    depth, d, n = w_ada.shape
    tn = 1536
    return pl.pallas_call(
        _mod_kernel,
        grid=(depth, n // tn),
        in_specs=[pl.BlockSpec((SUBLANES, d), lambda i, j: (0, 0)),
                  pl.BlockSpec((None, d, tn), lambda i, j: (i, 0, j)),
                  pl.BlockSpec((None, 1, tn), lambda i, j: (i, 0, j))],
        out_specs=pl.BlockSpec((None, SUBLANES, tn), lambda i, j: (i, 0, j)),
        out_shape=jax.ShapeDtypeStruct((depth, SUBLANES, n), F32),
        compiler_params=_cparams(2),
        name="adaln_mod",
    )(cc, w_ada, b_ada.reshape(depth, 1, n))


def _in_kernel(x_ref, mod_ref, g_ref, cos_ref, sin_ref, w_ref, pc_ref, ps_ref, q_ref, kv_ref):
    x = x_ref[...]
    h = _rms(x) * g_ref[...] * (1.0 + mod_ref[1:2, :]) + mod_ref[0:1, :]
    hb = h.astype(BF16)

    def proj(lo, hi):
        return _dot(hb, w_ref[:, lo:hi])

    pc_ref[...] = proj(0, CONV_END)
    ps_ref[...] = proj(CONV_END, SGU_END)
    cos = cos_ref[...]
    sin = sin_ref[...]
    cos4 = jnp.concatenate([cos] * 4, axis=1)
    sin4 = jnp.concatenate([sin] * 4, axis=1)
    scale = HEAD_DIM ** -0.5
    q = (proj(_QO, _QO + D_ATTN) * cos4 + proj(_QR, _QR + D_ATTN) * sin4) * scale
    q_ref[...] = q.astype(BF16)
    ka = proj(_KA, _KA + 128) * cos + proj(_KAR, _KAR + 128) * sin
    kb = proj(_KB, _KB + 128) * cos + proj(_KBR, _KBR + 128) * sin
    kv_ref[:, 0:128] = ka.astype(BF16)
    kv_ref[:, 128:256] = kb.astype(BF16)
    kv_ref[:, 256:512] = proj(_VA, _VA + 256).astype(BF16)


def _in_call(x, mod, norm_g, cos2, sin2, w_aug):
    b, r, d = x.shape
    tm = min(ROW_TILE, r)
    row = lambda n: pl.BlockSpec((None, tm, n), lambda i, t: (i, t, 0))
    return pl.pallas_call(
        _in_kernel,
        grid=(b, r // tm),
        in_specs=[row(d),
                  pl.BlockSpec((None, 6, d), lambda i, t: (i, 0, 0)),
                  pl.BlockSpec((1, d), lambda i, t: (0, 0)),
                  pl.BlockSpec((tm, 128), lambda i, t: (t, 0)),
                  pl.BlockSpec((tm, 128), lambda i, t: (t, 0)),
                  pl.BlockSpec((d, D_AUG), lambda i, t: (0, 0))],
        out_specs=[row(CONV_END), row(2 * D_SGU), row(D_ATTN), row(512)],
        out_shape=[jax.ShapeDtypeStruct((b, r, CONV_END), F32),
                   jax.ShapeDtypeStruct((b, r, 2 * D_SGU), F32),
                   jax.ShapeDtypeStruct((b, r, D_ATTN), BF16),
                   jax.ShapeDtypeStruct((b, r, 512), BF16)],
        compiler_params=_cparams(2),
        name="in_proj",
    )(x, mod, norm_g, cos2, sin2, w_aug)


def _fold_lanes(blocks, op):
    parts = [b[:, c:c + LANES] for b in blocks for c in range(0, b.shape[1], LANES)]
    out = parts[0]
    for p in parts[1:]:
        out = op(out, p)
    return out


def _mix_kernel(x_ref, pc_ref, pcp_ref, pcn_ref, ps_ref, q_ref, kv_ref, kvp_ref, kvn_ref, kvc_ref,
                mod_ref, convw_ref, sgug_ref, sguw_ref, sgub_ref, sink_ref, mixg_ref, wout_ref, n2g_ref,
                xo_ref, h2_ref, kvx_ref, attn_ref, sgu_ref, *, local):
    tq = x_ref.shape[0]
    nblk = tq // BLOCK
    t = pl.program_id(1)
    nt = pl.num_programs(1)

    pc = pc_ref[...]
    z = pc[:, D_CONV:2 * D_CONV] * pc[:, 2 * D_CONV:]
    z_before = pcp_ref[7:8, D_CONV:2 * D_CONV] * pcp_ref[7:8, 2 * D_CONV:]
    z_after = pcn_ref[0:1, D_CONV:2 * D_CONV] * pcn_ref[0:1, 2 * D_CONV:]
    z_before = z_before * (t > 0).astype(F32)
    z_after = z_after * (t < nt - 1).astype(F32)
    rows = lax.broadcasted_iota(jnp.int32, (tq, D_CONV), 0)
    z_prev = jnp.where(rows == 0, z_before, pltpu.roll(z, 1, axis=0))
    z_next = jnp.where(rows == tq - 1, z_after, pltpu.roll(z, tq - 1, axis=0))
    conv = pc[:, :D_CONV] * (z_prev * convw_ref[0:1, :] + z * convw_ref[1:2, :] + z_next * convw_ref[2:3, :])

    zg = _gelu(ps_ref[...])
    u = zg[:, :D_SGU]
    v = zg[:, D_SGU:]
    mu = jnp.mean(v, axis=-1, keepdims=True)
    vc = v - mu
    vn = vc * lax.rsqrt(jnp.mean(vc * vc, axis=-1, keepdims=True) + EPS) * sgug_ref[...]
    lane = lax.broadcasted_iota(jnp.int32, (BLOCK, LANES), 1)
    low = lane < HEAD_DIM
    for cb in range(nblk):
        pieces = []
        for a in range(SGU_HEADS // 2):
            vp = vn[cb * BLOCK:(cb + 1) * BLOCK, a * LANES:(a + 1) * LANES]
            v_lo = jnp.where(low, vp, 0.0).astype(BF16)
            v_hi = jnp.where(low, 0.0, vp).astype(BF16)
            pieces.append(_dot(sguw_ref[2 * a], v_lo) + _dot(sguw_ref[2 * a + 1], v_hi))
        s = jnp.concatenate(pieces, axis=1) + sgub_ref[...]
        sgu_ref[cb * BLOCK:(cb + 1) * BLOCK, :] = u[cb * BLOCK:(cb + 1) * BLOCK, :] * s

    if local:
        kvx_ref[0:BLOCK, :] = kvp_ref[...]
        kvx_ref[BLOCK:BLOCK + tq, :] = kv_ref[...]
        kvx_ref[BLOCK + tq:, :] = kvn_ref[...]
    kvc = kvc_ref[...]
    qi = lax.broadcasted_iota(jnp.int32, (BLOCK, 3 * BLOCK), 0)
    ko = lax.broadcasted_iota(jnp.int32, (BLOCK, 3 * BLOCK), 1)
    band = (ko >= qi) & (ko <= qi + 2 * BLOCK)

    def attend(jb, carry):
        r0 = pl.multiple_of(jb * BLOCK, BLOCK)
        if local:
            n = t * nblk + jb
            first_key = jnp.where(n > 0, 0, BLOCK)
            end_key = jnp.where(n < nt * nblk - 1, 3 * BLOCK, 2 * BLOCK)
            ok = band & (ko >= first_key) & (ko < end_key)
            kvl = kvx_ref[pl.ds(r0, 3 * BLOCK), :]
        heads = []
        for a in range(N_HEADS // 2):
            qp = q_ref[pl.ds(r0, BLOCK), a * LANES:(a + 1) * LANES]
            q_lo = jnp.where(low, qp, jnp.zeros_like(qp))
            q_hi = jnp.where(low, jnp.zeros_like(qp), qp)
            first = a < N_HEADS // 4
            for qh, even in ((q_lo, True), (q_hi, False)):
                natural = first == even
                ksel = slice(0, 128) if natural else slice(128, 256)
                vsel = slice(256, 384) if natural else slice(384, 512)
                s_list = [_dot_nt(qh, kvc[:, ksel])]
                v_list = [kvc[:, vsel]]
                if local:
                    s_list.append(jnp.where(ok, _dot_nt(qh, kvl[:, ksel]), -jnp.inf))
                    v_list.append(kvl[:, vsel])
                heads.append((s_list, v_list))
        probs = []
        for hd, (s_list, _) in enumerate(heads):
            sink = sink_ref[hd:hd + 1, 0:1]
            m = jnp.maximum(sink, jnp.max(_fold_lanes(s_list, jnp.maximum), axis=-1, keepdims=True))
            ps = [jnp.exp(s - m) for s in s_list]
            denom = jnp.exp(sink - m) + jnp.sum(_fold_lanes(ps, jnp.add), axis=-1, keepdims=True)
            probs.append(([p.astype(BF16) for p in ps], 1.0 / denom))
        outs = []
        for (ps, rden), (_, v_list) in zip(probs, heads):
            o = _dot(ps[0], v_list[0])
            for p, vv in zip(ps[1:], v_list[1:]):
                o = o + _dot(p, vv)
            outs.append(o * rden)
        for a in range(N_HEADS // 2):
            attn_ref[pl.ds(r0, BLOCK), a * LANES:(a + 1) * LANES] = jnp.where(low, outs[2 * a], outs[2 * a + 1])
        return carry

    lax.fori_loop(0, nblk, attend, 0)

    g = mixg_ref[...]
    yc = (_rms(conv) * g[:, :D_CONV]).astype(BF16)
    ys = (_rms(sgu_ref[...]) * g[:, D_CONV:D_CONV + D_SGU]).astype(BF16)
    ya = (_rms(attn_ref[...]) * g[:, D_CONV + D_SGU:]).astype(BF16)
    yl = (_dot(yc, wout_ref[0:D_CONV, :]) + _dot(ys, wout_ref[D_CONV:D_CONV + D_SGU, :])
          + _dot(ya, wout_ref[D_CONV + D_SGU:, :]))
    xn = x_ref[...] + mod_ref[2:3, :] * yl
    xo_ref[...] = xn
    h2 = _rms(xn) * n2g_ref[...] * (1.0 + mod_ref[4:5, :]) + mod_ref[3:4, :]
    h2_ref[...] = h2.astype(BF16)


def _mix_call(x, pc, ps, q, kv, kvc, mod, conv_w, sgu_g, sgu_w, sgu_b, sink_b, mix_g, w_out, n2g, *, local):
    b, r, d = x.shape
    c = kvc.shape[1]
    tq = min(ROW_TILE, r)
    nt = r // tq
    hb = tq // SUBLANES
    kb = tq // BLOCK
    row = lambda n: pl.BlockSpec((None, tq, n), lambda i, t: (i, t, 0))
    full = lambda shape: pl.BlockSpec(shape, lambda i, t: (0,) * len(shape))
    return pl.pallas_call(
        functools.partial(_mix_kernel, local=local),
        grid=(b, nt),
        in_specs=[row(d), row(CONV_END),
                  pl.BlockSpec((None, SUBLANES, CONV_END), lambda i, t: (i, jnp.maximum(t * hb - 1, 0), 0)),
                  pl.BlockSpec((None, SUBLANES, CONV_END), lambda i, t: (i, jnp.minimum((t + 1) * hb, nt * hb - 1), 0)),
                  row(2 * D_SGU), row(D_ATTN), row(512),
                  pl.BlockSpec((None, BLOCK, 512), lambda i, t: (i, jnp.maximum(t * kb - 1, 0), 0)),
                  pl.BlockSpec((None, BLOCK, 512), lambda i, t: (i, jnp.minimum((t + 1) * kb, nt * kb - 1), 0)),
                  pl.BlockSpec((None, c, 512), lambda i, t: (i, 0, 0)),
                  pl.BlockSpec((None, 6, d), lambda i, t: (i, 0, 0)),
                  full((3, D_CONV)), full((1, D_SGU)), full((SGU_HEADS, SGU_CHUNK, SGU_CHUNK)),
                  full((SGU_CHUNK, D_SGU)), full((N_HEADS, LANES)), full((1, d)), full((d, d)), full((1, d))],
        out_specs=[row(d), row(d)],
        out_shape=[jax.ShapeDtypeStruct((b, r, d), F32), jax.ShapeDtypeStruct((b, r, d), BF16)],
        scratch_shapes=[pltpu.VMEM((tq + 2 * BLOCK, 512), BF16),
                        pltpu.VMEM((tq, D_ATTN), F32),
                        pltpu.VMEM((tq, D_SGU), F32)],
        compiler_params=_cparams(2),
        name="mixers_local" if local else "mixers_ctx",
    )(x, pc, pc, pc, ps, q, kv, kv, kv, kvc, mod, conv_w, sgu_g, sgu_w, sgu_b, sink_b, mix_g, w_out, n2g)


def _oddeven_merge(lo, hi, r):
    step = r * 2
    if step < hi - lo:
        yield from _oddeven_merge(lo, hi, step)
        yield from _oddeven_merge(lo + r, hi, step)
        yield from [(i, i + r) for i in range(lo + r, hi - r, step)]
    else:
        yield (lo, lo + r)


def _oddeven_sort(lo, hi):
    if hi - lo >= 1:
        mid = lo + (hi - lo) // 2
        yield from _oddeven_sort(lo, mid)
        yield from _oddeven_sort(mid + 1, hi)
        yield from _oddeven_merge(lo, hi, 1)


_SORT16 = tuple(_oddeven_sort(0, PEER_TOPK - 1))


def _sort16(x):
    x = list(x)
    for i, j in _SORT16:
        x[i], x[j] = jnp.maximum(x[i], x[j]), jnp.minimum(x[i], x[j])
    return x


def _merge_top16(a, b):
    n = PEER_TOPK
    c = [jnp.maximum(a[i], b[n - 1 - i]) for i in range(n)]
    d = n // 2
    while d:
        for i in range(n):
            if not i & d:
                c[i], c[i + d] = jnp.maximum(c[i], c[i + d]), jnp.minimum(c[i], c[i + d])
        d //= 2
    return c


def _top16(load):
    def tree(lo, n):
        if n == PEER_TOPK:
            return _sort16([load(lo + i) for i in range(n)])
        return _merge_top16(tree(lo, n // 2), tree(lo + n // 2, n // 2))
    return tree(0, N_KEYS)


def _peer_select(h_ref, ws_ref, u0_ref, ht_ref, e0_ref, e1_ref, th_ref, s0_ref, s1_ref, a0_ref):
    tm = h_ref.shape[0]
    nh = PEER_HEADS
    ht_ref[...] = h_ref[...].astype(F32).T.astype(BF16)
    s0 = _dot(_unpack_rows(ws_ref[0]), ht_ref[...])
    s1 = _dot(_unpack_rows(ws_ref[1]), ht_ref[...])
    for c in range(tm // LANES):
        s0_ref[c] = s0[:, c * LANES:(c + 1) * LANES]
        s1_ref[c] = s1[:, c * LANES:(c + 1) * LANES]

    rnd = lambda v: v.astype(BF16).astype(F32)

    def select(c, carry):
        lanes = pl.ds(pl.multiple_of(c * LANES, LANES), LANES)
        a0_ref[:, lanes] = _dot(_unpack_rows(u0_ref[...]), ht_ref[:, lanes]).astype(BF16)
        a = _top16(lambda i: s0_ref[c, i * nh:(i + 1) * nh, :])
        b = _top16(lambda j: s1_ref[c, j * nh:(j + 1) * nh, :])
        ea = [jnp.exp(v - a[0]) for v in a]
        eb = [jnp.exp(v - b[0]) for v in b]
        cand = [ea[k] * eb[l] for k, l in _CAND]
        rest = cand[PEER_TOPK:]
        rest = rest + [jnp.full_like(cand[0], -1.0)] * (-len(rest) % PEER_TOPK)
        best = cand[:PEER_TOPK]
        for g in range(0, len(rest), PEER_TOPK):
            best = _merge_top16(best, _sort16(rest[g:g + PEER_TOPK]))
        top = best[PEER_TOPK - 1]
        zsum = jnp.zeros_like(top)
        for p in cand:
            zsum = zsum + jnp.where(p >= top, p, 0.0)
        rz = 1.0 / zsum
        ean = [rnd(v * rz) for v in ea]
        ebn = [rnd(v) for v in eb]
        thn = jnp.full_like(top, jnp.inf)
        for (k, l), p in zip(_CAND, cand):
            thn = jnp.minimum(thn, jnp.where(p >= top, rnd(ean[k] * ebn[l]), jnp.inf))
        for hh in range(nh):
            th_ref[hh, :, lanes] = jnp.broadcast_to(thn[hh:hh + 1, :], (2 * SUBLANES, LANES)).astype(BF16)
        for i in range(N_KEYS):
            e0_ref[i, :, lanes] = jnp.exp(s0_ref[c, i * nh:(i + 1) * nh, :] - a[0]) * rz
        for hh in range(nh):
            s1_head = s1_ref[c, pl.ds(hh, N_KEYS, stride=nh), :]
            e1_ref[hh, :, lanes] = jnp.exp(s1_head - b[0][hh:hh + 1, :]).astype(BF16)
        return carry

    lax.fori_loop(0, tm // LANES, select, 0)


def _gelu_sigmoid_form(x):
    k0 = -2.0 * math.sqrt(2.0 / math.pi) * math.log2(math.e)
    k1 = 0.044715 * k0
    return x / (1.0 + jnp.exp2(x * (x * x * k1 + k0)))


def _peer_gate(a_ref, row0, e0_ref, e1_ref, th_ref, hbuf_ref, lanes):
    n_lanes = lanes.stop - lanes.start
    pack = 2 * SUBLANES
    for ii in range(a_ref.shape[0] // N_KEYS):
        e0 = e0_ref[row0 + ii, :, lanes]
        e0r = [jnp.broadcast_to(e0[hh:hh + 1, :], (pack, n_lanes)).astype(BF16) for hh in range(PEER_HEADS)]
        for c in range(N_KEYS // pack):
            gate = None
            for hh in range(PEER_HEADS):
                p = e0r[hh] * e1_ref[hh, c * pack:(c + 1) * pack, lanes]
                sel = jnp.where(p >= th_ref[hh, :, lanes], p, jnp.zeros_like(p))
                gate = sel if gate is None else gate + sel
            r0 = ii * N_KEYS + c * pack
            act = _gelu_sigmoid_form(a_ref[r0:r0 + pack, lanes])
            hbuf_ref[r0:r0 + pack, lanes] = act * gate


def _peer_lane_split(tm):
    return 2 if tm % (2 * 2 * LANES) == 0 else 1


def _peer_kernel(h_ref, x_ref, mod_ref, fg_ref, ws_ref, u0_ref, uodd_ref, uevn_ref, vt_ref, o_ref,
                 ht_ref, e0_ref, e1_ref, th_ref, *scratch, final_norm):
    e = pl.program_id(2)
    ne = pl.num_programs(2)
    tm = h_ref.shape[0]
    n_split = _peer_lane_split(tm)
    s0_ref, s1_ref, aevn_ref, aodd_ref, hevn_ref, hodd_ref, acc_ref = scratch
    eb = 2 * u0_ref.shape[0]
    keys_per_block = eb // N_KEYS

    @pl.when(e == 0)
    def _():
        _peer_select(h_ref, ws_ref, u0_ref, ht_ref, e0_ref, e1_ref, th_ref, s0_ref, s1_ref, aevn_ref)
        acc_ref[...] = jnp.zeros_like(acc_ref)

    lane_ranges = [slice(s * (tm // n_split), (s + 1) * (tm // n_split)) for s in range(n_split)]
    stages = ((uodd_ref, aodd_ref, aevn_ref, hevn_ref), (uevn_ref, aevn_ref, aodd_ref, hodd_ref))

    def next_scores(s, lanes):
        u_next, a_next, _, _ = stages[s]
        a_next[:, lanes] = _dot(_unpack_rows(u_next[...]), ht_ref[:, lanes]).astype(BF16)

    def gates(s, lanes):
        _, _, a_cur, hbuf_ref = stages[s]
        _peer_gate(a_cur, (2 * e + s) * keys_per_block, e0_ref, e1_ref, th_ref, hbuf_ref, lanes)

    def values(s, lanes):
        hbuf_ref = stages[s][3]
        vt = _unpack_rows(vt_ref[:, s * eb:(s + 1) * eb])
        acc_ref[:, lanes] += _dot(vt, hbuf_ref[:, lanes])

    chains = [(s, lanes) for s in range(len(stages)) for lanes in lane_ranges]
    next_scores(*chains[0])
    for c, chain in enumerate(chains):
        if c + 1 < len(chains):
            next_scores(*chains[c + 1])
        gates(*chain)
        values(*chain)

    @pl.when(e == ne - 1)
    def _():
        y = x_ref[...] + mod_ref[5:6, :] * acc_ref[...].T
        if final_norm:
            y = _rms(y) * fg_ref[...]
        o_ref[...] = y


def _tables_kernel(u_ref, v_ref, up_ref, vtp_ref):
    up_ref[...] = _pack_rows_in_kernel(u_ref[...].astype(BF16))
    vtp_ref[...] = _pack_rows_in_kernel(v_ref[...].T.astype(BF16))


def _tables_call(peer_u, peer_v):
    depth, n_exp, d = peer_u.shape
    eb = PEER_EXPERTS
    return pl.pallas_call(
        _tables_kernel,
        grid=(depth, n_exp // eb),
        in_specs=[pl.BlockSpec((None, eb, d), lambda i, e: (i, e, 0)),
                  pl.BlockSpec((None, eb, d), lambda i, e: (i, e, 0))],
        out_specs=[pl.BlockSpec((None, eb // 2, d), lambda i, e: (i, e, 0)),
                   pl.BlockSpec((None, d // 2, eb), lambda i, e: (i, 0, e))],
        out_shape=[jax.ShapeDtypeStruct((depth, n_exp // 2, d), jnp.int32),
                   jax.ShapeDtypeStruct((depth, d // 2, n_exp), jnp.int32)],
        compiler_params=_cparams(2),
        name="pack_tables",
    )(peer_u, peer_v)


def _fold_kernel(k_ref, w_ref, o_ref):
    k = k_ref[...]
    w = w_ref[...]
    k_hi = k.astype(BF16)
    k_lo = (k - k_hi.astype(F32)).astype(BF16)
    w_hi = w.astype(BF16)
    w_lo = (w - w_hi.astype(F32)).astype(BF16)
    o_ref[...] = _pack_rows_in_kernel((_dot(k_hi, w_hi) + _dot(k_lo, w_hi) + _dot(k_hi, w_lo)).astype(BF16))


def _fold_call(keys_p, wq_t):
    _, m, k = keys_p.shape
    n = wq_t.shape[2]
    tn = 256
    return pl.pallas_call(
        _fold_kernel,
        grid=(2, n // tn),
        in_specs=[pl.BlockSpec((None, m, k), lambda p, j: (p, 0, 0)),
                  pl.BlockSpec((None, k, tn), lambda p, j: (p, 0, j))],
        out_specs=pl.BlockSpec((None, m // 2, tn), lambda p, j: (p, 0, j)),
        out_shape=jax.ShapeDtypeStruct((2, m // 2, n), jnp.int32),
        compiler_params=_cparams(2),
        name="fold_keys",
    )(keys_p, wq_t)


def _peer_call(h2, x, mod, final_g, w_s, u_b, vt_b, *, layer, final_norm):
    b, r, d = x.shape
    tm = min(PEER_TOKENS, r)
    eb = PEER_EXPERTS
    n_blocks = 2 * u_b.shape[1] // eb
    nh = PEER_HEADS
    score_chunks = pltpu.VMEM((tm // LANES, N_KEYS * nh, LANES), F32)
    row = lambda: pl.BlockSpec((None, tm, d), lambda i, t, e: (i, t, 0))
    full = lambda shape: pl.BlockSpec(shape, lambda i, t, e: (0,) * len(shape))
    return pl.pallas_call(
        functools.partial(_peer_kernel, final_norm=final_norm),
        grid=(b, r // tm, n_blocks // 2),
        in_specs=[row(), row(),
                  pl.BlockSpec((None, 6, d), lambda i, t, e: (i, 0, 0)),
                  full((1, d)), full(w_s.shape),
                  pl.BlockSpec((None, eb // 2, d), lambda i, t, e: (layer, 0, 0)),
                  pl.BlockSpec((None, eb // 2, d), lambda i, t, e: (layer, 2 * e + 1, 0)),
                  pl.BlockSpec((None, eb // 2, d),
                               lambda i, t, e: (layer, jnp.minimum(2 * e + 2, n_blocks - 2), 0)),
                  pl.BlockSpec((None, d // 2, 2 * eb), lambda i, t, e: (layer, 0, e))],
        out_specs=row(),
        out_shape=jax.ShapeDtypeStruct((b, r, d), F32),
        scratch_shapes=[pltpu.VMEM((d, tm), BF16),
                        pltpu.VMEM((N_KEYS, nh, tm), F32),
                        pltpu.VMEM((nh, N_KEYS, tm), BF16),
                        pltpu.VMEM((nh, 2 * SUBLANES, tm), BF16),
                        score_chunks,
                        score_chunks,
                        pltpu.VMEM((eb, tm), BF16),
                        pltpu.VMEM((eb, tm), BF16),
                        pltpu.VMEM((eb, tm), BF16),
                        pltpu.VMEM((eb, tm), BF16),
                        pltpu.VMEM((d, tm), F32)],
        compiler_params=_cparams(3),
        name="peer_final" if final_norm else "peer",
    )(h2, x, mod, final_g, w_s, u_b, u_b, u_b, vt_b)


def _rope_tables(length):
    rows = length // GRID_W
    row = jnp.repeat(jnp.arange(rows), GRID_W).astype(F32)
    col = jnp.tile(jnp.arange(GRID_W), rows).astype(F32)
    inv = ROPE_THETA ** (-jnp.arange(ROPE_FREQS, dtype=F32) / ROPE_FREQS)
    ar = row[:, None] * inv[None, :]
    ac = col[:, None] * inv[None, :]
    ang = jnp.concatenate([ar, ar, ac, ac, ar, ar, ac, ac], axis=-1)
    return jnp.cos(ang), jnp.sin(ang)


def _rot_cols(w, heads):
    r = w.reshape(w.shape[0], heads, 4, ROPE_FREQS)
    return jnp.stack([-r[:, :, 1], r[:, :, 0], -r[:, :, 3], r[:, :, 2]], axis=2).reshape(w.shape)


def _swap_heads(w):
    return jnp.concatenate([w[:, HEAD_DIM:], w[:, :HEAD_DIM]], axis=1)


def _augment_w_in(w):
    wq = w[:, SGU_END:Q_END]
    wk = w[:, Q_END:K_END]
    wv = w[:, K_END:]
    wkr = _rot_cols(wk, N_KV_HEADS)
    cols = [w[:, :SGU_END], wq, _rot_cols(wq, N_HEADS), wk, _swap_heads(wk), wkr, _swap_heads(wkr),
            wv, _swap_heads(wv)]
    return jnp.concatenate(cols, axis=1).astype(BF16)


def kernel(x, c, ctx, c_ctx, w_ada, b_ada, norm1_g, norm2_g, w_in, conv_w, sgu_norm_g, sgu_w, sgu_b,
           attn_sink, mix_norm_g, w_out, peer_wq, peer_keys, peer_u, peer_v, final_g):
    bsz, length, d = x.shape
    n_ctx = ctx.shape[1]
    depth = w_ada.shape[0]
    nh = PEER_HEADS

    cc = jnp.zeros((SUBLANES, d), F32).at[:bsz].set(c).at[bsz].set(c_ctx)
    mod = _mod_call(cc, w_ada, b_ada)

    cos_l, sin_l = _rope_tables(length)
    cos_c = jnp.ones((n_ctx, 2 * HEAD_DIM), F32)
    sin_c = jnp.zeros((n_ctx, 2 * HEAD_DIM), F32)
    fg = final_g.reshape(1, d)
    u_packed, vt_packed = _tables_call(peer_u, peer_v)

    xl, xc = x, ctx
    for i in range(depth):
        last = i == depth - 1
        mod_l = mod[i, :bsz].reshape(bsz, 6, d)
        mod_c = jnp.broadcast_to(mod[i, bsz].reshape(1, 6, d), (bsz, 6, d))
        n1g = norm1_g[i].reshape(1, d)
        n2g = norm2_g[i].reshape(1, d)
        w_aug = _augment_w_in(w_in[i])
        sgu_g = sgu_norm_g[i].reshape(1, D_SGU)
        sgu_wb = sgu_w[i].astype(BF16)
        sgu_bias = jnp.repeat(sgu_b[i].T, D_SGU // SGU_HEADS, axis=1)
        sink_b = jnp.broadcast_to(attn_sink[i][:, None], (N_HEADS, LANES))
        mix_g = mix_norm_g[i].reshape(1, d)
        w_out_b = w_out[i].astype(BF16)
        wq_t =peer_wq[i].reshape(d, nh, 2, PEER_DHALF).transpose(2, 1, 3, 0).reshape(2 * nh * PEER_DHALF, d)
        keys_p = jnp.einsum('hpid,hg->pihgd', peer_keys[i], jnp.eye(nh, dtype=F32))
        keys_p = keys_p.reshape(2, N_KEYS * nh, nh * PEER_DHALF)
        w_s = _fold_call(keys_p, wq_t.reshape(2, nh * PEER_DHALF, d))
        peer = functools.partial(_peer_call, w_s=w_s, u_b=u_packed, vt_b=vt_packed, layer=i)
        mixer_w = (conv_w[i], sgu_g, sgu_wb, sgu_bias, sink_b, mix_g, w_out_b, n2g)

        pc_c, ps_c, q_c, kv_c = _in_call(xc, mod_c, n1g, cos_c, sin_c, w_aug)
        pc_l, ps_l, q_l, kv_l = _in_call(xl, mod_l, n1g, cos_l, sin_l, w_aug)
        xl, h2_l = _mix_call(xl, pc_l, ps_l, q_l, kv_l, kv_c, mod_l, *mixer_w, local=True)
        if not last:
            xc, h2_c = _mix_call(xc, pc_c, ps_c, q_c, kv_c, kv_c, mod_c, *mixer_w, local=False)
            xc = peer(h2_c, xc, mod_c, fg, final_norm=False)
        xl = peer(h2_l, xl, mod_l, fg, final_norm=last)
    return xl
```

```python
import functools
import math

import jax
import jax.numpy as jnp
from jax import lax
from jax.experimental import pallas as pl
from jax.experimental.pallas import tpu as pltpu

F32 = jnp.float32
BF16 = jnp.bfloat16

EPS = 1e-6
GRID_W = 64
D_CONV = 256
D_SGU = 256
SGU_HEADS = 4
SGU_CHUNK = 128
N_HEADS = 8
N_KV_HEADS = 2
HEAD_DIM = 64
D_ATTN = N_HEADS * HEAD_DIM
BLOCK = 128
ROPE_THETA = 10000.0
ROPE_FREQS = HEAD_DIM // 4
CONV_END = 3 * D_CONV
SGU_END = CONV_END + 2 * D_SGU
Q_END = SGU_END + D_ATTN
K_END = Q_END + N_KV_HEADS * HEAD_DIM
N_KEYS = 128
PEER_HEADS = 8
PEER_TOPK = 16
PEER_DHALF = 128

LANES = 128
SUBLANES = 8
VMEM_LIMIT_BYTES = 56 * 1024 * 1024

ROW_TILE = 512
PEER_TOKENS = 512
PEER_EXPERTS = 1024

D_IN = K_END + N_KV_HEADS * HEAD_DIM

_CAND = [(k, l) for k in range(PEER_TOPK) for l in range(PEER_TOPK) if (k + 1) * (l + 1) <= PEER_TOPK]


def _cparams(n_axes):
    return pltpu.CompilerParams(dimension_semantics=("arbitrary",) * n_axes,
                                vmem_limit_bytes=VMEM_LIMIT_BYTES)


def _gelu_sigmoid_form(x):
    k0 = -2.0 * math.sqrt(2.0 / math.pi) * math.log2(math.e)
    k1 = 0.044715 * k0
    return x / (1.0 + jnp.exp2(x * (x * x * k1 + k0)))


def _dot(a, b):
    return jnp.dot(a, b, preferred_element_type=F32)


def _dot_nt(a, b):
    return lax.dot_general(a, b, (((1,), (1,)), ((), ())), preferred_element_type=F32)


def _pack_rows_in_kernel(w):
    return pltpu.bitcast(w, jnp.int32)


def _unpack_rows(x):
    return pltpu.bitcast(x, BF16)


def _rms(x):
    return x * lax.rsqrt(jnp.mean(x * x, axis=-1, keepdims=True) + EPS)


def _mod_kernel(c_ref, w_ref, b_ref, o_ref):
    c = c_ref[...]
    sc = c / (1.0 + jnp.exp(-c))
    w = w_ref[...]
    c_hi = sc.astype(BF16)
    c_lo = (sc - c_hi.astype(F32)).astype(BF16)
    w_hi = w.astype(BF16)
    w_lo = (w - w_hi.astype(F32)).astype(BF16)
    o_ref[...] = _dot(c_hi, w_hi) + _dot(c_lo, w_hi) + _dot(c_hi, w_lo) + b_ref[...]


def _mod_call(cc, w_ada, b_ada):
    depth, d, n = w_ada.shape
    tn = 1536
    return pl.pallas_call(
        _mod_kernel,
        grid=(depth, n // tn),
        in_specs=[pl.BlockSpec((SUBLANES, d), lambda i, j: (0, 0)),
                  pl.BlockSpec((None, d, tn), lambda i, j: (i, 0, j)),
                  pl.BlockSpec((None, 1, tn), lambda i, j: (i, 0, j))],
        out_specs=pl.BlockSpec((None, SUBLANES, tn), lambda i, j: (i, 0, j)),
        out_shape=jax.ShapeDtypeStruct((depth, SUBLANES, n), F32),
        compiler_params=_cparams(2),
        name="adaln_mod",
    )(cc, w_ada, b_ada.reshape(depth, 1, n))


def _rotary(x, rope):
    n = x.shape[1]
    cos, sin_first, sin_second = (jnp.concatenate([rope[i]] * (n // LANES), axis=1) for i in range(3))
    from_right = pltpu.roll(x, n - ROPE_FREQS, axis=1)
    from_left = pltpu.roll(x, ROPE_FREQS, axis=1)
    return x * cos + from_right * sin_first + from_left * sin_second


def _in_kernel(x_ref, mod_ref, g_ref, rope_ref, w_ref, pc_ref, ps_ref, q_ref, kv_ref):
    x = x_ref[...]
    h = _rms(x) * g_ref[...] * (1.0 + mod_ref[1:2, :]) + mod_ref[0:1, :]
    hb = h.astype(BF16)

    def proj(lo, hi):
        return _dot(hb, w_ref[:, lo:hi])

    pc_ref[...] = proj(0, CONV_END)
    ps_ref[...] = proj(CONV_END, SGU_END)
    rope = rope_ref[...]
    q_ref[...] = (_rotary(proj(SGU_END, Q_END), rope) * HEAD_DIM ** -0.5).astype(BF16)
    k = _rotary(proj(Q_END, K_END), rope)
    v = proj(K_END, D_IN)
    kv_ref[:, 0:128] = k.astype(BF16)
    kv_ref[:, 128:256] = pltpu.roll(k, HEAD_DIM, axis=1).astype(BF16)
    kv_ref[:, 256:384] = v.astype(BF16)
    kv_ref[:, 384:512] = pltpu.roll(v, HEAD_DIM, axis=1).astype(BF16)


def _in_call(x, mod, norm_g, rope, w_in):
    b, r, d = x.shape
    tm = min(ROW_TILE, r)
    row = lambda n: pl.BlockSpec((None, tm, n), lambda i, t: (i, t, 0))
    return pl.pallas_call(
        _in_kernel,
        grid=(b, r // tm),
        in_specs=[row(d),
                  pl.BlockSpec((None, 6, d), lambda i, t: (i, 0, 0)),
                  pl.BlockSpec((1, d), lambda i, t: (0, 0)),
                  pl.BlockSpec((3, tm, LANES), lambda i, t: (0, t, 0)),
                  pl.BlockSpec((d, D_IN), lambda i, t: (0, 0))],
        out_specs=[row(CONV_END), row(2 * D_SGU), row(D_ATTN), row(512)],
        out_shape=[jax.ShapeDtypeStruct((b, r, CONV_END), F32),
                   jax.ShapeDtypeStruct((b, r, 2 * D_SGU), F32),
                   jax.ShapeDtypeStruct((b, r, D_ATTN), BF16),
                   jax.ShapeDtypeStruct((b, r, 512), BF16)],
        compiler_params=_cparams(2),
        name="in_proj",
    )(x, mod, norm_g, rope, w_in)


def _fold_lanes(blocks, op):
    parts = [b[:, c:c + LANES] for b in blocks for c in range(0, b.shape[1], LANES)]
    out = parts[0]
    for p in parts[1:]:
        out = op(out, p)
    return out


def _mix_kernel(x_ref, pc_ref, pcp_ref, pcn_ref, ps_ref, q_ref, kv_ref, kvp_ref, kvn_ref, kvc_ref,
                mod_ref, convw_ref, sgug_ref, sguw_ref, sgub_ref, sink_ref, mixg_ref, wout_ref, n2g_ref,
                xo_ref, h2_ref, kvx_ref, attn_ref, sgu_ref, *, local):
    tq = x_ref.shape[0]
    nblk = tq // BLOCK
    t = pl.program_id(1)
    nt = pl.num_programs(1)

    pc = pc_ref[...]
    z = pc[:, D_CONV:2 * D_CONV] * pc[:, 2 * D_CONV:]
    z_before = pcp_ref[7:8, D_CONV:2 * D_CONV] * pcp_ref[7:8, 2 * D_CONV:]
    z_after = pcn_ref[0:1, D_CONV:2 * D_CONV] * pcn_ref[0:1, 2 * D_CONV:]
    z_before = z_before * (t > 0).astype(F32)
    z_after = z_after * (t < nt - 1).astype(F32)
    rows = lax.broadcasted_iota(jnp.int32, (tq, D_CONV), 0)
    z_prev = jnp.where(rows == 0, z_before, pltpu.roll(z, 1, axis=0))
    z_next = jnp.where(rows == tq - 1, z_after, pltpu.roll(z, tq - 1, axis=0))
    conv = pc[:, :D_CONV] * (z_prev * convw_ref[0:1, :] + z * convw_ref[1:2, :] + z_next * convw_ref[2:3, :])

    zg = _gelu_sigmoid_form(ps_ref[...])
    u = zg[:, :D_SGU]
    v = zg[:, D_SGU:]
    mu = jnp.mean(v, axis=-1, keepdims=True)
    vc = v - mu
    vn = vc * lax.rsqrt(jnp.mean(vc * vc, axis=-1, keepdims=True) + EPS) * sgug_ref[...]
    lane = lax.broadcasted_iota(jnp.int32, (BLOCK, LANES), 1)
    low = lane < HEAD_DIM
    for cb in range(nblk):
        pieces = []
        for a in range(SGU_HEADS // 2):
            vp = vn[cb * BLOCK:(cb + 1) * BLOCK, a * LANES:(a + 1) * LANES]
            v_lo = jnp.where(low, vp, 0.0).astype(BF16)
            v_hi = jnp.where(low, 0.0, vp).astype(BF16)
            pieces.append(_dot(sguw_ref[2 * a], v_lo) + _dot(sguw_ref[2 * a + 1], v_hi))
        s = jnp.concatenate(pieces, axis=1) + sgub_ref[...]
        sgu_ref[cb * BLOCK:(cb + 1) * BLOCK, :] = u[cb * BLOCK:(cb + 1) * BLOCK, :] * s

    if local:
        kvx_ref[0:BLOCK, :] = kvp_ref[...]
        kvx_ref[BLOCK:BLOCK + tq, :] = kv_ref[...]
        kvx_ref[BLOCK + tq:, :] = kvn_ref[...]
    kvc = kvc_ref[...]
    qi = lax.broadcasted_iota(jnp.int32, (BLOCK, 3 * BLOCK), 0)
    ko = lax.broadcasted_iota(jnp.int32, (BLOCK, 3 * BLOCK), 1)
    band = (ko >= qi) & (ko <= qi + 2 * BLOCK)

    def attend(jb, carry):
        r0 = pl.multiple_of(jb * BLOCK, BLOCK)
        if local:
            n = t * nblk + jb
            first_key = jnp.where(n > 0, 0, BLOCK)
            end_key = jnp.where(n < nt * nblk - 1, 3 * BLOCK, 2 * BLOCK)
            ok = band & (ko >= first_key) & (ko < end_key)
            kvl = kvx_ref[pl.ds(r0, 3 * BLOCK), :]
        heads = []
        for a in range(N_HEADS // 2):
            qp = q_ref[pl.ds(r0, BLOCK), a * LANES:(a + 1) * LANES]
            q_lo = jnp.where(low, qp, jnp.zeros_like(qp))
            q_hi = jnp.where(low, jnp.zeros_like(qp), qp)
            first = a < N_HEADS // 4
            for qh, even in ((q_lo, True), (q_hi, False)):
                natural = first == even
                ksel = slice(0, 128) if natural else slice(128, 256)
                vsel = slice(256, 384) if natural else slice(384, 512)
                s_list = [_dot_nt(qh, kvc[:, ksel])]
                v_list = [kvc[:, vsel]]
                if local:
                    s_list.append(jnp.where(ok, _dot_nt(qh, kvl[:, ksel]), -jnp.inf))
                    v_list.append(kvl[:, vsel])
                heads.append((s_list, v_list))
        probs = []
        for hd, (s_list, _) in enumerate(heads):
            sink = sink_ref[hd:hd + 1, 0:1]
            m = jnp.maximum(sink, jnp.max(_fold_lanes(s_list, jnp.maximum), axis=-1, keepdims=True))
            ps = [jnp.exp(s - m) for s in s_list]
            denom = jnp.exp(sink - m) + jnp.sum(_fold_lanes(ps, jnp.add), axis=-1, keepdims=True)
            probs.append(([p.astype(BF16) for p in ps], 1.0 / denom))
        outs = []
        for (ps, rden), (_, v_list) in zip(probs, heads):
            o = _dot(ps[0], v_list[0])
            for p, vv in zip(ps[1:], v_list[1:]):
                o = o + _dot(p, vv)
            outs.append(o * rden)
        for a in range(N_HEADS // 2):
            attn_ref[pl.ds(r0, BLOCK), a * LANES:(a + 1) * LANES] = jnp.where(low, outs[2 * a], outs[2 * a + 1])
        return carry

    lax.fori_loop(0, nblk, attend, 0)

    g = mixg_ref[...]
    yc = (_rms(conv) * g[:, :D_CONV]).astype(BF16)
    ys = (_rms(sgu_ref[...]) * g[:, D_CONV:D_CONV + D_SGU]).astype(BF16)
    ya = (_rms(attn_ref[...]) * g[:, D_CONV + D_SGU:]).astype(BF16)
    yl = (_dot(yc, wout_ref[0:D_CONV, :]) + _dot(ys, wout_ref[D_CONV:D_CONV + D_SGU, :])
          + _dot(ya, wout_ref[D_CONV + D_SGU:, :]))
    xn = x_ref[...] + mod_ref[2:3, :] * yl
    xo_ref[...] = xn
    h2 = _rms(xn) * n2g_ref[...] * (1.0 + mod_ref[4:5, :]) + mod_ref[3:4, :]
    h2_ref[...] = h2.astype(BF16)


def _mix_call(x, pc, ps, q, kv, kvc, mod, conv_w, sgu_g, sgu_w, sgu_b, sink_b, mix_g, w_out, n2g, *, local):
    b, r, d = x.shape
    c = kvc.shape[1]
    tq = min(ROW_TILE, r)
    nt = r // tq
    hb = tq // SUBLANES
    kb = tq // BLOCK
    row = lambda n: pl.BlockSpec((None, tq, n), lambda i, t: (i, t, 0))
    full = lambda shape: pl.BlockSpec(shape, lambda i, t: (0,) * len(shape))
    return pl.pallas_call(
        functools.partial(_mix_kernel, local=local),
        grid=(b, nt),
        in_specs=[row(d), row(CONV_END),
                  pl.BlockSpec((None, SUBLANES, CONV_END), lambda i, t: (i, jnp.maximum(t * hb - 1, 0), 0)),
                  pl.BlockSpec((None, SUBLANES, CONV_END), lambda i, t: (i, jnp.minimum((t + 1) * hb, nt * hb - 1), 0)),
                  row(2 * D_SGU), row(D_ATTN), row(512),
                  pl.BlockSpec((None, BLOCK, 512), lambda i, t: (i, jnp.maximum(t * kb - 1, 0), 0)),
                  pl.BlockSpec((None, BLOCK, 512), lambda i, t: (i, jnp.minimum((t + 1) * kb, nt * kb - 1), 0)),
                  pl.BlockSpec((None, c, 512), lambda i, t: (i, 0, 0)),
                  pl.BlockSpec((None, 6, d), lambda i, t: (i, 0, 0)),
                  full((3, D_CONV)), full((1, D_SGU)), full((SGU_HEADS, SGU_CHUNK, SGU_CHUNK)),
                  full((SGU_CHUNK, D_SGU)), full((N_HEADS, LANES)), full((1, d)), full((d, d)), full((1, d))],
        out_specs=[row(d), row(d)],
        out_shape=[jax.ShapeDtypeStruct((b, r, d), F32), jax.ShapeDtypeStruct((b, r, d), BF16)],
        scratch_shapes=[pltpu.VMEM((tq + 2 * BLOCK, 512), BF16),
                        pltpu.VMEM((tq, D_ATTN), F32),
                        pltpu.VMEM((tq, D_SGU), F32)],
        compiler_params=_cparams(2),
        name="mixers_local" if local else "mixers_ctx",
    )(x, pc, pc, pc, ps, q, kv, kv, kv, kvc, mod, conv_w, sgu_g, sgu_w, sgu_b, sink_b, mix_g, w_out, n2g)


def _oddeven_merge(lo, hi, r):
    step = r * 2
    if step < hi - lo:
        yield from _oddeven_merge(lo, hi, step)
        yield from _oddeven_merge(lo + r, hi, step)
        yield from [(i, i + r) for i in range(lo + r, hi - r, step)]
    else:
        yield (lo, lo + r)


def _oddeven_sort(lo, hi):
    if hi - lo >= 1:
        mid = lo + (hi - lo) // 2
        yield from _oddeven_sort(lo, mid)
        yield from _oddeven_sort(mid + 1, hi)
        yield from _oddeven_merge(lo, hi, 1)


_SORT16 = tuple(_oddeven_sort(0, PEER_TOPK - 1))


def _sort16(x):
    x = list(x)
    for i, j in _SORT16:
        x[i], x[j] = jnp.maximum(x[i], x[j]), jnp.minimum(x[i], x[j])
    return x


def _merge_top16(a, b):
    n = PEER_TOPK
    c = [jnp.maximum(a[i], b[n - 1 - i]) for i in range(n)]
    d = n // 2
    while d:
        for i in range(n):
            if not i & d:
                c[i], c[i + d] = jnp.maximum(c[i], c[i + d]), jnp.minimum(c[i], c[i + d])
        d //= 2
    return c


def _top16(load):
    def tree(lo, n):
        if n == PEER_TOPK:
            return _sort16([load(lo + i) for i in range(n)])
        return _merge_top16(tree(lo, n // 2), tree(lo + n // 2, n // 2))
    return tree(0, N_KEYS)


def _peer_select(h_ref, ws_ref, u0_ref, ht_ref, e0_ref, e1_ref, th_ref, s0_ref, s1_ref, a0_ref):
    tm = h_ref.shape[0]
    nh = PEER_HEADS
    ht_ref[...] = h_ref[...].astype(F32).T.astype(BF16)
    s0 = _dot(_unpack_rows(ws_ref[0]), ht_ref[...])
    s1 = _dot(_unpack_rows(ws_ref[1]), ht_ref[...])
    for c in range(tm // LANES):
        s0_ref[c] = s0[:, c * LANES:(c + 1) * LANES]
        s1_ref[c] = s1[:, c * LANES:(c + 1) * LANES]

    rnd = lambda v: v.astype(BF16).astype(F32)

    def select(c, carry):
        lanes = pl.ds(pl.multiple_of(c * LANES, LANES), LANES)
        a0_ref[:, lanes] = _dot(_unpack_rows(u0_ref[...]), ht_ref[:, lanes]).astype(BF16)
        a = _top16(lambda i: s0_ref[c, i * nh:(i + 1) * nh, :])
        b = _top16(lambda j: s1_ref[c, j * nh:(j + 1) * nh, :])
        ea = [jnp.exp(v - a[0]) for v in a]
        eb = [jnp.exp(v - b[0]) for v in b]
        cand = [ea[k] * eb[l] for k, l in _CAND]
        rest = cand[PEER_TOPK:]
        rest = rest + [jnp.full_like(cand[0], -1.0)] * (-len(rest) % PEER_TOPK)
        best = cand[:PEER_TOPK]
        for g in range(0, len(rest), PEER_TOPK):
            best = _merge_top16(best, _sort16(rest[g:g + PEER_TOPK]))
        top = best[PEER_TOPK - 1]
        zsum = jnp.zeros_like(top)
        for p in cand:
            zsum = zsum + jnp.where(p >= top, p, 0.0)
        rz = 1.0 / zsum
        ean = [rnd(v * rz) for v in ea]
        ebn = [rnd(v) for v in eb]
        thn = jnp.full_like(top, jnp.inf)
        for (k, l), p in zip(_CAND, cand):
            thn = jnp.minimum(thn, jnp.where(p >= top, rnd(ean[k] * ebn[l]), jnp.inf))
        for hh in range(nh):
            th_ref[hh, :, lanes] = jnp.broadcast_to(thn[hh:hh + 1, :], (2 * SUBLANES, LANES)).astype(BF16)
        for i in range(N_KEYS):
            e0_ref[i, :, lanes] = jnp.exp(s0_ref[c, i * nh:(i + 1) * nh, :] - a[0]) * rz
        for hh in range(nh):
            s1_head = s1_ref[c, pl.ds(hh, N_KEYS, stride=nh), :]
            e1_ref[hh, :, lanes] = jnp.exp(s1_head - b[0][hh:hh + 1, :]).astype(BF16)
        return carry

    lax.fori_loop(0, tm // LANES, select, 0)


def _peer_gate(a_ref, row0, e0_ref, e1_ref, th_ref, hbuf_ref, lanes):
    n_lanes = lanes.stop - lanes.start
    pack = 2 * SUBLANES
    for ii in range(a_ref.shape[0] // N_KEYS):
        e0 = e0_ref[row0 + ii, :, lanes]
        e0r = [jnp.broadcast_to(e0[hh:hh + 1, :], (pack, n_lanes)).astype(BF16) for hh in range(PEER_HEADS)]
        for c in range(N_KEYS // pack):
            gate = None
            for hh in range(PEER_HEADS):
                p = e0r[hh] * e1_ref[hh, c * pack:(c + 1) * pack, lanes]
                sel = jnp.where(p >= th_ref[hh, :, lanes], p, jnp.zeros_like(p))
                gate = sel if gate is None else gate + sel
            r0 = ii * N_KEYS + c * pack
            act = _gelu_sigmoid_form(a_ref[r0:r0 + pack, lanes])
            hbuf_ref[r0:r0 + pack, lanes] = act * gate


def _peer_lane_split(tm):
    return 2 if tm % (2 * 2 * LANES) == 0 else 1


def _peer_kernel(h_ref, x_ref, mod_ref, fg_ref, ws_ref, u0_ref, uodd_ref, uevn_ref, vt_ref, o_ref,
                 ht_ref, e0_ref, e1_ref, th_ref, *scratch, final_norm):
    e = pl.program_id(2)
    ne = pl.num_programs(2)
    tm = h_ref.shape[0]
    n_split = _peer_lane_split(tm)
    s0_ref, s1_ref, aevn_ref, aodd_ref, hevn_ref, hodd_ref, acc_ref = scratch
    eb = 2 * u0_ref.shape[0]
    keys_per_block = eb // N_KEYS

    @pl.when(e == 0)
    def _():
        _peer_select(h_ref, ws_ref, u0_ref, ht_ref, e0_ref, e1_ref, th_ref, s0_ref, s1_ref, aevn_ref)
        acc_ref[...] = jnp.zeros_like(acc_ref)

    lane_ranges = [slice(s * (tm // n_split), (s + 1) * (tm // n_split)) for s in range(n_split)]
    stages = ((uodd_ref, aodd_ref, aevn_ref, hevn_ref), (uevn_ref, aevn_ref, aodd_ref, hodd_ref))

    def next_scores(s, lanes):
        u_next, a_next, _, _ = stages[s]
        a_next[:, lanes] = _dot(_unpack_rows(u_next[...]), ht_ref[:, lanes]).astype(BF16)

    def gates(s, lanes):
        _, _, a_cur, hbuf_ref = stages[s]
        _peer_gate(a_cur, (2 * e + s) * keys_per_block, e0_ref, e1_ref, th_ref, hbuf_ref, lanes)

    def values(s, lanes):
        hbuf_ref = stages[s][3]
        vt = _unpack_rows(vt_ref[:, s * eb:(s + 1) * eb])
        acc_ref[:, lanes] += _dot(vt, hbuf_ref[:, lanes])

    chains = [(s, lanes) for s in range(len(stages)) for lanes in lane_ranges]
    next_scores(*chains[0])
    for c, chain in enumerate(chains):
        if c + 1 < len(chains):
            next_scores(*chains[c + 1])
        gates(*chain)
        values(*chain)

    @pl.when(e == ne - 1)
    def _():
        y = x_ref[...] + mod_ref[5:6, :] * acc_ref[...].T
        if final_norm:
            y = _rms(y) * fg_ref[...]
        o_ref[...] = y


def _tables_kernel(u_ref, v_ref, up_ref, vtp_ref):
    up_ref[...] = _pack_rows_in_kernel(u_ref[...].astype(BF16))
    vtp_ref[...] = _pack_rows_in_kernel(v_ref[...].T.astype(BF16))


def _tables_call(peer_u, peer_v):
    depth, n_exp, d = peer_u.shape
    eb = PEER_EXPERTS
    return pl.pallas_call(
        _tables_kernel,
        grid=(depth, n_exp // eb),
        in_specs=[pl.BlockSpec((None, eb, d), lambda i, e: (i, e, 0)),
                  pl.BlockSpec((None, eb, d), lambda i, e: (i, e, 0))],
        out_specs=[pl.BlockSpec((None, eb // 2, d), lambda i, e: (i, e, 0)),
                   pl.BlockSpec((None, d // 2, eb), lambda i, e: (i, 0, e))],
        out_shape=[jax.ShapeDtypeStruct((depth, n_exp // 2, d), jnp.int32),
                   jax.ShapeDtypeStruct((depth, d // 2, n_exp), jnp.int32)],
        compiler_params=_cparams(2),
        name="pack_tables",
    )(peer_u, peer_v)


def _fold_kernel(k_ref, w_ref, o_ref):
    k = k_ref[...]
    w = w_ref[...]
    k_hi = k.astype(BF16)
    k_lo = (k - k_hi.astype(F32)).astype(BF16)
    w_hi = w.astype(BF16)
    w_lo = (w - w_hi.astype(F32)).astype(BF16)
    o_ref[...] = _pack_rows_in_kernel((_dot(k_hi, w_hi) + _dot(k_lo, w_hi) + _dot(k_hi, w_lo)).astype(BF16))


def _fold_call(keys_p, wq_t):
    _, m, k = keys_p.shape
    n = wq_t.shape[2]
    tn = 256
    return pl.pallas_call(
        _fold_kernel,
        grid=(2, n // tn),
        in_specs=[pl.BlockSpec((None, m, k), lambda p, j: (p, 0, 0)),
                  pl.BlockSpec((None, k, tn), lambda p, j: (p, 0, j))],
        out_specs=pl.BlockSpec((None, m // 2, tn), lambda p, j: (p, 0, j)),
        out_shape=jax.ShapeDtypeStruct((2, m // 2, n), jnp.int32),
        compiler_params=_cparams(2),
        name="fold_keys",
    )(keys_p, wq_t)


def _peer_call(h2, x, mod, final_g, w_s, u_b, vt_b, *, layer, final_norm):
    b, r, d = x.shape
    tm = min(PEER_TOKENS, r)
    eb = PEER_EXPERTS
    n_blocks = 2 * u_b.shape[1] // eb
    nh = PEER_HEADS
    score_chunks = pltpu.VMEM((tm // LANES, N_KEYS * nh, LANES), F32)
    row = lambda: pl.BlockSpec((None, tm, d), lambda i, t, e: (i, t, 0))
    full = lambda shape: pl.BlockSpec(shape, lambda i, t, e: (0,) * len(shape))
    return pl.pallas_call(
        functools.partial(_peer_kernel, final_norm=final_norm),
        grid=(b, r // tm, n_blocks // 2),
        in_specs=[row(), row(),
                  pl.BlockSpec((None, 6, d), lambda i, t, e: (i, 0, 0)),
                  full((1, d)), full(w_s.shape),
                  pl.BlockSpec((None, eb // 2, d), lambda i, t, e: (layer, 0, 0)),
                  pl.BlockSpec((None, eb // 2, d), lambda i, t, e: (layer, 2 * e + 1, 0)),
                  pl.BlockSpec((None, eb // 2, d),
                               lambda i, t, e: (layer, jnp.minimum(2 * e + 2, n_blocks - 2), 0)),
                  pl.BlockSpec((None, d // 2, 2 * eb), lambda i, t, e: (layer, 0, e))],
        out_specs=row(),
        out_shape=jax.ShapeDtypeStruct((b, r, d), F32),
        scratch_shapes=[pltpu.VMEM((d, tm), BF16),
                        pltpu.VMEM((N_KEYS, nh, tm), F32),
                        pltpu.VMEM((nh, N_KEYS, tm), BF16),
                        pltpu.VMEM((nh, 2 * SUBLANES, tm), BF16),
                        score_chunks,
                        score_chunks,
                        pltpu.VMEM((eb, tm), BF16),
                        pltpu.VMEM((eb, tm), BF16),
                        pltpu.VMEM((eb, tm), BF16),
                        pltpu.VMEM((eb, tm), BF16),
                        pltpu.VMEM((d, tm), F32)],
        compiler_params=_cparams(3),
        name="peer_final" if final_norm else "peer",
    )(h2, x, mod, final_g, w_s, u_b, u_b, u_b, vt_b)


def _rope_tables(length):
    rows = length // GRID_W
    row = jnp.repeat(jnp.arange(rows), GRID_W).astype(F32)
    col = jnp.tile(jnp.arange(GRID_W), rows).astype(F32)
    inv = ROPE_THETA ** (-jnp.arange(ROPE_FREQS, dtype=F32) / ROPE_FREQS)
    ar = row[:, None] * inv[None, :]
    ac = col[:, None] * inv[None, :]
    ang = jnp.concatenate([ar, ar, ac, ac, ar, ar, ac, ac], axis=-1)
    first = (jnp.arange(2 * HEAD_DIM) // ROPE_FREQS) % 2 == 0
    sin = jnp.sin(ang)
    return jnp.stack([jnp.cos(ang), jnp.where(first, -sin, 0.0), jnp.where(first, 0.0, sin)])


def kernel(x, c, ctx, c_ctx, w_ada, b_ada, norm1_g, norm2_g, w_in, conv_w, sgu_norm_g, sgu_w, sgu_b,
           attn_sink, mix_norm_g, w_out, peer_wq, peer_keys, peer_u, peer_v, final_g):
    bsz, length, d = x.shape
    n_ctx = ctx.shape[1]
    depth = w_ada.shape[0]
    nh = PEER_HEADS

    cc = jnp.zeros((SUBLANES, d), F32).at[:bsz].set(c).at[bsz].set(c_ctx)
    mod = _mod_call(cc, w_ada, b_ada)

    rope_l = _rope_tables(length)
    rope_c = jnp.zeros((3, n_ctx, 2 * HEAD_DIM), F32).at[0].set(1.0)
    fg = final_g.reshape(1, d)
    u_packed, vt_packed = _tables_call(peer_u, peer_v)

    xl, xc = x, ctx
    for i in range(depth):
        last = i == depth - 1
        mod_l = mod[i, :bsz].reshape(bsz, 6, d)
        mod_c = jnp.broadcast_to(mod[i, bsz].reshape(1, 6, d), (bsz, 6, d))
        n1g = norm1_g[i].reshape(1, d)
        n2g = norm2_g[i].reshape(1, d)
        w_in_b = w_in[i].astype(BF16)
        sgu_g = sgu_norm_g[i].reshape(1, D_SGU)
        sgu_wb = sgu_w[i].astype(BF16)
        sgu_bias = jnp.repeat(sgu_b[i].T, D_SGU // SGU_HEADS, axis=1)
        sink_b = jnp.broadcast_to(attn_sink[i][:, None], (N_HEADS, LANES))
        mix_g = mix_norm_g[i].reshape(1, d)
        w_out_b = w_out[i].astype(BF16)
        wq_t =peer_wq[i].reshape(d, nh, 2, PEER_DHALF).transpose(2, 1, 3, 0).reshape(2 * nh * PEER_DHALF, d)
        keys_p = jnp.einsum('hpid,hg->pihgd', peer_keys[i], jnp.eye(nh, dtype=F32))
        keys_p = keys_p.reshape(2, N_KEYS * nh, nh * PEER_DHALF)
        w_s = _fold_call(keys_p, wq_t.reshape(2, nh * PEER_DHALF, d))
        peer = functools.partial(_peer_call, w_s=w_s, u_b=u_packed, vt_b=vt_packed, layer=i)
        mixer_w = (conv_w[i], sgu_g, sgu_wb, sgu_bias, sink_b, mix_g, w_out_b, n2g)

        pc_c, ps_c, q_c, kv_c = _in_call(xc, mod_c, n1g, rope_c, w_in_b)
        pc_l, ps_l, q_l, kv_l = _in_call(xl, mod_l, n1g, rope_l, w_in_b)
        xl, h2_l = _mix_call(xl, pc_l, ps_l, q_l, kv_l, kv_c, mod_l, *mixer_w, local=True)
        if not last:
            xc, h2_c = _mix_call(xc, pc_c, ps_c, q_c, kv_c, kv_c, mod_c, *mixer_w, local=False)
            xc = peer(h2_c, xc, mod_c, fg, final_norm=False)
        xl = peer(h2_l, xl, mod_l, fg, final_norm=last)
    return xl
```

```python
import functools
import math

import jax
import jax.numpy as jnp
from jax import lax
from jax.experimental import pallas as pl
from jax.experimental.pallas import tpu as pltpu

F32 = jnp.float32
BF16 = jnp.bfloat16

EPS = 1e-6
GRID_W = 64
D_CONV = 256
D_SGU = 256
SGU_HEADS = 4
SGU_CHUNK = 128
N_HEADS = 8
N_KV_HEADS = 2
HEAD_DIM = 64
D_ATTN = N_HEADS * HEAD_DIM
BLOCK = 128
ROPE_THETA = 10000.0
ROPE_FREQS = HEAD_DIM // 4
CONV_END = 3 * D_CONV
SGU_END = CONV_END + 2 * D_SGU
Q_END = SGU_END + D_ATTN
K_END = Q_END + N_KV_HEADS * HEAD_DIM
N_KEYS = 128
PEER_HEADS = 8
PEER_TOPK = 16
PEER_DHALF = 128

LANES = 128
SUBLANES = 8
VMEM_LIMIT_BYTES = 56 * 1024 * 1024

ROW_TILE = 512
PEER_TOKENS = 512
PEER_EXPERTS = 1024

D_IN = K_END + N_KV_HEADS * HEAD_DIM

_CAND = [(k, l) for k in range(PEER_TOPK) for l in range(PEER_TOPK) if (k + 1) * (l + 1) <= PEER_TOPK]


def _cparams(n_axes):
    return pltpu.CompilerParams(dimension_semantics=("arbitrary",) * n_axes,
                                vmem_limit_bytes=VMEM_LIMIT_BYTES)


def _gelu_sigmoid_form(x):
    k0 = -2.0 * math.sqrt(2.0 / math.pi) * math.log2(math.e)
    k1 = 0.044715 * k0
    return x / (1.0 + jnp.exp2(x * (x * x * k1 + k0)))


def _dot(a, b):
    return jnp.dot(a, b, preferred_element_type=F32)


def _dot_nt(a, b):
    return lax.dot_general(a, b, (((1,), (1,)), ((), ())), preferred_element_type=F32)


def _pack_rows_in_kernel(w):
    return pltpu.bitcast(w, jnp.int32)


def _unpack_rows(x):
    return pltpu.bitcast(x, BF16)


def _rms(x):
    return x * lax.rsqrt(jnp.mean(x * x, axis=-1, keepdims=True) + EPS)


def _mod_kernel(c_ref, w_ref, b_ref, o_ref):
    c = c_ref[...]
    sc = c / (1.0 + jnp.exp(-c))
    w = w_ref[...]
    c_hi = sc.astype(BF16)
    c_lo = (sc - c_hi.astype(F32)).astype(BF16)
    w_hi = w.astype(BF16)
    w_lo = (w - w_hi.astype(F32)).astype(BF16)
    o_ref[...] = _dot(c_hi, w_hi) + _dot(c_lo, w_hi) + _dot(c_hi, w_lo) + b_ref[...]


def _mod_call(cc, w_ada, b_ada):
    depth, d, n = w_ada.shape
    tn = 1536
    return pl.pallas_call(
        _mod_kernel,
        grid=(depth, n // tn),
        in_specs=[pl.BlockSpec((SUBLANES, d), lambda i, j: (0, 0)),
                  pl.BlockSpec((None, d, tn), lambda i, j: (i, 0, j)),
                  pl.BlockSpec((None, 1, tn), lambda i, j: (i, 0, j))],
        out_specs=pl.BlockSpec((None, SUBLANES, tn), lambda i, j: (i, 0, j)),
        out_shape=jax.ShapeDtypeStruct((depth, SUBLANES, n), F32),
        compiler_params=_cparams(2),
        name="adaln_mod",
    )(cc, w_ada, b_ada.reshape(depth, 1, n))


def _rotary(x, rope):
    n = x.shape[1]
    cos, sin_first, sin_second = (jnp.concatenate([rope[i]] * (n // LANES), axis=1) for i in range(3))
    from_right = pltpu.roll(x, n - ROPE_FREQS, axis=1)
    from_left = pltpu.roll(x, ROPE_FREQS, axis=1)
    return x * cos + from_right * sin_first + from_left * sin_second


def _in_kernel(x_ref, mod_ref, g_ref, rope_ref, w_ref, pc_ref, ps_ref, q_ref, kv_ref):
    x = x_ref[...]
    h = _rms(x) * g_ref[...] * (1.0 + mod_ref[1:2, :]) + mod_ref[0:1, :]
    hb = h.astype(BF16)

    def proj(lo, hi):
        return _dot(hb, w_ref[:, lo:hi])

    pc_ref[...] = proj(0, CONV_END)
    ps_ref[...] = proj(CONV_END, SGU_END)
    rope = rope_ref[...]
    q_ref[...] = (_rotary(proj(SGU_END, Q_END), rope) * HEAD_DIM ** -0.5).astype(BF16)
    k = _rotary(proj(Q_END, K_END), rope)
    v = proj(K_END, D_IN)
    kv_ref[:, 0:128] = k.astype(BF16)
    kv_ref[:, 128:256] = pltpu.roll(k, HEAD_DIM, axis=1).astype(BF16)
    kv_ref[:, 256:384] = v.astype(BF16)
    kv_ref[:, 384:512] = pltpu.roll(v, HEAD_DIM, axis=1).astype(BF16)


def _in_call(x, mod, norm_g, rope, w_in):
    b, r, d = x.shape
    tm = min(ROW_TILE, r)
    row = lambda n: pl.BlockSpec((None, tm, n), lambda i, t: (i, t, 0))
    return pl.pallas_call(
        _in_kernel,
        grid=(b, r // tm),
        in_specs=[row(d),
                  pl.BlockSpec((None, 6, d), lambda i, t: (i, 0, 0)),
                  pl.BlockSpec((1, d), lambda i, t: (0, 0)),
                  pl.BlockSpec((3, tm, LANES), lambda i, t: (0, t, 0)),
                  pl.BlockSpec((d, D_IN), lambda i, t: (0, 0))],
        out_specs=[row(CONV_END), row(2 * D_SGU), row(D_ATTN), row(512)],
        out_shape=[jax.ShapeDtypeStruct((b, r, CONV_END), F32),
                   jax.ShapeDtypeStruct((b, r, 2 * D_SGU), F32),
                   jax.ShapeDtypeStruct((b, r, D_ATTN), BF16),
                   jax.ShapeDtypeStruct((b, r, 512), BF16)],
        compiler_params=_cparams(2),
        name="in_proj",
    )(x, mod, norm_g, rope, w_in)


def _fold_lanes(blocks, op):
    parts = [b[:, c:c + LANES] for b in blocks for c in range(0, b.shape[1], LANES)]
    out = parts[0]
    for p in parts[1:]:
        out = op(out, p)
    return out


def _mix_kernel(x_ref, pc_ref, pcp_ref, pcn_ref, ps_ref, q_ref, kv_ref, kvp_ref, kvn_ref, kvc_ref,
                mod_ref, convw_ref, sgug_ref, sguw_ref, sgub_ref, sink_ref, mixg_ref, wout_ref, n2g_ref,
                xo_ref, h2_ref, kvx_ref, attn_ref, sgu_ref, *, local):
    tq = x_ref.shape[0]
    nblk = tq // BLOCK
    t = pl.program_id(1)
    nt = pl.num_programs(1)

    pc = pc_ref[...]
    z = pc[:, D_CONV:2 * D_CONV] * pc[:, 2 * D_CONV:]
    z_before = pcp_ref[7:8, D_CONV:2 * D_CONV] * pcp_ref[7:8, 2 * D_CONV:]
    z_after = pcn_ref[0:1, D_CONV:2 * D_CONV] * pcn_ref[0:1, 2 * D_CONV:]
    z_before = z_before * (t > 0).astype(F32)
    z_after = z_after * (t < nt - 1).astype(F32)
    rows = lax.broadcasted_iota(jnp.int32, (tq, D_CONV), 0)
    z_prev = jnp.where(rows == 0, z_before, pltpu.roll(z, 1, axis=0))
    z_next = jnp.where(rows == tq - 1, z_after, pltpu.roll(z, tq - 1, axis=0))
    conv = pc[:, :D_CONV] * (z_prev * convw_ref[0:1, :] + z * convw_ref[1:2, :] + z_next * convw_ref[2:3, :])

    zg = _gelu_sigmoid_form(ps_ref[...])
    u = zg[:, :D_SGU]
    v = zg[:, D_SGU:]
    mu = jnp.mean(v, axis=-1, keepdims=True)
    vc = v - mu
    vn = vc * lax.rsqrt(jnp.mean(vc * vc, axis=-1, keepdims=True) + EPS) * sgug_ref[...]
    lane = lax.broadcasted_iota(jnp.int32, (BLOCK, LANES), 1)
    low = lane < HEAD_DIM
    for cb in range(nblk):
        pieces = []
        for a in range(SGU_HEADS // 2):
            vp = vn[cb * BLOCK:(cb + 1) * BLOCK, a * LANES:(a + 1) * LANES]
            v_lo = jnp.where(low, vp, 0.0).astype(BF16)
            v_hi = jnp.where(low, 0.0, vp).astype(BF16)
            pieces.append(_dot(sguw_ref[2 * a], v_lo) + _dot(sguw_ref[2 * a + 1], v_hi))
        s = jnp.concatenate(pieces, axis=1) + sgub_ref[...]
        sgu_ref[cb * BLOCK:(cb + 1) * BLOCK, :] = u[cb * BLOCK:(cb + 1) * BLOCK, :] * s

    if local:
        kvx_ref[0:BLOCK, :] = kvp_ref[...]
        kvx_ref[BLOCK:BLOCK + tq, :] = kv_ref[...]
        kvx_ref[BLOCK + tq:, :] = kvn_ref[...]
    kvc = kvc_ref[...]
    qi = lax.broadcasted_iota(jnp.int32, (2 * BLOCK, 3 * BLOCK), 0) & (BLOCK - 1)
    ko = lax.broadcasted_iota(jnp.int32, (2 * BLOCK, 3 * BLOCK), 1)
    band = (ko >= qi) & (ko <= qi + 2 * BLOCK)
    upper = lax.broadcasted_iota(jnp.int32, (2 * BLOCK, 1), 0) < BLOCK

    def attend(jb, carry):
        r0 = pl.multiple_of(jb * BLOCK, BLOCK)
        if local:
            n = t * nblk + jb
            first_key = jnp.where(n > 0, 0, BLOCK)
            end_key = jnp.where(n < nt * nblk - 1, 3 * BLOCK, 2 * BLOCK)
            ok = band & (ko >= first_key) & (ko < end_key)
            kvl = kvx_ref[pl.ds(r0, 3 * BLOCK), :]
        pairs = [q_ref[pl.ds(r0, BLOCK), a * LANES:(a + 1) * LANES] for a in range(N_HEADS // 2)]
        groups = []
        for kvh in range(N_KV_HEADS):
            for even in (True, False):
                members = [2 * kvh, 2 * kvh + 1]
                keep = low if even else jnp.logical_not(low)
                qg = jnp.concatenate([jnp.where(keep, pairs[a], jnp.zeros_like(pairs[a])) for a in members], axis=0)
                natural = (kvh == 0) == even
                ksel = slice(0, 128) if natural else slice(128, 256)
                vsel = slice(256, 384) if natural else slice(384, 512)
                s_list = [_dot_nt(qg, kvc[:, ksel])]
                v_list = [kvc[:, vsel]]
                if local:
                    s_list.append(jnp.where(ok, _dot_nt(qg, kvl[:, ksel]), -jnp.inf))
                    v_list.append(kvl[:, vsel])
                hds = [2 * a + (0 if even else 1) for a in members]
                sink = jnp.where(upper, sink_ref[hds[0]:hds[0] + 1, 0:1], sink_ref[hds[1]:hds[1] + 1, 0:1])
                groups.append((hds, s_list, v_list, sink))
        probs = []
        for _, s_list, _, sink in groups:
            m = jnp.maximum(sink, jnp.max(_fold_lanes(s_list, jnp.maximum), axis=-1, keepdims=True))
            ps = [jnp.exp(s - m) for s in s_list]
            denom = jnp.exp(sink - m) + jnp.sum(_fold_lanes(ps, jnp.add), axis=-1, keepdims=True)
            probs.append(([p.astype(BF16) for p in ps], 1.0 / denom))
        outs = [None] * N_HEADS
        for (ps, rden), (hds, _, v_list, _) in zip(probs, groups):
            o = _dot(ps[0], v_list[0])
            for p, vv in zip(ps[1:], v_list[1:]):
                o = o + _dot(p, vv)
            o = o * rden
            outs[hds[0]] = o[:BLOCK]
            outs[hds[1]] = o[BLOCK:]
        for a in range(N_HEADS // 2):
            attn_ref[pl.ds(r0, BLOCK), a * LANES:(a + 1) * LANES] = jnp.where(low, outs[2 * a], outs[2 * a + 1])
        return carry

    lax.fori_loop(0, nblk, attend, 0)

    g = mixg_ref[...]
    yc = (_rms(conv) * g[:, :D_CONV]).astype(BF16)
    ys = (_rms(sgu_ref[...]) * g[:, D_CONV:D_CONV + D_SGU]).astype(BF16)
    ya = (_rms(attn_ref[...]) * g[:, D_CONV + D_SGU:]).astype(BF16)
    yl = (_dot(yc, wout_ref[0:D_CONV, :]) + _dot(ys, wout_ref[D_CONV:D_CONV + D_SGU, :])
          + _dot(ya, wout_ref[D_CONV + D_SGU:, :]))
    xn = x_ref[...] + mod_ref[2:3, :] * yl
    xo_ref[...] = xn
    h2 = _rms(xn) * n2g_ref[...] * (1.0 + mod_ref[4:5, :]) + mod_ref[3:4, :]
    h2_ref[...] = h2.astype(BF16)


def _mix_call(x, pc, ps, q, kv, kvc, mod, conv_w, sgu_g, sgu_w, sgu_b, sink_b, mix_g, w_out, n2g, *, local):
    b, r, d = x.shape
    c = kvc.shape[1]
    tq = min(ROW_TILE, r)
    nt = r // tq
    hb = tq // SUBLANES
    kb = tq // BLOCK
    row = lambda n: pl.BlockSpec((None, tq, n), lambda i, t: (i, t, 0))
    full = lambda shape: pl.BlockSpec(shape, lambda i, t: (0,) * len(shape))
    return pl.pallas_call(
        functools.partial(_mix_kernel, local=local),
        grid=(b, nt),
        in_specs=[row(d), row(CONV_END),
                  pl.BlockSpec((None, SUBLANES, CONV_END), lambda i, t: (i, jnp.maximum(t * hb - 1, 0), 0)),
                  pl.BlockSpec((None, SUBLANES, CONV_END), lambda i, t: (i, jnp.minimum((t + 1) * hb, nt * hb - 1), 0)),
                  row(2 * D_SGU), row(D_ATTN), row(512),
                  pl.BlockSpec((None, BLOCK, 512), lambda i, t: (i, jnp.maximum(t * kb - 1, 0), 0)),
                  pl.BlockSpec((None, BLOCK, 512), lambda i, t: (i, jnp.minimum((t + 1) * kb, nt * kb - 1), 0)),
                  pl.BlockSpec((None, c, 512), lambda i, t: (i, 0, 0)),
                  pl.BlockSpec((None, 6, d), lambda i, t: (i, 0, 0)),
                  full((3, D_CONV)), full((1, D_SGU)), full((SGU_HEADS, SGU_CHUNK, SGU_CHUNK)),
                  full((SGU_CHUNK, D_SGU)), full((N_HEADS, LANES)), full((1, d)), full((d, d)), full((1, d))],
        out_specs=[row(d), row(d)],
        out_shape=[jax.ShapeDtypeStruct((b, r, d), F32), jax.ShapeDtypeStruct((b, r, d), BF16)],
        scratch_shapes=[pltpu.VMEM((tq + 2 * BLOCK, 512), BF16),
                        pltpu.VMEM((tq, D_ATTN), F32),
                        pltpu.VMEM((tq, D_SGU), F32)],
        compiler_params=_cparams(2),
        name="mixers_local" if local else "mixers_ctx",
    )(x, pc, pc, pc, ps, q, kv, kv, kv, kvc, mod, conv_w, sgu_g, sgu_w, sgu_b, sink_b, mix_g, w_out, n2g)


def _oddeven_merge(lo, hi, r):
    step = r * 2
    if step < hi - lo:
        yield from _oddeven_merge(lo, hi, step)
        yield from _oddeven_merge(lo + r, hi, step)
        yield from [(i, i + r) for i in range(lo + r, hi - r, step)]
    else:
        yield (lo, lo + r)


def _oddeven_sort(lo, hi):
    if hi - lo >= 1:
        mid = lo + (hi - lo) // 2
        yield from _oddeven_sort(lo, mid)
        yield from _oddeven_sort(mid + 1, hi)
        yield from _oddeven_merge(lo, hi, 1)


_SORT16 = tuple(_oddeven_sort(0, PEER_TOPK - 1))


def _sort16(x):
    x = list(x)
    for i, j in _SORT16:
        x[i], x[j] = jnp.maximum(x[i], x[j]), jnp.minimum(x[i], x[j])
    return x


def _merge_top16(a, b):
    n = PEER_TOPK
    c = [jnp.maximum(a[i], b[n - 1 - i]) for i in range(n)]
    d = n // 2
    while d:
        for i in range(n):
            if not i & d:
                c[i], c[i + d] = jnp.maximum(c[i], c[i + d]), jnp.minimum(c[i], c[i + d])
        d //= 2
    return c


def _top16(load):
    def tree(lo, n):
        if n == PEER_TOPK:
            return _sort16([load(lo + i) for i in range(n)])
        return _merge_top16(tree(lo, n // 2), tree(lo + n // 2, n // 2))
    return tree(0, N_KEYS)


def _peer_select(h_ref, ws_ref, u0_ref, ht_ref, e0_ref, e1_ref, th_ref, s0_ref, s1_ref, a0_ref):
    tm = h_ref.shape[0]
    nh = PEER_HEADS
    ht_ref[...] = h_ref[...].astype(F32).T.astype(BF16)
    s0 = _dot(_unpack_rows(ws_ref[0]), ht_ref[...])
    s1 = _dot(_unpack_rows(ws_ref[1]), ht_ref[...])
    for c in range(tm // LANES):
        s0_ref[c] = s0[:, c * LANES:(c + 1) * LANES]
        s1_ref[c] = s1[:, c * LANES:(c + 1) * LANES]

    rnd = lambda v: v.astype(BF16).astype(F32)

    def select(c, carry):
        lanes = pl.ds(pl.multiple_of(c * LANES, LANES), LANES)
        a0_ref[:, lanes] = _dot(_unpack_rows(u0_ref[...]), ht_ref[:, lanes]).astype(BF16)
        a = _top16(lambda i: s0_ref[c, i * nh:(i + 1) * nh, :])
        b = _top16(lambda j: s1_ref[c, j * nh:(j + 1) * nh, :])
        ea = [jnp.exp(v - a[0]) for v in a]
        eb = [jnp.exp(v - b[0]) for v in b]
        cand = [ea[k] * eb[l] for k, l in _CAND]
        rest = cand[PEER_TOPK:]
        rest = rest + [jnp.full_like(cand[0], -1.0)] * (-len(rest) % PEER_TOPK)
        best = cand[:PEER_TOPK]
        for g in range(0, len(rest), PEER_TOPK):
            best = _merge_top16(best, _sort16(rest[g:g + PEER_TOPK]))
        top = best[PEER_TOPK - 1]
        zsum = jnp.zeros_like(top)
        for p in cand:
            zsum = zsum + jnp.where(p >= top, p, 0.0)
        rz = 1.0 / zsum
        ean = [rnd(v * rz) for v in ea]
        ebn = [rnd(v) for v in eb]
        thn = jnp.full_like(top, jnp.inf)
        for (k, l), p in zip(_CAND, cand):
            thn = jnp.minimum(thn, jnp.where(p >= top, rnd(ean[k] * ebn[l]), jnp.inf))
        for hh in range(nh):
            th_ref[hh, :, lanes] = jnp.broadcast_to(thn[hh:hh + 1, :], (2 * SUBLANES, LANES)).astype(BF16)
        for i in range(N_KEYS):
            e0_ref[i, :, lanes] = jnp.exp(s0_ref[c, i * nh:(i + 1) * nh, :] - a[0]) * rz
        for hh in range(nh):
            s1_head = s1_ref[c, pl.ds(hh, N_KEYS, stride=nh), :]
            e1_ref[hh, :, lanes] = jnp.exp(s1_head - b[0][hh:hh + 1, :]).astype(BF16)
        return carry

    lax.fori_loop(0, tm // LANES, select, 0)


def _peer_gate(a_ref, row0, e0_ref, e1_ref, th_ref, hbuf_ref, lanes):
    n_lanes = lanes.stop - lanes.start
    pack = 2 * SUBLANES
    for ii in range(a_ref.shape[0] // N_KEYS):
        e0 = e0_ref[row0 + ii, :, lanes]
        e0r = [jnp.broadcast_to(e0[hh:hh + 1, :], (pack, n_lanes)).astype(BF16) for hh in range(PEER_HEADS)]
        for c in range(N_KEYS // pack):
            gate = None
            for hh in range(PEER_HEADS):
                p = e0r[hh] * e1_ref[hh, c * pack:(c + 1) * pack, lanes]
                sel = jnp.where(p >= th_ref[hh, :, lanes], p, jnp.zeros_like(p))
                gate = sel if gate is None else gate + sel
            r0 = ii * N_KEYS + c * pack
            act = _gelu_sigmoid_form(a_ref[r0:r0 + pack, lanes])
            hbuf_ref[r0:r0 + pack, lanes] = act * gate


def _peer_lane_split(tm):
    return 2 if tm % (2 * 2 * LANES) == 0 else 1


def _peer_kernel(h_ref, x_ref, mod_ref, fg_ref, ws_ref, u0_ref, uodd_ref, uevn_ref, vt_ref, o_ref,
                 ht_ref, e0_ref, e1_ref, th_ref, *scratch, final_norm):
    e = pl.program_id(2)
    ne = pl.num_programs(2)
    tm = h_ref.shape[0]
    n_split = _peer_lane_split(tm)
    s0_ref, s1_ref, aevn_ref, aodd_ref, hevn_ref, hodd_ref, acc_ref = scratch
    eb = 2 * u0_ref.shape[0]
    keys_per_block = eb // N_KEYS

    lane_ranges = [slice(s * (tm // n_split), (s + 1) * (tm // n_split)) for s in range(n_split)]
    stages = ((uodd_ref, aodd_ref, aevn_ref, hevn_ref), (uevn_ref, aevn_ref, aodd_ref, hodd_ref))
    chains = [(s, lanes) for s in range(len(stages)) for lanes in lane_ranges]

    @pl.when(e == 0)
    def _():
        _peer_select(h_ref, ws_ref, u0_ref, ht_ref, e0_ref, e1_ref, th_ref, s0_ref, s1_ref, aevn_ref)
        acc_ref[...] = jnp.zeros_like(acc_ref)

    def next_scores(s, lanes):
        u_next, a_next, _, _ = stages[s]
        a_next[:, lanes] = _dot(_unpack_rows(u_next[...]), ht_ref[:, lanes]).astype(BF16)

    def gates(s, lanes):
        _, _, a_cur, hbuf_ref = stages[s]
        _peer_gate(a_cur, (2 * e + s) * keys_per_block, e0_ref, e1_ref, th_ref, hbuf_ref, lanes)

    def values(s, lanes):
        hbuf_ref = stages[s][3]
        vt = _unpack_rows(vt_ref[:, s * eb:(s + 1) * eb])
        acc_ref[:, lanes] += _dot(vt, hbuf_ref[:, lanes])

    next_scores(*chains[0])
    for c, chain in enumerate(chains):
        if c + 1 < len(chains):
            next_scores(*chains[c + 1])
        gates(*chain)
        values(*chain)

    @pl.when(e == ne - 1)
    def _():
        y = x_ref[...] + mod_ref[5:6, :] * acc_ref[...].T
        if final_norm:
            y = _rms(y) * fg_ref[...]
        o_ref[...] = y


def _tables_kernel(u_ref, v_ref, up_ref, vtp_ref):
    up_ref[...] = _pack_rows_in_kernel(u_ref[...].astype(BF16))
    vtp_ref[...] = _pack_rows_in_kernel(v_ref[...].T.astype(BF16))


def _tables_call(peer_u, peer_v):
    depth, n_exp, d = peer_u.shape
    eb = PEER_EXPERTS
    return pl.pallas_call(
        _tables_kernel,
        grid=(depth, n_exp // eb),
        in_specs=[pl.BlockSpec((None, eb, d), lambda i, e: (i, e, 0)),
                  pl.BlockSpec((None, eb, d), lambda i, e: (i, e, 0))],
        out_specs=[pl.BlockSpec((None, eb // 2, d), lambda i, e: (i, e, 0)),
                   pl.BlockSpec((None, d // 2, eb), lambda i, e: (i, 0, e))],
        out_shape=[jax.ShapeDtypeStruct((depth, n_exp // 2, d), jnp.int32),
                   jax.ShapeDtypeStruct((depth, d // 2, n_exp), jnp.int32)],
        compiler_params=_cparams(2),
        name="pack_tables",
    )(peer_u, peer_v)


def _fold_kernel(k_ref, w_ref, o_ref):
    k = k_ref[...]
    w = w_ref[...]
    k_hi = k.astype(BF16)
    k_lo = (k - k_hi.astype(F32)).astype(BF16)
    w_hi = w.astype(BF16)
    w_lo = (w - w_hi.astype(F32)).astype(BF16)
    o_ref[...] = _pack_rows_in_kernel((_dot(k_hi, w_hi) + _dot(k_lo, w_hi) + _dot(k_hi, w_lo)).astype(BF16))


def _fold_call(keys_p, wq_t):
    _, m, k = keys_p.shape
    n = wq_t.shape[2]
    tn = 256
    return pl.pallas_call(
        _fold_kernel,
        grid=(2, n // tn),
        in_specs=[pl.BlockSpec((None, m, k), lambda p, j: (p, 0, 0)),
                  pl.BlockSpec((None, k, tn), lambda p, j: (p, 0, j))],
        out_specs=pl.BlockSpec((None, m // 2, tn), lambda p, j: (p, 0, j)),
        out_shape=jax.ShapeDtypeStruct((2, m // 2, n), jnp.int32),
        compiler_params=_cparams(2),
        name="fold_keys",
    )(keys_p, wq_t)


def _peer_call(h2, x, mod, final_g, w_s, u_b, vt_b, *, layer, final_norm):
    b, r, d = x.shape
    tm = min(PEER_TOKENS, r)
    eb = PEER_EXPERTS
    n_blocks = 2 * u_b.shape[1] // eb
    nh = PEER_HEADS
    score_chunks = pltpu.VMEM((tm // LANES, N_KEYS * nh, LANES), F32)
    row = lambda: pl.BlockSpec((None, tm, d), lambda i, t, e: (i, t, 0))
    full = lambda shape: pl.BlockSpec(shape, lambda i, t, e: (0,) * len(shape))
    return pl.pallas_call(
        functools.partial(_peer_kernel, final_norm=final_norm),
        grid=(b, r // tm, n_blocks // 2),
        in_specs=[row(), row(),
                  pl.BlockSpec((None, 6, d), lambda i, t, e: (i, 0, 0)),
                  full((1, d)), full(w_s.shape),
                  pl.BlockSpec((None, eb // 2, d), lambda i, t, e: (layer, 0, 0)),
                  pl.BlockSpec((None, eb // 2, d), lambda i, t, e: (layer, 2 * e + 1, 0)),
                  pl.BlockSpec((None, eb // 2, d),
                               lambda i, t, e: (layer, jnp.minimum(2 * e + 2, n_blocks - 2), 0)),
                  pl.BlockSpec((None, d // 2, 2 * eb), lambda i, t, e: (layer, 0, e))],
        out_specs=row(),
        out_shape=jax.ShapeDtypeStruct((b, r, d), F32),
        scratch_shapes=[pltpu.VMEM((d, tm), BF16),
                        pltpu.VMEM((N_KEYS, nh, tm), F32),
                        pltpu.VMEM((nh, N_KEYS, tm), BF16),
                        pltpu.VMEM((nh, 2 * SUBLANES, tm), BF16),
                        score_chunks,
                        score_chunks,
                        pltpu.VMEM((eb, tm), BF16),
                        pltpu.VMEM((eb, tm), BF16),
                        pltpu.VMEM((eb, tm), BF16),
                        pltpu.VMEM((eb, tm), BF16),
                        pltpu.VMEM((d, tm), F32)],
        compiler_params=_cparams(3),
        name="peer_final" if final_norm else "peer",
    )(h2, x, mod, final_g, w_s, u_b, u_b, u_b, vt_b)


def _rope_tables(length):
    rows = length // GRID_W
    row = jnp.repeat(jnp.arange(rows), GRID_W).astype(F32)
    col = jnp.tile(jnp.arange(GRID_W), rows).astype(F32)
    inv = ROPE_THETA ** (-jnp.arange(ROPE_FREQS, dtype=F32) / ROPE_FREQS)
    ar = row[:, None] * inv[None, :]
    ac = col[:, None] * inv[None, :]
    ang = jnp.concatenate([ar, ar, ac, ac, ar, ar, ac, ac], axis=-1)
    first = (jnp.arange(2 * HEAD_DIM) // ROPE_FREQS) % 2 == 0
    sin = jnp.sin(ang)
    return jnp.stack([jnp.cos(ang), jnp.where(first, -sin, 0.0), jnp.where(first, 0.0, sin)])


def kernel(x, c, ctx, c_ctx, w_ada, b_ada, norm1_g, norm2_g, w_in, conv_w, sgu_norm_g, sgu_w, sgu_b,
           attn_sink, mix_norm_g, w_out, peer_wq, peer_keys, peer_u, peer_v, final_g):
    bsz, length, d = x.shape
    n_ctx = ctx.shape[1]
    depth = w_ada.shape[0]
    nh = PEER_HEADS

    cc = jnp.zeros((SUBLANES, d), F32).at[:bsz].set(c).at[bsz].set(c_ctx)
    mod = _mod_call(cc, w_ada, b_ada)

    rope_l = _rope_tables(length)
    rope_c = jnp.zeros((3, n_ctx, 2 * HEAD_DIM), F32).at[0].set(1.0)
    fg = final_g.reshape(1, d)
    u_packed, vt_packed = _tables_call(peer_u, peer_v)

    xl, xc = x, ctx
    for i in range(depth):
        last = i == depth - 1
        mod_l = mod[i, :bsz].reshape(bsz, 6, d)
        mod_c = jnp.broadcast_to(mod[i, bsz].reshape(1, 6, d), (bsz, 6, d))
        n1g = norm1_g[i].reshape(1, d)
        n2g = norm2_g[i].reshape(1, d)
        w_in_b = w_in[i].astype(BF16)
        sgu_g = sgu_norm_g[i].reshape(1, D_SGU)
        sgu_wb = sgu_w[i].astype(BF16)
        sgu_bias = jnp.repeat(sgu_b[i].T, D_SGU // SGU_HEADS, axis=1)
        sink_b = jnp.broadcast_to(attn_sink[i][:, None], (N_HEADS, LANES))
        mix_g = mix_norm_g[i].reshape(1, d)
        w_out_b = w_out[i].astype(BF16)
        wq_t =peer_wq[i].reshape(d, nh, 2, PEER_DHALF).transpose(2, 1, 3, 0).reshape(2 * nh * PEER_DHALF, d)
        keys_p = jnp.einsum('hpid,hg->pihgd', peer_keys[i], jnp.eye(nh, dtype=F32))
        keys_p = keys_p.reshape(2, N_KEYS * nh, nh * PEER_DHALF)
        w_s = _fold_call(keys_p, wq_t.reshape(2, nh * PEER_DHALF, d))
        peer = functools.partial(_peer_call, w_s=w_s, u_b=u_packed, vt_b=vt_packed, layer=i)
        mixer_w = (conv_w[i], sgu_g, sgu_wb, sgu_bias, sink_b, mix_g, w_out_b, n2g)

        pc_c, ps_c, q_c, kv_c = _in_call(xc, mod_c, n1g, rope_c, w_in_b)
        pc_l, ps_l, q_l, kv_l = _in_call(xl, mod_l, n1g, rope_l, w_in_b)
        xl, h2_l = _mix_call(xl, pc_l, ps_l, q_l, kv_l, kv_c, mod_l, *mixer_w, local=True)
        if not last:
            xc, h2_c = _mix_call(xc, pc_c, ps_c, q_c, kv_c, kv_c, mod_c, *mixer_w, local=False)
            xc = peer(h2_c, xc, mod_c, fg, final_norm=False)
        xl = peer(h2_l, xl, mod_l, fg, final_norm=last)
    return xl
```

```python
import functools
import math

import jax
import jax.numpy as jnp
from jax import lax
from jax.experimental import pallas as pl
from jax.experimental.pallas import tpu as pltpu

F32 = jnp.float32
BF16 = jnp.bfloat16

EPS = 1e-6
GRID_W = 64
D_CONV = 256
D_SGU = 256
SGU_HEADS = 4
SGU_CHUNK = 128
N_HEADS = 8
N_KV_HEADS = 2
HEAD_DIM = 64
D_ATTN = N_HEADS * HEAD_DIM
BLOCK = 128
ROPE_THETA = 10000.0
ROPE_FREQS = HEAD_DIM // 4
CONV_END = 3 * D_CONV
SGU_END = CONV_END + 2 * D_SGU
Q_END = SGU_END + D_ATTN
K_END = Q_END + N_KV_HEADS * HEAD_DIM
N_KEYS = 128
PEER_HEADS = 8
PEER_TOPK = 16
PEER_DHALF = 128

LANES = 128
SUBLANES = 8
VMEM_LIMIT_BYTES = 56 * 1024 * 1024

ROW_TILE = 512
PEER_TOKENS = 512
PEER_EXPERTS = 1024

D_IN = K_END + N_KV_HEADS * HEAD_DIM

_CAND = [(k, l) for k in range(PEER_TOPK) for l in range(PEER_TOPK) if (k + 1) * (l + 1) <= PEER_TOPK]


def _cparams(n_axes):
    return pltpu.CompilerParams(dimension_semantics=("arbitrary",) * n_axes,
                                vmem_limit_bytes=VMEM_LIMIT_BYTES)


def _gelu_sigmoid_form(x):
    k0 = -2.0 * math.sqrt(2.0 / math.pi) * math.log2(math.e)
    k1 = 0.044715 * k0
    return x / (1.0 + jnp.exp2(x * (x * x * k1 + k0)))


def _dot(a, b):
    return jnp.dot(a, b, preferred_element_type=F32)


def _dot_nt(a, b):
    return lax.dot_general(a, b, (((1,), (1,)), ((), ())), preferred_element_type=F32)


def _pack_rows_in_kernel(w):
    return pltpu.bitcast(w, jnp.int32)


def _unpack_rows(x):
    return pltpu.bitcast(x, BF16)


def _rms(x):
    return x * lax.rsqrt(jnp.mean(x * x, axis=-1, keepdims=True) + EPS)


def _mod_kernel(c_ref, w_ref, b_ref, o_ref):
    c = c_ref[...]
    sc = c / (1.0 + jnp.exp(-c))
    w = w_ref[...]
    c_hi = sc.astype(BF16)
    c_lo = (sc - c_hi.astype(F32)).astype(BF16)
    w_hi = w.astype(BF16)
    w_lo = (w - w_hi.astype(F32)).astype(BF16)
    o_ref[...] = _dot(c_hi, w_hi) + _dot(c_lo, w_hi) + _dot(c_hi, w_lo) + b_ref[...]


def _mod_call(cc, w_ada, b_ada):
    depth, d, n = w_ada.shape
    tn = 1536
    return pl.pallas_call(
        _mod_kernel,
        grid=(depth, n // tn),
        in_specs=[pl.BlockSpec((SUBLANES, d), lambda i, j: (0, 0)),
                  pl.BlockSpec((None, d, tn), lambda i, j: (i, 0, j)),
                  pl.BlockSpec((None, 1, tn), lambda i, j: (i, 0, j))],
        out_specs=pl.BlockSpec((None, SUBLANES, tn), lambda i, j: (i, 0, j)),
        out_shape=jax.ShapeDtypeStruct((depth, SUBLANES, n), F32),
        compiler_params=_cparams(2),
        name="adaln_mod",
    )(cc, w_ada, b_ada.reshape(depth, 1, n))


def _rotary(x, rope):
    n = x.shape[1]
    cos, sin_first, sin_second = (jnp.concatenate([rope[i]] * (n // LANES), axis=1) for i in range(3))
    from_right = pltpu.roll(x, n - ROPE_FREQS, axis=1)
    from_left = pltpu.roll(x, ROPE_FREQS, axis=1)
    return x * cos + from_right * sin_first + from_left * sin_second


def _in_kernel(x_ref, mod_ref, g_ref, rope_ref, w_ref, pc_ref, ps_ref, q_ref, kv_ref):
    x = x_ref[...]
    h = _rms(x) * g_ref[...] * (1.0 + mod_ref[1:2, :]) + mod_ref[0:1, :]
    hb = h.astype(BF16)

    def proj(lo, hi):
        return _dot(hb, w_ref[:, lo:hi])

    pc_ref[...] = proj(0, CONV_END)
    ps_ref[...] = proj(CONV_END, SGU_END)
    rope = rope_ref[...]
    q_ref[...] = (_rotary(proj(SGU_END, Q_END), rope) * HEAD_DIM ** -0.5).astype(BF16)
    k = _rotary(proj(Q_END, K_END), rope)
    v = proj(K_END, D_IN)
    kv_ref[:, 0:128] = k.astype(BF16)
    kv_ref[:, 128:256] = pltpu.roll(k, HEAD_DIM, axis=1).astype(BF16)
    kv_ref[:, 256:384] = v.astype(BF16)
    kv_ref[:, 384:512] = pltpu.roll(v, HEAD_DIM, axis=1).astype(BF16)


def _in_call(x, mod, norm_g, rope, w_in):
    b, r, d = x.shape
    tm = min(ROW_TILE, r)
    row = lambda n: pl.BlockSpec((None, tm, n), lambda i, t: (i, t, 0))
    return pl.pallas_call(
        _in_kernel,
        grid=(b, r // tm),
        in_specs=[row(d),
                  pl.BlockSpec((None, 6, d), lambda i, t: (i, 0, 0)),
                  pl.BlockSpec((1, d), lambda i, t: (0, 0)),
                  pl.BlockSpec((3, tm, LANES), lambda i, t: (0, t, 0)),
                  pl.BlockSpec((d, D_IN), lambda i, t: (0, 0))],
        out_specs=[row(CONV_END), row(2 * D_SGU), row(D_ATTN), row(512)],
        out_shape=[jax.ShapeDtypeStruct((b, r, CONV_END), F32),
                   jax.ShapeDtypeStruct((b, r, 2 * D_SGU), F32),
                   jax.ShapeDtypeStruct((b, r, D_ATTN), BF16),
                   jax.ShapeDtypeStruct((b, r, 512), BF16)],
        compiler_params=_cparams(2),
        name="in_proj",
    )(x, mod, norm_g, rope, w_in)


def _fold_lanes(blocks, op):
    parts = [b[:, c:c + LANES] for b in blocks for c in range(0, b.shape[1], LANES)]
    out = parts[0]
    for p in parts[1:]:
        out = op(out, p)
    return out


def _mix_kernel(x_ref, pc_ref, pcp_ref, pcn_ref, ps_ref, q_ref, kv_ref, kvp_ref, kvn_ref, kvc_ref,
                mod_ref, convw_ref, sgug_ref, sguw_ref, sgub_ref, sink_ref, mixg_ref, wout_ref, n2g_ref,
                xo_ref, h2_ref, kvx_ref, attn_ref, sgu_ref, *, local):
    tq = x_ref.shape[0]
    nblk = tq // BLOCK
    t = pl.program_id(1)
    nt = pl.num_programs(1)

    pc = pc_ref[...]
    z = pc[:, D_CONV:2 * D_CONV] * pc[:, 2 * D_CONV:]
    z_before = pcp_ref[7:8, D_CONV:2 * D_CONV] * pcp_ref[7:8, 2 * D_CONV:]
    z_after = pcn_ref[0:1, D_CONV:2 * D_CONV] * pcn_ref[0:1, 2 * D_CONV:]
    z_before = z_before * (t > 0).astype(F32)
    z_after = z_after * (t < nt - 1).astype(F32)
    rows = lax.broadcasted_iota(jnp.int32, (tq, D_CONV), 0)
    z_prev = jnp.where(rows == 0, z_before, pltpu.roll(z, 1, axis=0))
    z_next = jnp.where(rows == tq - 1, z_after, pltpu.roll(z, tq - 1, axis=0))
    conv = pc[:, :D_CONV] * (z_prev * convw_ref[0:1, :] + z * convw_ref[1:2, :] + z_next * convw_ref[2:3, :])

    zg = _gelu_sigmoid_form(ps_ref[...])
    u = zg[:, :D_SGU]
    v = zg[:, D_SGU:]
    mu = jnp.mean(v, axis=-1, keepdims=True)
    vc = v - mu
    vn = vc * lax.rsqrt(jnp.mean(vc * vc, axis=-1, keepdims=True) + EPS) * sgug_ref[...]
    lane = lax.broadcasted_iota(jnp.int32, (BLOCK, LANES), 1)
    low = lane < HEAD_DIM
    for cb in range(nblk):
        pieces = []
        for a in range(SGU_HEADS // 2):
            vp = vn[cb * BLOCK:(cb + 1) * BLOCK, a * LANES:(a + 1) * LANES]
            v_lo = jnp.where(low, vp, 0.0).astype(BF16)
            v_hi = jnp.where(low, 0.0, vp).astype(BF16)
            pieces.append(_dot(sguw_ref[2 * a], v_lo) + _dot(sguw_ref[2 * a + 1], v_hi))
        s = jnp.concatenate(pieces, axis=1) + sgub_ref[...]
        sgu_ref[cb * BLOCK:(cb + 1) * BLOCK, :] = u[cb * BLOCK:(cb + 1) * BLOCK, :] * s

    if local:
        kvx_ref[0:BLOCK, :] = kvp_ref[...]
        kvx_ref[BLOCK:BLOCK + tq, :] = kv_ref[...]
        kvx_ref[BLOCK + tq:, :] = kvn_ref[...]
    kvc = kvc_ref[...]
    qi = lax.broadcasted_iota(jnp.int32, (2 * BLOCK, 3 * BLOCK), 0) & (BLOCK - 1)
    ko = lax.broadcasted_iota(jnp.int32, (2 * BLOCK, 3 * BLOCK), 1)
    band = (ko >= qi) & (ko <= qi + 2 * BLOCK)
    upper = lax.broadcasted_iota(jnp.int32, (2 * BLOCK, 1), 0) < BLOCK

    def attend(jb, carry):
        r0 = pl.multiple_of(jb * BLOCK, BLOCK)
        if local:
            n = t * nblk + jb
            first_key = jnp.where(n > 0, 0, BLOCK)
            end_key = jnp.where(n < nt * nblk - 1, 3 * BLOCK, 2 * BLOCK)
            ok = band & (ko >= first_key) & (ko < end_key)
            kvl = kvx_ref[pl.ds(r0, 3 * BLOCK), :]
        pairs = [q_ref[pl.ds(r0, BLOCK), a * LANES:(a + 1) * LANES] for a in range(N_HEADS // 2)]
        groups = []
        for kvh in range(N_KV_HEADS):
            for even in (True, False):
                members = [2 * kvh, 2 * kvh + 1]
                keep = low if even else jnp.logical_not(low)
                qg = jnp.concatenate([jnp.where(keep, pairs[a], jnp.zeros_like(pairs[a])) for a in members], axis=0)
                natural = (kvh == 0) == even
                ksel = slice(0, 128) if natural else slice(128, 256)
                vsel = slice(256, 384) if natural else slice(384, 512)
                s_list = [_dot_nt(qg, kvc[:, ksel])]
                v_list = [kvc[:, vsel]]
                if local:
                    s_list.append(jnp.where(ok, _dot_nt(qg, kvl[:, ksel]), -jnp.inf))
                    v_list.append(kvl[:, vsel])
                hds = [2 * a + (0 if even else 1) for a in members]
                sink = jnp.where(upper, sink_ref[hds[0]:hds[0] + 1, 0:1], sink_ref[hds[1]:hds[1] + 1, 0:1])
                groups.append((hds, s_list, v_list, sink))
        probs = []
        for _, s_list, _, sink in groups:
            m = jnp.maximum(sink, jnp.max(_fold_lanes(s_list, jnp.maximum), axis=-1, keepdims=True))
            ps = [jnp.exp(s - m) for s in s_list]
            denom = jnp.exp(sink - m) + jnp.sum(_fold_lanes(ps, jnp.add), axis=-1, keepdims=True)
            probs.append(([p.astype(BF16) for p in ps], 1.0 / denom))
        outs = [None] * N_HEADS
        for (ps, rden), (hds, _, v_list, _) in zip(probs, groups):
            o = _dot(ps[0], v_list[0])
            for p, vv in zip(ps[1:], v_list[1:]):
                o = o + _dot(p, vv)
            o = o * rden
            outs[hds[0]] = o[:BLOCK]
            outs[hds[1]] = o[BLOCK:]
        for a in range(N_HEADS // 2):
            attn_ref[pl.ds(r0, BLOCK), a * LANES:(a + 1) * LANES] = jnp.where(low, outs[2 * a], outs[2 * a + 1])
        return carry

    lax.fori_loop(0, nblk, attend, 0)

    g = mixg_ref[...]
    yc = (_rms(conv) * g[:, :D_CONV]).astype(BF16)
    ys = (_rms(sgu_ref[...]) * g[:, D_CONV:D_CONV + D_SGU]).astype(BF16)
    ya = (_rms(attn_ref[...]) * g[:, D_CONV + D_SGU:]).astype(BF16)
    yl = (_dot(yc, wout_ref[0:D_CONV, :]) + _dot(ys, wout_ref[D_CONV:D_CONV + D_SGU, :])
          + _dot(ya, wout_ref[D_CONV + D_SGU:, :]))
    xn = x_ref[...] + mod_ref[2:3, :] * yl
    xo_ref[...] = xn
    h2 = _rms(xn) * n2g_ref[...] * (1.0 + mod_ref[4:5, :]) + mod_ref[3:4, :]
    h2_ref[...] = h2.astype(BF16)


def _mix_call(x, pc, ps, q, kv, kvc, mod, conv_w, sgu_g, sgu_w, sgu_b, sink_b, mix_g, w_out, n2g, *, local):
    b, r, d = x.shape
    c = kvc.shape[1]
    tq = min(ROW_TILE, r)
    nt = r // tq
    hb = tq // SUBLANES
    kb = tq // BLOCK
    row = lambda n: pl.BlockSpec((None, tq, n), lambda i, t: (i, t, 0))
    full = lambda shape: pl.BlockSpec(shape, lambda i, t: (0,) * len(shape))
    return pl.pallas_call(
        functools.partial(_mix_kernel, local=local),
        grid=(b, nt),
        in_specs=[row(d), row(CONV_END),
                  pl.BlockSpec((None, SUBLANES, CONV_END), lambda i, t: (i, jnp.maximum(t * hb - 1, 0), 0)),
                  pl.BlockSpec((None, SUBLANES, CONV_END), lambda i, t: (i, jnp.minimum((t + 1) * hb, nt * hb - 1), 0)),
                  row(2 * D_SGU), row(D_ATTN), row(512),
                  pl.BlockSpec((None, BLOCK, 512), lambda i, t: (i, jnp.maximum(t * kb - 1, 0), 0)),
                  pl.BlockSpec((None, BLOCK, 512), lambda i, t: (i, jnp.minimum((t + 1) * kb, nt * kb - 1), 0)),
                  pl.BlockSpec((None, c, 512), lambda i, t: (i, 0, 0)),
                  pl.BlockSpec((None, 6, d), lambda i, t: (i, 0, 0)),
                  full((3, D_CONV)), full((1, D_SGU)), full((SGU_HEADS, SGU_CHUNK, SGU_CHUNK)),
                  full((SGU_CHUNK, D_SGU)), full((N_HEADS, LANES)), full((1, d)), full((d, d)), full((1, d))],
        out_specs=[row(d), row(d)],
        out_shape=[jax.ShapeDtypeStruct((b, r, d), F32), jax.ShapeDtypeStruct((b, r, d), BF16)],
        scratch_shapes=[pltpu.VMEM((tq + 2 * BLOCK, 512), BF16),
                        pltpu.VMEM((tq, D_ATTN), F32),
                        pltpu.VMEM((tq, D_SGU), F32)],
        compiler_params=_cparams(2),
        name="mixers_local" if local else "mixers_ctx",
    )(x, pc, pc, pc, ps, q, kv, kv, kv, kvc, mod, conv_w, sgu_g, sgu_w, sgu_b, sink_b, mix_g, w_out, n2g)


def _oddeven_merge(lo, hi, r):
    step = r * 2
    if step < hi - lo:
        yield from _oddeven_merge(lo, hi, step)
        yield from _oddeven_merge(lo + r, hi, step)
        yield from [(i, i + r) for i in range(lo + r, hi - r, step)]
    else:
        yield (lo, lo + r)


def _oddeven_sort(lo, hi):
    if hi - lo >= 1:
        mid = lo + (hi - lo) // 2
        yield from _oddeven_sort(lo, mid)
        yield from _oddeven_sort(mid + 1, hi)
        yield from _oddeven_merge(lo, hi, 1)


_SORT16 = tuple(_oddeven_sort(0, PEER_TOPK - 1))


def _sort16(x):
    x = list(x)
    for i, j in _SORT16:
        x[i], x[j] = jnp.maximum(x[i], x[j]), jnp.minimum(x[i], x[j])
    return x


def _merge_top16(a, b):
    n = PEER_TOPK
    c = [jnp.maximum(a[i], b[n - 1 - i]) for i in range(n)]
    d = n // 2
    while d:
        for i in range(n):
            if not i & d:
                c[i], c[i + d] = jnp.maximum(c[i], c[i + d]), jnp.minimum(c[i], c[i + d])
        d //= 2
    return c


def _top16(load):
    def tree(lo, n):
        if n == PEER_TOPK:
            return _sort16([load(lo + i) for i in range(n)])
        return _merge_top16(tree(lo, n // 2), tree(lo + n // 2, n // 2))
    return tree(0, N_KEYS)


def _peer_select(h_ref, ws_ref, u0_ref, ht_ref, e0_ref, e1_ref, th_ref, s0_ref, s1_ref, a0_ref):
    tm = h_ref.shape[0]
    nh = PEER_HEADS
    ht_ref[...] = h_ref[...].astype(F32).T.astype(BF16)
    s0 = _dot(_unpack_rows(ws_ref[0]), ht_ref[...])
    s1 = _dot(_unpack_rows(ws_ref[1]), ht_ref[...])
    for c in range(tm // LANES):
        s0_ref[c] = s0[:, c * LANES:(c + 1) * LANES]
        s1_ref[c] = s1[:, c * LANES:(c + 1) * LANES]

    rnd = lambda v: v.astype(BF16).astype(F32)

    def select(c, carry):
        lanes = pl.ds(pl.multiple_of(c * LANES, LANES), LANES)
        a0_ref[:, lanes] = _dot(_unpack_rows(u0_ref[...]), ht_ref[:, lanes]).astype(BF16)
        a = _top16(lambda i: s0_ref[c, i * nh:(i + 1) * nh, :])
        b = _top16(lambda j: s1_ref[c, j * nh:(j + 1) * nh, :])
        ea = [jnp.exp(v - a[0]) for v in a]
        eb = [jnp.exp(v - b[0]) for v in b]
        cand = [ea[k] * eb[l] for k, l in _CAND]
        rest = cand[PEER_TOPK:]
        rest = rest + [jnp.full_like(cand[0], -1.0)] * (-len(rest) % PEER_TOPK)
        best = cand[:PEER_TOPK]
        for g in range(0, len(rest), PEER_TOPK):
            best = _merge_top16(best, _sort16(rest[g:g + PEER_TOPK]))
        top = best[PEER_TOPK - 1]
        zsum = jnp.zeros_like(top)
        for p in cand:
            zsum = zsum + jnp.where(p >= top, p, 0.0)
        rz = 1.0 / zsum
        ean = [rnd(v * rz) for v in ea]
        ebn = [rnd(v) for v in eb]
        thn = jnp.full_like(top, jnp.inf)
        for (k, l), p in zip(_CAND, cand):
            thn = jnp.minimum(thn, jnp.where(p >= top, rnd(ean[k] * ebn[l]), jnp.inf))
        for hh in range(nh):
            th_ref[hh, :, lanes] = jnp.broadcast_to(thn[hh:hh + 1, :], (2 * SUBLANES, LANES)).astype(BF16)
        for i in range(N_KEYS):
            e0_ref[i, :, lanes] = jnp.exp(s0_ref[c, i * nh:(i + 1) * nh, :] - a[0]) * rz
        for hh in range(nh):
            s1_head = s1_ref[c, pl.ds(hh, N_KEYS, stride=nh), :]
            e1_ref[hh, :, lanes] = jnp.exp(s1_head - b[0][hh:hh + 1, :]).astype(BF16)
        return carry

    lax.fori_loop(0, tm // LANES, select, 0)


def _peer_gate(a_ref, row0, e0_ref, e1_ref, th_ref, hbuf_ref, lanes):
    n_lanes = lanes.stop - lanes.start
    pack = 2 * SUBLANES
    for ii in range(a_ref.shape[0] // N_KEYS):
        e0 = e0_ref[row0 + ii, :, lanes]
        e0r = [jnp.broadcast_to(e0[hh:hh + 1, :], (pack, n_lanes)).astype(BF16) for hh in range(PEER_HEADS)]
        for c in range(N_KEYS // pack):
            gate = None
            for hh in range(PEER_HEADS):
                p = e0r[hh] * e1_ref[hh, c * pack:(c + 1) * pack, lanes]
                sel = jnp.where(p >= th_ref[hh, :, lanes], p, jnp.zeros_like(p))
                gate = sel if gate is None else gate + sel
            r0 = ii * N_KEYS + c * pack
            act = _gelu_sigmoid_form(a_ref[r0:r0 + pack, lanes])
            hbuf_ref[r0:r0 + pack, lanes] = act * gate


def _peer_lane_split(tm):
    return 2 if tm % (2 * 2 * LANES) == 0 else 1


def _peer_kernel(h_ref, x_ref, mod_ref, fg_ref, ws_ref, u0_ref, uodd_ref, uevn_ref, vt_ref, o_ref,
                 ht_ref, e0_ref, e1_ref, th_ref, *scratch, final_norm):
    e = pl.program_id(2)
    ne = pl.num_programs(2)
    tm = h_ref.shape[0]
    n_split = _peer_lane_split(tm)
    s0_ref, s1_ref, aevn_ref, aodd_ref, hevn_ref, hodd_ref, acc_ref = scratch
    eb = 2 * u0_ref.shape[0]
    keys_per_block = eb // N_KEYS

    lane_ranges = [slice(s * (tm // n_split), (s + 1) * (tm // n_split)) for s in range(n_split)]
    stages = ((uodd_ref, aodd_ref, aevn_ref, hevn_ref), (uevn_ref, aevn_ref, aodd_ref, hodd_ref))
    chains = [(s, lanes) for s in range(len(stages)) for lanes in lane_ranges]

    @pl.when(e == 0)
    def _():
        _peer_select(h_ref, ws_ref, u0_ref, ht_ref, e0_ref, e1_ref, th_ref, s0_ref, s1_ref, aevn_ref)
        acc_ref[...] = jnp.zeros_like(acc_ref)

    def next_scores(s, lanes):
        u_next, a_next, _, _ = stages[s]
        a_next[:, lanes] = _dot(_unpack_rows(u_next[...]), ht_ref[:, lanes]).astype(BF16)

    def gates(s, lanes):
        _, _, a_cur, hbuf_ref = stages[s]
        _peer_gate(a_cur, (2 * e + s) * keys_per_block, e0_ref, e1_ref, th_ref, hbuf_ref, lanes)

    def values(s, lanes):
        hbuf_ref = stages[s][3]
        vt = _unpack_rows(vt_ref[:, s * eb:(s + 1) * eb])
        acc_ref[:, lanes] += _dot(vt, hbuf_ref[:, lanes])

    next_scores(*chains[0])
    for c, chain in enumerate(chains):
        if c + 1 < len(chains):
            next_scores(*chains[c + 1])
        gates(*chain)
        values(*chain)

    @pl.when(e == ne - 1)
    def _():
        y = x_ref[...] + mod_ref[5:6, :] * acc_ref[...].T
        if final_norm:
            y = _rms(y) * fg_ref[...]
        o_ref[...] = y


def _tables_kernel(u_ref, v_ref, up_ref, vtp_ref):
    up_ref[...] = _pack_rows_in_kernel(u_ref[...].astype(BF16))
    vtp_ref[...] = _pack_rows_in_kernel(v_ref[...].T.astype(BF16))


def _tables_call(peer_u, peer_v):
    depth, n_exp, d = peer_u.shape
    eb = PEER_EXPERTS
    return pl.pallas_call(
        _tables_kernel,
        grid=(depth, n_exp // eb),
        in_specs=[pl.BlockSpec((None, eb, d), lambda i, e: (i, e, 0)),
                  pl.BlockSpec((None, eb, d), lambda i, e: (i, e, 0))],
        out_specs=[pl.BlockSpec((None, eb // 2, d), lambda i, e: (i, e, 0)),
                   pl.BlockSpec((None, d // 2, eb), lambda i, e: (i, 0, e))],
        out_shape=[jax.ShapeDtypeStruct((depth, n_exp // 2, d), jnp.int32),
                   jax.ShapeDtypeStruct((depth, d // 2, n_exp), jnp.int32)],
        compiler_params=_cparams(2),
        name="pack_tables",
    )(peer_u, peer_v)


def _fold_kernel(k_ref, w_ref, o_ref):
    k = k_ref[...]
    w = w_ref[...]
    k_hi = k.astype(BF16)
    k_lo = (k - k_hi.astype(F32)).astype(BF16)
    w_hi = w.astype(BF16)
    w_lo = (w - w_hi.astype(F32)).astype(BF16)
    o_ref[...] = _pack_rows_in_kernel((_dot(k_hi, w_hi) + _dot(k_lo, w_hi) + _dot(k_hi, w_lo)).astype(BF16))


def _fold_call(keys_p, wq_t):
    _, m, k = keys_p.shape
    n = wq_t.shape[2]
    tn = 256
    return pl.pallas_call(
        _fold_kernel,
        grid=(2, n // tn),
        in_specs=[pl.BlockSpec((None, m, k), lambda p, j: (p, 0, 0)),
                  pl.BlockSpec((None, k, tn), lambda p, j: (p, 0, j))],
        out_specs=pl.BlockSpec((None, m // 2, tn), lambda p, j: (p, 0, j)),
        out_shape=jax.ShapeDtypeStruct((2, m // 2, n), jnp.int32),
        compiler_params=_cparams(2),
        name="fold_keys",
    )(keys_p, wq_t)


def _peer_call(h2, x, mod, final_g, w_s, u_b, vt_b, *, layer, final_norm):
    b, r, d = x.shape
    tm = min(PEER_TOKENS, r)
    eb = PEER_EXPERTS
    n_blocks = 2 * u_b.shape[1] // eb
    nh = PEER_HEADS
    score_chunks = pltpu.VMEM((tm // LANES, N_KEYS * nh, LANES), F32)
    row = lambda: pl.BlockSpec((None, tm, d), lambda i, t, e: (i, t, 0))
    full = lambda shape: pl.BlockSpec(shape, lambda i, t, e: (0,) * len(shape))
    return pl.pallas_call(
        functools.partial(_peer_kernel, final_norm=final_norm),
        grid=(b, r // tm, n_blocks // 2),
        in_specs=[row(), row(),
                  pl.BlockSpec((None, 6, d), lambda i, t, e: (i, 0, 0)),
                  full((1, d)), full(w_s.shape),
                  pl.BlockSpec((None, eb // 2, d), lambda i, t, e: (layer, 0, 0)),
                  pl.BlockSpec((None, eb // 2, d), lambda i, t, e: (layer, 2 * e + 1, 0)),
                  pl.BlockSpec((None, eb // 2, d),
                               lambda i, t, e: (layer, jnp.minimum(2 * e + 2, n_blocks - 2), 0)),
                  pl.BlockSpec((None, d // 2, 2 * eb), lambda i, t, e: (layer, 0, e))],
        out_specs=row(),
        out_shape=jax.ShapeDtypeStruct((b, r, d), F32),
        scratch_shapes=[pltpu.VMEM((d, tm), BF16),
                        pltpu.VMEM((N_KEYS, nh, tm), F32),
                        pltpu.VMEM((nh, N_KEYS, tm), BF16),
                        pltpu.VMEM((nh, 2 * SUBLANES, tm), BF16),
                        score_chunks,
                        score_chunks,
                        pltpu.VMEM((eb, tm), BF16),
                        pltpu.VMEM((eb, tm), BF16),
                        pltpu.VMEM((eb, tm), BF16),
                        pltpu.VMEM((eb, tm), BF16),
                        pltpu.VMEM((d, tm), F32)],
        compiler_params=_cparams(3),
        name="peer_final" if final_norm else "peer",
    )(h2, x, mod, final_g, w_s, u_b, u_b, u_b, vt_b)


def _rope_tables(length):
    rows = length // GRID_W
    row = jnp.repeat(jnp.arange(rows), GRID_W).astype(F32)
    col = jnp.tile(jnp.arange(GRID_W), rows).astype(F32)
    inv = ROPE_THETA ** (-jnp.arange(ROPE_FREQS, dtype=F32) / ROPE_FREQS)
    ar = row[:, None] * inv[None, :]
    ac = col[:, None] * inv[None, :]
    ang = jnp.concatenate([ar, ar, ac, ac, ar, ar, ac, ac], axis=-1)
    first = (jnp.arange(2 * HEAD_DIM) // ROPE_FREQS) % 2 == 0
    sin = jnp.sin(ang)
    return jnp.stack([jnp.cos(ang), jnp.where(first, -sin, 0.0), jnp.where(first, 0.0, sin)])


def kernel(x, c, ctx, c_ctx, w_ada, b_ada, norm1_g, norm2_g, w_in, conv_w, sgu_norm_g, sgu_w, sgu_b,
           attn_sink, mix_norm_g, w_out, peer_wq, peer_keys, peer_u, peer_v, final_g):
    bsz, length, d = x.shape
    n_ctx = ctx.shape[1]
    depth = w_ada.shape[0]
    nh = PEER_HEADS

    cc = jnp.zeros((SUBLANES, d), F32).at[:bsz].set(c).at[bsz].set(c_ctx)
    mod = _mod_call(cc, w_ada, b_ada)

    rope_l = _rope_tables(length)
    rope_c = jnp.zeros((3, n_ctx, 2 * HEAD_DIM), F32).at[0].set(1.0)
    fg = final_g.reshape(1, d)
    u_packed, vt_packed = _tables_call(peer_u, peer_v)

    xl, xc = x, ctx
    for i in range(depth):
        last = i == depth - 1
        mod_l = mod[i, :bsz].reshape(bsz, 6, d)
        mod_c = jnp.broadcast_to(mod[i, bsz].reshape(1, 6, d), (bsz, 6, d))
        n1g = norm1_g[i].reshape(1, d)
        n2g = norm2_g[i].reshape(1, d)
        w_in_b = w_in[i].astype(BF16)
        sgu_g = sgu_norm_g[i].reshape(1, D_SGU)
        sgu_wb = sgu_w[i].astype(BF16)
        sgu_bias = jnp.repeat(sgu_b[i].T, D_SGU // SGU_HEADS, axis=1)
        sink_b = jnp.broadcast_to(attn_sink[i][:, None], (N_HEADS, LANES))
        mix_g = mix_norm_g[i].reshape(1, d)
        w_out_b = w_out[i].astype(BF16)
        wq_t =peer_wq[i].reshape(d, nh, 2, PEER_DHALF).transpose(2, 1, 3, 0).reshape(2 * nh * PEER_DHALF, d)
        keys_p = jnp.einsum('hpid,hg->pihgd', peer_keys[i], jnp.eye(nh, dtype=F32))
        keys_p = keys_p.reshape(2, N_KEYS * nh, nh * PEER_DHALF)
        w_s = _fold_call(keys_p, wq_t.reshape(2, nh * PEER_DHALF, d))
        peer = functools.partial(_peer_call, w_s=w_s, u_b=u_packed, vt_b=vt_packed, layer=i)
        mixer_w = (conv_w[i], sgu_g, sgu_wb, sgu_bias, sink_b, mix_g, w_out_b, n2g)

        pc_c, ps_c, q_c, kv_c = _in_call(xc, mod_c, n1g, rope_c, w_in_b)
        pc_l, ps_l, q_l, kv_l = _in_call(xl, mod_l, n1g, rope_l, w_in_b)
        xl, h2_l = _mix_call(xl, pc_l, ps_l, q_l, kv_l, kv_c, mod_l, *mixer_w, local=True)
        if not last:
            xc, h2_c = _mix_call(xc, pc_c, ps_c, q_c, kv_c, kv_c, mod_c, *mixer_w, local=False)
            rows = min(PEER_TOKENS, bsz * n_ctx)
            merged = (bsz * n_ctx // rows, rows, d)
            xc = peer(h2_c.reshape(merged), xc.reshape(merged), mod_c[:merged[0]], fg, final_norm=False)
            xc = xc.reshape(bsz, n_ctx, d)
        xl = peer(h2_l, xl, mod_l, fg, final_norm=last)
    return xl
```

```python
import functools
import math

import jax
import jax.numpy as jnp
from jax import lax
from jax.experimental import pallas as pl
from jax.experimental.pallas import tpu as pltpu

F32 = jnp.float32
BF16 = jnp.bfloat16

EPS = 1e-6
GRID_W = 64
D_CONV = 256
D_SGU = 256
SGU_HEADS = 4
SGU_CHUNK = 128
N_HEADS = 8
N_KV_HEADS = 2
HEAD_DIM = 64
D_ATTN = N_HEADS * HEAD_DIM
BLOCK = 128
ROPE_THETA = 10000.0
ROPE_FREQS = HEAD_DIM // 4
CONV_END = 3 * D_CONV
SGU_END = CONV_END + 2 * D_SGU
Q_END = SGU_END + D_ATTN
K_END = Q_END + N_KV_HEADS * HEAD_DIM
N_KEYS = 128
PEER_HEADS = 8
PEER_TOPK = 16
PEER_DHALF = 128

LANES = 128
SUBLANES = 8
VMEM_LIMIT_BYTES = 56 * 1024 * 1024

ROW_TILE = 512
PEER_TOKENS = 512
PEER_EXPERTS = 1024

D_IN = K_END + N_KV_HEADS * HEAD_DIM
D_KV = N_KV_HEADS * HEAD_DIM
KV_K, KV_K_SWAPPED, KV_V, KV_V_SWAPPED = (slice(g * D_KV, (g + 1) * D_KV) for g in range(4))
KV_COLS = 4 * D_KV
MOD_COLS = 1536
FOLD_COLS = 256

_CAND = [(k, l) for k in range(PEER_TOPK) for l in range(PEER_TOPK) if (k + 1) * (l + 1) <= PEER_TOPK]


def _cparams(n_axes):
    return pltpu.CompilerParams(dimension_semantics=("arbitrary",) * n_axes,
                                vmem_limit_bytes=VMEM_LIMIT_BYTES)


def _gelu_sigmoid_form(x):
    k0 = -2.0 * math.sqrt(2.0 / math.pi) * math.log2(math.e)
    k1 = 0.044715 * k0
    return x / (1.0 + jnp.exp2(x * (x * x * k1 + k0)))


def _dot(a, b):
    return jnp.dot(a, b, preferred_element_type=F32)


def _dot_nt(a, b):
    return lax.dot_general(a, b, (((1,), (1,)), ((), ())), preferred_element_type=F32)


def _pack_rows_in_kernel(w):
    return pltpu.bitcast(w, jnp.int32)


def _unpack_rows(x):
    return pltpu.bitcast(x, BF16)


def _rms(x):
    return x * lax.rsqrt(jnp.mean(x * x, axis=-1, keepdims=True) + EPS)


def _mod_kernel(c_ref, w_ref, b_ref, o_ref):
    c = c_ref[...]
    sc = c / (1.0 + jnp.exp(-c))
    w = w_ref[...]
    c_hi = sc.astype(BF16)
    c_lo = (sc - c_hi.astype(F32)).astype(BF16)
    w_hi = w.astype(BF16)
    w_lo = (w - w_hi.astype(F32)).astype(BF16)
    o_ref[...] = _dot(c_hi, w_hi) + _dot(c_lo, w_hi) + _dot(c_hi, w_lo) + b_ref[...]


def _mod_call(cc, w_ada, b_ada):
    depth, d, n = w_ada.shape
    tn = MOD_COLS
    return pl.pallas_call(
        _mod_kernel,
        grid=(depth, n // tn),
        in_specs=[pl.BlockSpec((SUBLANES, d), lambda i, j: (0, 0)),
                  pl.BlockSpec((None, d, tn), lambda i, j: (i, 0, j)),
                  pl.BlockSpec((None, 1, tn), lambda i, j: (i, 0, j))],
        out_specs=pl.BlockSpec((None, SUBLANES, tn), lambda i, j: (i, 0, j)),
        out_shape=jax.ShapeDtypeStruct((depth, SUBLANES, n), F32),
        compiler_params=_cparams(2),
        name="adaln_mod",
    )(cc, w_ada, b_ada.reshape(depth, 1, n))


def _rotary(x, rope):
    n = x.shape[1]
    cos, sin_first, sin_second = (jnp.concatenate([rope[i]] * (n // LANES), axis=1) for i in range(3))
    from_right = pltpu.roll(x, n - ROPE_FREQS, axis=1)
    from_left = pltpu.roll(x, ROPE_FREQS, axis=1)
    return x * cos + from_right * sin_first + from_left * sin_second


def _in_kernel(x_ref, mod_ref, g_ref, rope_ref, w_ref, pc_ref, ps_ref, q_ref, kv_ref):
    x = x_ref[...]
    h = _rms(x) * g_ref[...] * (1.0 + mod_ref[1:2, :]) + mod_ref[0:1, :]
    hb = h.astype(BF16)

    def proj(lo, hi):
        return _dot(hb, w_ref[:, lo:hi])

    pc_ref[...] = proj(0, CONV_END)
    ps_ref[...] = proj(CONV_END, SGU_END)
    rope = rope_ref[...]
    q_ref[...] = (_rotary(proj(SGU_END, Q_END), rope) * HEAD_DIM ** -0.5).astype(BF16)
    k = _rotary(proj(Q_END, K_END), rope)
    v = proj(K_END, D_IN)
    kv_ref[:, KV_K] = k.astype(BF16)
    kv_ref[:, KV_K_SWAPPED] = pltpu.roll(k, HEAD_DIM, axis=1).astype(BF16)
    kv_ref[:, KV_V] = v.astype(BF16)
    kv_ref[:, KV_V_SWAPPED] = pltpu.roll(v, HEAD_DIM, axis=1).astype(BF16)


def _in_call(x, mod, norm_g, rope, w_in):
    b, r, d = x.shape
    tm = min(ROW_TILE, r)
    row = lambda n: pl.BlockSpec((None, tm, n), lambda i, t: (i, t, 0))
    return pl.pallas_call(
        _in_kernel,
        grid=(b, r // tm),
        in_specs=[row(d),
                  pl.BlockSpec((None, 6, d), lambda i, t: (i, 0, 0)),
                  pl.BlockSpec((1, d), lambda i, t: (0, 0)),
                  pl.BlockSpec((3, tm, LANES), lambda i, t: (0, t, 0)),
                  pl.BlockSpec((d, D_IN), lambda i, t: (0, 0))],
        out_specs=[row(CONV_END), row(2 * D_SGU), row(D_ATTN), row(KV_COLS)],
        out_shape=[jax.ShapeDtypeStruct((b, r, CONV_END), F32),
                   jax.ShapeDtypeStruct((b, r, 2 * D_SGU), F32),
                   jax.ShapeDtypeStruct((b, r, D_ATTN), BF16),
                   jax.ShapeDtypeStruct((b, r, KV_COLS), BF16)],
        compiler_params=_cparams(2),
        name="in_proj",
    )(x, mod, norm_g, rope, w_in)


def _fold_lanes(blocks, op):
    parts = [b[:, c:c + LANES] for b in blocks for c in range(0, b.shape[1], LANES)]
    out = parts[0]
    for p in parts[1:]:
        out = op(out, p)
    return out


def _mix_kernel(x_ref, pc_ref, pcp_ref, pcn_ref, ps_ref, q_ref, kv_ref, kvp_ref, kvn_ref, kvc_ref,
                mod_ref, convw_ref, sgug_ref, sguw_ref, sgub_ref, sink_ref, mixg_ref, wout_ref, n2g_ref,
                xo_ref, h2_ref, kvx_ref, attn_ref, sgu_ref, *, local):
    tq = x_ref.shape[0]
    nblk = tq // BLOCK
    t = pl.program_id(1)
    nt = pl.num_programs(1)

    pc = pc_ref[...]
    z = pc[:, D_CONV:2 * D_CONV] * pc[:, 2 * D_CONV:]
    last = SUBLANES - 1
    z_before = pcp_ref[last:last + 1, D_CONV:2 * D_CONV] * pcp_ref[last:last + 1, 2 * D_CONV:]
    z_after = pcn_ref[0:1, D_CONV:2 * D_CONV] * pcn_ref[0:1, 2 * D_CONV:]
    z_before = jnp.where(t > 0, z_before, 0.0)
    z_after = jnp.where(t < nt - 1, z_after, 0.0)
    rows = lax.broadcasted_iota(jnp.int32, (tq, D_CONV), 0)
    z_prev = jnp.where(rows == 0, z_before, pltpu.roll(z, 1, axis=0))
    z_next = jnp.where(rows == tq - 1, z_after, pltpu.roll(z, tq - 1, axis=0))
    conv = pc[:, :D_CONV] * (z_prev * convw_ref[0:1, :] + z * convw_ref[1:2, :] + z_next * convw_ref[2:3, :])

    zg = _gelu_sigmoid_form(ps_ref[...])
    u = zg[:, :D_SGU]
    v = zg[:, D_SGU:]
    mu = jnp.mean(v, axis=-1, keepdims=True)
    vc = v - mu
    vn = vc * lax.rsqrt(jnp.mean(vc * vc, axis=-1, keepdims=True) + EPS) * sgug_ref[...]
    lane = lax.broadcasted_iota(jnp.int32, (BLOCK, LANES), 1)
    low = lane < HEAD_DIM
    for cb in range(nblk):
        pieces = []
        for a in range(SGU_HEADS // 2):
            vp = vn[cb * BLOCK:(cb + 1) * BLOCK, a * LANES:(a + 1) * LANES]
            v_lo = jnp.where(low, vp, 0.0).astype(BF16)
            v_hi = jnp.where(low, 0.0, vp).astype(BF16)
            pieces.append(_dot(sguw_ref[2 * a], v_lo) + _dot(sguw_ref[2 * a + 1], v_hi))
        s = jnp.concatenate(pieces, axis=1) + sgub_ref[...]
        sgu_ref[cb * BLOCK:(cb + 1) * BLOCK, :] = u[cb * BLOCK:(cb + 1) * BLOCK, :] * s

    if local:
        kvx_ref[0:BLOCK, :] = kvp_ref[...]
        kvx_ref[BLOCK:BLOCK + tq, :] = kv_ref[...]
        kvx_ref[BLOCK + tq:, :] = kvn_ref[...]
    kvc = kvc_ref[...]
    qi = lax.broadcasted_iota(jnp.int32, (2 * BLOCK, 3 * BLOCK), 0) & (BLOCK - 1)
    ko = lax.broadcasted_iota(jnp.int32, (2 * BLOCK, 3 * BLOCK), 1)
    band = (ko >= qi) & (ko <= qi + 2 * BLOCK)
    upper = lax.broadcasted_iota(jnp.int32, (2 * BLOCK, 1), 0) < BLOCK

    def attend(jb, carry):
        r0 = pl.multiple_of(jb * BLOCK, BLOCK)
        if local:
            n = t * nblk + jb
            first_key = jnp.where(n > 0, 0, BLOCK)
            end_key = jnp.where(n < nt * nblk - 1, 3 * BLOCK, 2 * BLOCK)
            ok = band & (ko >= first_key) & (ko < end_key)
            kvl = kvx_ref[pl.ds(r0, 3 * BLOCK), :]
        pairs = [q_ref[pl.ds(r0, BLOCK), a * LANES:(a + 1) * LANES] for a in range(N_HEADS // 2)]
        groups = []
        for kvh in range(N_KV_HEADS):
            for even in (True, False):
                members = [2 * kvh, 2 * kvh + 1]
                keep = low if even else jnp.logical_not(low)
                qg = jnp.concatenate([jnp.where(keep, pairs[a], jnp.zeros_like(pairs[a])) for a in members], axis=0)
                natural = (kvh == 0) == even
                ksel = KV_K if natural else KV_K_SWAPPED
                vsel = KV_V if natural else KV_V_SWAPPED
                s_list = [_dot_nt(qg, kvc[:, ksel])]
                v_list = [kvc[:, vsel]]
                if local:
                    s_list.append(jnp.where(ok, _dot_nt(qg, kvl[:, ksel]), -jnp.inf))
                    v_list.append(kvl[:, vsel])
                hds = [2 * a + (0 if even else 1) for a in members]
                sink = jnp.where(upper, sink_ref[hds[0]:hds[0] + 1, 0:1], sink_ref[hds[1]:hds[1] + 1, 0:1])
                groups.append((hds, s_list, v_list, sink))
        probs = []
        for _, s_list, _, sink in groups:
            m = jnp.maximum(sink, jnp.max(_fold_lanes(s_list, jnp.maximum), axis=-1, keepdims=True))
            ps = [jnp.exp(s - m) for s in s_list]
            denom = jnp.exp(sink - m) + jnp.sum(_fold_lanes(ps, jnp.add), axis=-1, keepdims=True)
            probs.append(([p.astype(BF16) for p in ps], 1.0 / denom))
        outs = [None] * N_HEADS
        for (ps, rden), (hds, _, v_list, _) in zip(probs, groups):
            o = _dot(ps[0], v_list[0])
            for p, vv in zip(ps[1:], v_list[1:]):
                o = o + _dot(p, vv)
            o = o * rden
            outs[hds[0]] = o[:BLOCK]
            outs[hds[1]] = o[BLOCK:]
        for a in range(N_HEADS // 2):
            attn_ref[pl.ds(r0, BLOCK), a * LANES:(a + 1) * LANES] = jnp.where(low, outs[2 * a], outs[2 * a + 1])
        return carry

    lax.fori_loop(0, nblk, attend, 0)

    g = mixg_ref[...]
    yc = (_rms(conv) * g[:, :D_CONV]).astype(BF16)
    ys = (_rms(sgu_ref[...]) * g[:, D_CONV:D_CONV + D_SGU]).astype(BF16)
    ya = (_rms(attn_ref[...]) * g[:, D_CONV + D_SGU:]).astype(BF16)
    yl = (_dot(yc, wout_ref[0:D_CONV, :]) + _dot(ys, wout_ref[D_CONV:D_CONV + D_SGU, :])
          + _dot(ya, wout_ref[D_CONV + D_SGU:, :]))
    xn = x_ref[...] + mod_ref[2:3, :] * yl
    xo_ref[...] = xn
    h2 = _rms(xn) * n2g_ref[...] * (1.0 + mod_ref[4:5, :]) + mod_ref[3:4, :]
    h2_ref[...] = h2.astype(BF16)


def _mix_call(x, pc, ps, q, kv, kvc, mod, conv_w, sgu_g, sgu_w, sgu_b, sink_b, mix_g, w_out, n2g, *, local):
    b, r, d = x.shape
    c = kvc.shape[1]
    tq = min(ROW_TILE, r)
    nt = r // tq
    hb = tq // SUBLANES
    kb = tq // BLOCK
    row = lambda n: pl.BlockSpec((None, tq, n), lambda i, t: (i, t, 0))
    full = lambda shape: pl.BlockSpec(shape, lambda i, t: (0,) * len(shape))
    return pl.pallas_call(
        functools.partial(_mix_kernel, local=local),
        grid=(b, nt),
        in_specs=[row(d), row(CONV_END),
                  pl.BlockSpec((None, SUBLANES, CONV_END), lambda i, t: (i, jnp.maximum(t * hb - 1, 0), 0)),
                  pl.BlockSpec((None, SUBLANES, CONV_END), lambda i, t: (i, jnp.minimum((t + 1) * hb, nt * hb - 1), 0)),
                  row(2 * D_SGU), row(D_ATTN), row(KV_COLS),
                  pl.BlockSpec((None, BLOCK, KV_COLS), lambda i, t: (i, jnp.maximum(t * kb - 1, 0), 0)),
                  pl.BlockSpec((None, BLOCK, KV_COLS), lambda i, t: (i, jnp.minimum((t + 1) * kb, nt * kb - 1), 0)),
                  pl.BlockSpec((None, c, KV_COLS), lambda i, t: (i, 0, 0)),
                  pl.BlockSpec((None, 6, d), lambda i, t: (i, 0, 0)),
                  full((3, D_CONV)), full((1, D_SGU)), full((SGU_HEADS, SGU_CHUNK, SGU_CHUNK)),
                  full((SGU_CHUNK, D_SGU)), full((N_HEADS, LANES)), full((1, d)), full((d, d)), full((1, d))],
        out_specs=[row(d), row(d)],
        out_shape=[jax.ShapeDtypeStruct((b, r, d), F32), jax.ShapeDtypeStruct((b, r, d), BF16)],
        scratch_shapes=[pltpu.VMEM((tq + 2 * BLOCK, KV_COLS), BF16),
                        pltpu.VMEM((tq, D_ATTN), F32),
                        pltpu.VMEM((tq, D_SGU), F32)],
        compiler_params=_cparams(2),
        name="mixers_local" if local else "mixers_ctx",
    )(x, pc, pc, pc, ps, q, kv, kv, kv, kvc, mod, conv_w, sgu_g, sgu_w, sgu_b, sink_b, mix_g, w_out, n2g)


def _oddeven_merge(lo, hi, r):
    step = r * 2
    if step < hi - lo:
        yield from _oddeven_merge(lo, hi, step)
        yield from _oddeven_merge(lo + r, hi, step)
        yield from [(i, i + r) for i in range(lo + r, hi - r, step)]
    else:
        yield (lo, lo + r)


def _oddeven_sort(lo, hi):
    if hi - lo >= 1:
        mid = lo + (hi - lo) // 2
        yield from _oddeven_sort(lo, mid)
        yield from _oddeven_sort(mid + 1, hi)
        yield from _oddeven_merge(lo, hi, 1)


_SORT16 = tuple(_oddeven_sort(0, PEER_TOPK - 1))


def _sort16(x):
    x = list(x)
    for i, j in _SORT16:
        x[i], x[j] = jnp.maximum(x[i], x[j]), jnp.minimum(x[i], x[j])
    return x


def _merge_top16(a, b):
    n = PEER_TOPK
    c = [jnp.maximum(a[i], b[n - 1 - i]) for i in range(n)]
    d = n // 2
    while d:
        for i in range(n):
            if not i & d:
                c[i], c[i + d] = jnp.maximum(c[i], c[i + d]), jnp.minimum(c[i], c[i + d])
        d //= 2
    return c


def _top16(load):
    def tree(lo, n):
        if n == PEER_TOPK:
            return _sort16([load(lo + i) for i in range(n)])
        return _merge_top16(tree(lo, n // 2), tree(lo + n // 2, n // 2))
    return tree(0, N_KEYS)


def _peer_select(h_ref, ws_ref, u0_ref, ht_ref, e0_ref, e1_ref, th_ref, s0_ref, s1_ref, a0_ref):
    tm = h_ref.shape[0]
    nh = PEER_HEADS
    ht_ref[...] = h_ref[...].astype(F32).T.astype(BF16)
    s0 = _dot(_unpack_rows(ws_ref[0]), ht_ref[...])
    s1 = _dot(_unpack_rows(ws_ref[1]), ht_ref[...])
    for c in range(tm // LANES):
        s0_ref[c] = s0[:, c * LANES:(c + 1) * LANES]
        s1_ref[c] = s1[:, c * LANES:(c + 1) * LANES]

    rnd = lambda v: v.astype(BF16).astype(F32)

    def select(c, carry):
        lanes = pl.ds(pl.multiple_of(c * LANES, LANES), LANES)
        a0_ref[:, lanes] = _dot(_unpack_rows(u0_ref[...]), ht_ref[:, lanes]).astype(BF16)
        a = _top16(lambda i: s0_ref[c, i * nh:(i + 1) * nh, :])
        b = _top16(lambda j: s1_ref[c, j * nh:(j + 1) * nh, :])
        ea = [jnp.exp(v - a[0]) for v in a]
        eb = [jnp.exp(v - b[0]) for v in b]
        cand = [ea[k] * eb[l] for k, l in _CAND]
        rest = cand[PEER_TOPK:]
        rest = rest + [jnp.full_like(cand[0], -1.0)] * (-len(rest) % PEER_TOPK)
        best = cand[:PEER_TOPK]
        for g in range(0, len(rest), PEER_TOPK):
            best = _merge_top16(best, _sort16(rest[g:g + PEER_TOPK]))
        top = best[PEER_TOPK - 1]
        zsum = jnp.zeros_like(top)
        for p in cand:
            zsum = zsum + jnp.where(p >= top, p, 0.0)
        rz = 1.0 / zsum
        ean = [rnd(v * rz) for v in ea]
        ebn = [rnd(v) for v in eb]
        thn = jnp.full_like(top, jnp.inf)
        for (k, l), p in zip(_CAND, cand):
            thn = jnp.minimum(thn, jnp.where(p >= top, rnd(ean[k] * ebn[l]), jnp.inf))
        for hh in range(nh):
            th_ref[hh, :, lanes] = jnp.broadcast_to(thn[hh:hh + 1, :], (2 * SUBLANES, LANES)).astype(BF16)
        for i in range(N_KEYS):
            e0_ref[i, :, lanes] = jnp.exp(s0_ref[c, i * nh:(i + 1) * nh, :] - a[0]) * rz
        for hh in range(nh):
            s1_head = s1_ref[c, pl.ds(hh, N_KEYS, stride=nh), :]
            e1_ref[hh, :, lanes] = jnp.exp(s1_head - b[0][hh:hh + 1, :]).astype(BF16)
        return carry

    lax.fori_loop(0, tm // LANES, select, 0)


def _peer_gate(a_ref, row0, e0_ref, e1_ref, th_ref, hbuf_ref, lanes):
    n_lanes = lanes.stop - lanes.start
    pack = 2 * SUBLANES
    for ii in range(a_ref.shape[0] // N_KEYS):
        e0 = e0_ref[row0 + ii, :, lanes]
        e0r = [jnp.broadcast_to(e0[hh:hh + 1, :], (pack, n_lanes)).astype(BF16) for hh in range(PEER_HEADS)]
        for c in range(N_KEYS // pack):
            gate = None
            for hh in range(PEER_HEADS):
                p = e0r[hh] * e1_ref[hh, c * pack:(c + 1) * pack, lanes]
                sel = jnp.where(p >= th_ref[hh, :, lanes], p, jnp.zeros_like(p))
                gate = sel if gate is None else gate + sel
            r0 = ii * N_KEYS + c * pack
            act = _gelu_sigmoid_form(a_ref[r0:r0 + pack, lanes])
            hbuf_ref[r0:r0 + pack, lanes] = act * gate


def _peer_lane_split(tm):
    return 2 if tm % (2 * 2 * LANES) == 0 else 1


def _peer_kernel(h_ref, x_ref, mod_ref, fg_ref, ws_ref, u0_ref, uodd_ref, uevn_ref, vt_ref, o_ref,
                 ht_ref, e0_ref, e1_ref, th_ref, *scratch, final_norm):
    e = pl.program_id(2)
    ne = pl.num_programs(2)
    tm = h_ref.shape[0]
    n_split = _peer_lane_split(tm)
    s0_ref, s1_ref, aevn_ref, aodd_ref, hevn_ref, hodd_ref, acc_ref = scratch
    eb = 2 * u0_ref.shape[0]
    keys_per_block = eb // N_KEYS

    lane_ranges = [slice(s * (tm // n_split), (s + 1) * (tm // n_split)) for s in range(n_split)]
    stages = ((uodd_ref, aodd_ref, aevn_ref, hevn_ref), (uevn_ref, aevn_ref, aodd_ref, hodd_ref))
    chains = [(s, lanes) for s in range(len(stages)) for lanes in lane_ranges]

    @pl.when(e == 0)
    def _():
        _peer_select(h_ref, ws_ref, u0_ref, ht_ref, e0_ref, e1_ref, th_ref, s0_ref, s1_ref, aevn_ref)
        acc_ref[...] = jnp.zeros_like(acc_ref)

    def next_scores(s, lanes):
        u_next, a_next, _, _ = stages[s]
        a_next[:, lanes] = _dot(_unpack_rows(u_next[...]), ht_ref[:, lanes]).astype(BF16)

    def gates(s, lanes):
        _, _, a_cur, hbuf_ref = stages[s]
        _peer_gate(a_cur, (2 * e + s) * keys_per_block, e0_ref, e1_ref, th_ref, hbuf_ref, lanes)

    def values(s, lanes):
        hbuf_ref = stages[s][3]
        vt = _unpack_rows(vt_ref[:, s * eb:(s + 1) * eb])
        acc_ref[:, lanes] += _dot(vt, hbuf_ref[:, lanes])

    next_scores(*chains[0])
    for c, chain in enumerate(chains):
        if c + 1 < len(chains):
            next_scores(*chains[c + 1])
        gates(*chain)
        values(*chain)

    @pl.when(e == ne - 1)
    def _():
        y = x_ref[...] + mod_ref[5:6, :] * acc_ref[...].T
        if final_norm:
            y = _rms(y) * fg_ref[...]
        o_ref[...] = y


def _tables_kernel(u_ref, v_ref, up_ref, vtp_ref):
    up_ref[...] = _pack_rows_in_kernel(u_ref[...].astype(BF16))
    vtp_ref[...] = _pack_rows_in_kernel(v_ref[...].T.astype(BF16))


def _tables_call(peer_u, peer_v):
    depth, n_exp, d = peer_u.shape
    eb = PEER_EXPERTS
    return pl.pallas_call(
        _tables_kernel,
        grid=(depth, n_exp // eb),
        in_specs=[pl.BlockSpec((None, eb, d), lambda i, e: (i, e, 0)),
                  pl.BlockSpec((None, eb, d), lambda i, e: (i, e, 0))],
        out_specs=[pl.BlockSpec((None, eb // 2, d), lambda i, e: (i, e, 0)),
                   pl.BlockSpec((None, d // 2, eb), lambda i, e: (i, 0, e))],
        out_shape=[jax.ShapeDtypeStruct((depth, n_exp // 2, d), jnp.int32),
                   jax.ShapeDtypeStruct((depth, d // 2, n_exp), jnp.int32)],
        compiler_params=_cparams(2),
        name="pack_tables",
    )(peer_u, peer_v)


def _fold_kernel(k_ref, w_ref, o_ref):
    k = k_ref[...]
    w = w_ref[...]
    k_hi = k.astype(BF16)
    k_lo = (k - k_hi.astype(F32)).astype(BF16)
    w_hi = w.astype(BF16)
    w_lo = (w - w_hi.astype(F32)).astype(BF16)
    o_ref[...] = _pack_rows_in_kernel((_dot(k_hi, w_hi) + _dot(k_lo, w_hi) + _dot(k_hi, w_lo)).astype(BF16))


def _fold_call(keys_p, wq_t):
    _, m, k = keys_p.shape
    n = wq_t.shape[2]
    tn = FOLD_COLS
    return pl.pallas_call(
        _fold_kernel,
        grid=(2, n // tn),
        in_specs=[pl.BlockSpec((None, m, k), lambda p, j: (p, 0, 0)),
                  pl.BlockSpec((None, k, tn), lambda p, j: (p, 0, j))],
        out_specs=pl.BlockSpec((None, m // 2, tn), lambda p, j: (p, 0, j)),
        out_shape=jax.ShapeDtypeStruct((2, m // 2, n), jnp.int32),
        compiler_params=_cparams(2),
        name="fold_keys",
    )(keys_p, wq_t)


def _peer_call(h2, x, mod, final_g, w_s, u_b, vt_b, *, layer, final_norm):
    b, r, d = x.shape
    tm = min(PEER_TOKENS, r)
    eb = PEER_EXPERTS
    n_blocks = 2 * u_b.shape[1] // eb
    nh = PEER_HEADS
    score_chunks = pltpu.VMEM((tm // LANES, N_KEYS * nh, LANES), F32)
    row = lambda: pl.BlockSpec((None, tm, d), lambda i, t, e: (i, t, 0))
    full = lambda shape: pl.BlockSpec(shape, lambda i, t, e: (0,) * len(shape))
    return pl.pallas_call(
        functools.partial(_peer_kernel, final_norm=final_norm),
        grid=(b, r // tm, n_blocks // 2),
        in_specs=[row(), row(),
                  pl.BlockSpec((None, 6, d), lambda i, t, e: (i, 0, 0)),
                  full((1, d)), full(w_s.shape),
                  pl.BlockSpec((None, eb // 2, d), lambda i, t, e: (layer, 0, 0)),
                  pl.BlockSpec((None, eb // 2, d), lambda i, t, e: (layer, 2 * e + 1, 0)),
                  pl.BlockSpec((None, eb // 2, d),
                               lambda i, t, e: (layer, jnp.minimum(2 * e + 2, n_blocks - 2), 0)),
                  pl.BlockSpec((None, d // 2, 2 * eb), lambda i, t, e: (layer, 0, e))],
        out_specs=row(),
        out_shape=jax.ShapeDtypeStruct((b, r, d), F32),
        scratch_shapes=[pltpu.VMEM((d, tm), BF16),
                        pltpu.VMEM((N_KEYS, nh, tm), F32),
                        pltpu.VMEM((nh, N_KEYS, tm), BF16),
                        pltpu.VMEM((nh, 2 * SUBLANES, tm), BF16),
                        score_chunks,
                        score_chunks,
                        pltpu.VMEM((eb, tm), BF16),
                        pltpu.VMEM((eb, tm), BF16),
                        pltpu.VMEM((eb, tm), BF16),
                        pltpu.VMEM((eb, tm), BF16),
                        pltpu.VMEM((d, tm), F32)],
        compiler_params=_cparams(3),
        name="peer_final" if final_norm else "peer",
    )(h2, x, mod, final_g, w_s, u_b, u_b, u_b, vt_b)


def _rope_tables(length):
    rows = length // GRID_W
    row = jnp.repeat(jnp.arange(rows), GRID_W).astype(F32)
    col = jnp.tile(jnp.arange(GRID_W), rows).astype(F32)
    inv = ROPE_THETA ** (-jnp.arange(ROPE_FREQS, dtype=F32) / ROPE_FREQS)
    ar = row[:, None] * inv[None, :]
    ac = col[:, None] * inv[None, :]
    ang = jnp.concatenate([ar, ar, ac, ac, ar, ar, ac, ac], axis=-1)
    first = (jnp.arange(2 * HEAD_DIM) // ROPE_FREQS) % 2 == 0
    sin = jnp.sin(ang)
    return jnp.stack([jnp.cos(ang), jnp.where(first, -sin, 0.0), jnp.where(first, 0.0, sin)])


def kernel(x, c, ctx, c_ctx, w_ada, b_ada, norm1_g, norm2_g, w_in, conv_w, sgu_norm_g, sgu_w, sgu_b,
           attn_sink, mix_norm_g, w_out, peer_wq, peer_keys, peer_u, peer_v, final_g):
    bsz, length, d = x.shape
    n_ctx = ctx.shape[1]
    depth = w_ada.shape[0]
    nh = PEER_HEADS
    n_exp = peer_u.shape[1]
    assert w_in.shape[1:] == (d, D_IN) and w_out.shape[1:] == (d, d) and w_ada.shape[2] == 6 * d
    assert peer_wq.shape[1:] == (d, 2 * nh * PEER_DHALF) and peer_keys.shape[1:] == (nh, 2, N_KEYS, PEER_DHALF)
    assert n_exp == N_KEYS * N_KEYS and n_exp % (2 * PEER_EXPERTS) == 0 and PEER_EXPERTS % (SUBLANES * N_KEYS) == 0
    assert bsz + 1 <= SUBLANES and (6 * d) % MOD_COLS == 0 and d % FOLD_COLS == 0
    assert length % GRID_W == 0 and length % min(ROW_TILE, length) == 0 and min(ROW_TILE, length) % BLOCK == 0
    assert length % min(PEER_TOKENS, length) == 0 and (bsz * n_ctx) % min(PEER_TOKENS, bsz * n_ctx) == 0
    assert n_ctx % BLOCK == 0 and n_ctx % min(ROW_TILE, n_ctx) == 0

    cc = jnp.zeros((SUBLANES, d), F32).at[:bsz].set(c).at[bsz].set(c_ctx)
    mod = _mod_call(cc, w_ada, b_ada)

    rope_l = _rope_tables(length)
    rope_c = jnp.zeros((3, n_ctx, 2 * HEAD_DIM), F32).at[0].set(1.0)
    fg = final_g.reshape(1, d)
    u_packed, vt_packed = _tables_call(peer_u, peer_v)

    xl, xc = x, ctx
    for i in range(depth):
        last = i == depth - 1
        mod_l = mod[i, :bsz].reshape(bsz, 6, d)
        mod_c = jnp.broadcast_to(mod[i, bsz].reshape(1, 6, d), (bsz, 6, d))
        n1g = norm1_g[i].reshape(1, d)
        n2g = norm2_g[i].reshape(1, d)
        w_in_b = w_in[i].astype(BF16)
        sgu_g = sgu_norm_g[i].reshape(1, D_SGU)
        sgu_wb = sgu_w[i].astype(BF16)
        sgu_bias = jnp.repeat(sgu_b[i].T, D_SGU // SGU_HEADS, axis=1)
        sink_b = jnp.broadcast_to(attn_sink[i][:, None], (N_HEADS, LANES))
        mix_g = mix_norm_g[i].reshape(1, d)
        w_out_b = w_out[i].astype(BF16)
        wq_t = peer_wq[i].reshape(d, nh, 2, PEER_DHALF).transpose(2, 1, 3, 0).reshape(2 * nh * PEER_DHALF, d)
        keys_p = jnp.einsum('hpid,hg->pihgd', peer_keys[i], jnp.eye(nh, dtype=F32))
        keys_p = keys_p.reshape(2, N_KEYS * nh, nh * PEER_DHALF)
        w_s = _fold_call(keys_p, wq_t.reshape(2, nh * PEER_DHALF, d))
        peer = functools.partial(_peer_call, w_s=w_s, u_b=u_packed, vt_b=vt_packed, layer=i)
        mixer_w = (conv_w[i], sgu_g, sgu_wb, sgu_bias, sink_b, mix_g, w_out_b, n2g)

        pc_c, ps_c, q_c, kv_c = _in_call(xc, mod_c, n1g, rope_c, w_in_b)
        pc_l, ps_l, q_l, kv_l = _in_call(xl, mod_l, n1g, rope_l, w_in_b)
        xl, h2_l = _mix_call(xl, pc_l, ps_l, q_l, kv_l, kv_c, mod_l, *mixer_w, local=True)
        if not last:
            xc, h2_c = _mix_call(xc, pc_c, ps_c, q_c, kv_c, kv_c, mod_c, *mixer_w, local=False)
            rows = min(PEER_TOKENS, bsz * n_ctx)
            merged = (bsz * n_ctx // rows, rows, d)
            xc = peer(h2_c.reshape(merged), xc.reshape(merged), mod_c[:merged[0]], fg, final_norm=False)
            xc = xc.reshape(bsz, n_ctx, d)
        xl = peer(h2_l, xl, mod_l, fg, final_norm=last)
    return xl
```

```python
import functools
import math

import jax
import jax.numpy as jnp
from jax import lax
from jax.experimental import pallas as pl
from jax.experimental.pallas import tpu as pltpu

F32 = jnp.float32
BF16 = jnp.bfloat16

EPS = 1e-6
GRID_W = 64
D_CONV = 256
D_SGU = 256
SGU_HEADS = 4
SGU_CHUNK = 128
N_HEADS = 8
N_KV_HEADS = 2
HEAD_DIM = 64
D_ATTN = N_HEADS * HEAD_DIM
BLOCK = 128
ROPE_THETA = 10000.0
ROPE_FREQS = HEAD_DIM // 4
CONV_END = 3 * D_CONV
SGU_END = CONV_END + 2 * D_SGU
Q_END = SGU_END + D_ATTN
K_END = Q_END + N_KV_HEADS * HEAD_DIM
N_KEYS = 128
PEER_HEADS = 8
PEER_TOPK = 16
PEER_DHALF = 128

LANES = 128
SUBLANES = 8
VMEM_LIMIT_BYTES = 56 * 1024 * 1024

ROW_TILE = 512
PEER_TOKENS = 512
PEER_EXPERTS = 1024

D_IN = K_END + N_KV_HEADS * HEAD_DIM
D_KV = N_KV_HEADS * HEAD_DIM
KV_K, KV_K_SWAPPED, KV_V, KV_V_SWAPPED = (slice(g * D_KV, (g + 1) * D_KV) for g in range(4))
KV_COLS = 4 * D_KV
MOD_COLS = 1536
FOLD_COLS = 256

_CAND = [(k, l) for k in range(PEER_TOPK) for l in range(PEER_TOPK) if (k + 1) * (l + 1) <= PEER_TOPK]


def _cparams(n_axes):
    return pltpu.CompilerParams(dimension_semantics=("arbitrary",) * n_axes,
                                vmem_limit_bytes=VMEM_LIMIT_BYTES)


def _gelu_sigmoid_form(x):
    k0 = -2.0 * math.sqrt(2.0 / math.pi) * math.log2(math.e)
    k1 = 0.044715 * k0
    return x / (1.0 + jnp.exp2(x * (x * x * k1 + k0)))


def _dot(a, b):
    return jnp.dot(a, b, preferred_element_type=F32)


def _dot_nt(a, b):
    return lax.dot_general(a, b, (((1,), (1,)), ((), ())), preferred_element_type=F32)


def _pack_rows_in_kernel(w):
    return pltpu.bitcast(w, jnp.int32)


def _unpack_rows(x):
    return pltpu.bitcast(x, BF16)


def _rms(x):
    return x * lax.rsqrt(jnp.mean(x * x, axis=-1, keepdims=True) + EPS)


def _mod_kernel(c_ref, w_ref, b_ref, o_ref):
    c = c_ref[...]
    sc = c / (1.0 + jnp.exp(-c))
    w = w_ref[...]
    c_hi = sc.astype(BF16)
    c_lo = (sc - c_hi.astype(F32)).astype(BF16)
    w_hi = w.astype(BF16)
    w_lo = (w - w_hi.astype(F32)).astype(BF16)
    o_ref[...] = _dot(c_hi, w_hi) + _dot(c_lo, w_hi) + _dot(c_hi, w_lo) + b_ref[...]


def _mod_call(cc, w_ada, b_ada):
    depth, d, n = w_ada.shape
    tn = MOD_COLS
    return pl.pallas_call(
        _mod_kernel,
        grid=(depth, n // tn),
        in_specs=[pl.BlockSpec((SUBLANES, d), lambda i, j: (0, 0)),
                  pl.BlockSpec((None, d, tn), lambda i, j: (i, 0, j)),
                  pl.BlockSpec((None, 1, tn), lambda i, j: (i, 0, j))],
        out_specs=pl.BlockSpec((None, SUBLANES, tn), lambda i, j: (i, 0, j)),
        out_shape=jax.ShapeDtypeStruct((depth, SUBLANES, n), F32),
        compiler_params=_cparams(2),
        name="adaln_mod",
    )(cc, w_ada, b_ada.reshape(depth, 1, n))


def _rotary(x, rope):
    n = x.shape[1]
    cos, sin_first, sin_second = (jnp.concatenate([rope[i]] * (n // LANES), axis=1) for i in range(3))
    from_right = pltpu.roll(x, n - ROPE_FREQS, axis=1)
    from_left = pltpu.roll(x, ROPE_FREQS, axis=1)
    return x * cos + from_right * sin_first + from_left * sin_second


def _in_kernel(x_ref, mod_ref, g_ref, rope_ref, w_ref, pc_ref, ps_ref, q_ref, kv_ref):
    x = x_ref[...]
    h = _rms(x) * g_ref[...] * (1.0 + mod_ref[1:2, :]) + mod_ref[0:1, :]
    hb = h.astype(BF16)

    def proj(lo, hi):
        return _dot(hb, w_ref[:, lo:hi])

    pc_ref[...] = proj(0, CONV_END)
    ps_ref[...] = proj(CONV_END, SGU_END)
    rope = rope_ref[...]
    q_ref[...] = (_rotary(proj(SGU_END, Q_END), rope) * (HEAD_DIM ** -0.5 * math.log2(math.e))).astype(BF16)
    k = _rotary(proj(Q_END, K_END), rope)
    v = proj(K_END, D_IN)
    kv_ref[:, KV_K] = k.astype(BF16)
    kv_ref[:, KV_K_SWAPPED] = pltpu.roll(k, HEAD_DIM, axis=1).astype(BF16)
    kv_ref[:, KV_V] = v.astype(BF16)
    kv_ref[:, KV_V_SWAPPED] = pltpu.roll(v, HEAD_DIM, axis=1).astype(BF16)


def _in_call(x, mod, norm_g, rope, w_in):
    b, r, d = x.shape
    tm = min(ROW_TILE, r)
    row = lambda n: pl.BlockSpec((None, tm, n), lambda i, t: (i, t, 0))
    return pl.pallas_call(
        _in_kernel,
        grid=(b, r // tm),
        in_specs=[row(d),
                  pl.BlockSpec((None, 6, d), lambda i, t: (i, 0, 0)),
                  pl.BlockSpec((1, d), lambda i, t: (0, 0)),
                  pl.BlockSpec((3, tm, LANES), lambda i, t: (0, t, 0)),
                  pl.BlockSpec((d, D_IN), lambda i, t: (0, 0))],
        out_specs=[row(CONV_END), row(2 * D_SGU), row(D_ATTN), row(KV_COLS)],
        out_shape=[jax.ShapeDtypeStruct((b, r, CONV_END), F32),
                   jax.ShapeDtypeStruct((b, r, 2 * D_SGU), F32),
                   jax.ShapeDtypeStruct((b, r, D_ATTN), BF16),
                   jax.ShapeDtypeStruct((b, r, KV_COLS), BF16)],
        compiler_params=_cparams(2),
        name="in_proj",
    )(x, mod, norm_g, rope, w_in)


def _fold_lanes(blocks, op):
    parts = [b[:, c:c + LANES] for b in blocks for c in range(0, b.shape[1], LANES)]
    out = parts[0]
    for p in parts[1:]:
        out = op(out, p)
    return out


def _mix_kernel(x_ref, pc_ref, pcp_ref, pcn_ref, ps_ref, q_ref, kv_ref, kvp_ref, kvn_ref, kvc_ref,
                mod_ref, convw_ref, sgug_ref, sguw_ref, sgub_ref, sink_ref, mixg_ref, wout_ref, n2g_ref,
                xo_ref, h2_ref, kvx_ref, attn_ref, sgu_ref, *, local):
    tq = x_ref.shape[0]
    nblk = tq // BLOCK
    t = pl.program_id(1)
    nt = pl.num_programs(1)

    pc = pc_ref[...]
    z = pc[:, D_CONV:2 * D_CONV] * pc[:, 2 * D_CONV:]
    last = SUBLANES - 1
    z_before = pcp_ref[last:last + 1, D_CONV:2 * D_CONV] * pcp_ref[last:last + 1, 2 * D_CONV:]
    z_after = pcn_ref[0:1, D_CONV:2 * D_CONV] * pcn_ref[0:1, 2 * D_CONV:]
    z_before = jnp.where(t > 0, z_before, 0.0)
    z_after = jnp.where(t < nt - 1, z_after, 0.0)
    rows = lax.broadcasted_iota(jnp.int32, (tq, D_CONV), 0)
    z_prev = jnp.where(rows == 0, z_before, pltpu.roll(z, 1, axis=0))
    z_next = jnp.where(rows == tq - 1, z_after, pltpu.roll(z, tq - 1, axis=0))
    conv = pc[:, :D_CONV] * (z_prev * convw_ref[0:1, :] + z * convw_ref[1:2, :] + z_next * convw_ref[2:3, :])

    zg = _gelu_sigmoid_form(ps_ref[...])
    u = zg[:, :D_SGU]
    v = zg[:, D_SGU:]
    mu = jnp.mean(v, axis=-1, keepdims=True)
    vc = v - mu
    vn = vc * lax.rsqrt(jnp.mean(vc * vc, axis=-1, keepdims=True) + EPS) * sgug_ref[...]
    lane = lax.broadcasted_iota(jnp.int32, (BLOCK, LANES), 1)
    low = lane < HEAD_DIM
    for cb in range(nblk):
        pieces = []
        for a in range(SGU_HEADS // 2):
            vp = vn[cb * BLOCK:(cb + 1) * BLOCK, a * LANES:(a + 1) * LANES]
            v_lo = jnp.where(low, vp, 0.0).astype(BF16)
            v_hi = jnp.where(low, 0.0, vp).astype(BF16)
            pieces.append(_dot(sguw_ref[2 * a], v_lo) + _dot(sguw_ref[2 * a + 1], v_hi))
        s = jnp.concatenate(pieces, axis=1) + sgub_ref[...]
        sgu_ref[cb * BLOCK:(cb + 1) * BLOCK, :] = u[cb * BLOCK:(cb + 1) * BLOCK, :] * s

    if local:
        kvx_ref[0:BLOCK, :] = kvp_ref[...]
        kvx_ref[BLOCK:BLOCK + tq, :] = kv_ref[...]
        kvx_ref[BLOCK + tq:, :] = kvn_ref[...]
    kvc = kvc_ref[...]
    qi = lax.broadcasted_iota(jnp.int32, (2 * BLOCK, BLOCK), 0) & (BLOCK - 1)
    kj = lax.broadcasted_iota(jnp.int32, (2 * BLOCK, BLOCK), 1)
    upper = lax.broadcasted_iota(jnp.int32, (2 * BLOCK, 1), 0) < BLOCK

    def attend(jb, carry):
        r0 = pl.multiple_of(jb * BLOCK, BLOCK)
        if local:
            n = t * nblk + jb
            ok_prev = (kj >= qi) & (kj >= jnp.where(n > 0, 0, BLOCK))
            ok_next = (kj <= qi) & (kj < jnp.where(n < nt * nblk - 1, BLOCK, 0))
            kvl = kvx_ref[pl.ds(r0, 3 * BLOCK), :]

            def band(s):
                return jnp.concatenate([jnp.where(ok_prev, s[:, :BLOCK], -jnp.inf), s[:, BLOCK:2 * BLOCK],
                                        jnp.where(ok_next, s[:, 2 * BLOCK:], -jnp.inf)], axis=1)
        pairs = [q_ref[pl.ds(r0, BLOCK), a * LANES:(a + 1) * LANES] for a in range(N_HEADS // 2)]
        groups = []
        for kvh in range(N_KV_HEADS):
            for even in (True, False):
                members = [2 * kvh, 2 * kvh + 1]
                keep = low if even else jnp.logical_not(low)
                qg = jnp.concatenate([jnp.where(keep, pairs[a], jnp.zeros_like(pairs[a])) for a in members], axis=0)
                natural = (kvh == 0) == even
                ksel = KV_K if natural else KV_K_SWAPPED
                vsel = KV_V if natural else KV_V_SWAPPED
                s_list = [_dot_nt(qg, kvc[:, ksel])]
                v_list = [kvc[:, vsel]]
                if local:
                    s_list.append(band(_dot_nt(qg, kvl[:, ksel])))
                    v_list.append(kvl[:, vsel])
                hds = [2 * a + (0 if even else 1) for a in members]
                sink = jnp.where(upper, sink_ref[hds[0]:hds[0] + 1, 0:1], sink_ref[hds[1]:hds[1] + 1, 0:1])
                groups.append((hds, s_list, v_list, sink))
        probs = []
        for _, s_list, _, sink in groups:
            m = jnp.maximum(sink, jnp.max(_fold_lanes(s_list, jnp.maximum), axis=-1, keepdims=True))
            ps = [jnp.exp2(s - m) for s in s_list]
            denom = jnp.exp2(sink - m) + jnp.sum(_fold_lanes(ps, jnp.add), axis=-1, keepdims=True)
            probs.append(([p.astype(BF16) for p in ps], 1.0 / denom))
        outs = [None] * N_HEADS
        for (ps, rden), (hds, _, v_list, _) in zip(probs, groups):
            o = _dot(ps[0], v_list[0])
            for p, vv in zip(ps[1:], v_list[1:]):
                o = o + _dot(p, vv)
            o = o * rden
            outs[hds[0]] = o[:BLOCK]
            outs[hds[1]] = o[BLOCK:]
        for a in range(N_HEADS // 2):
            attn_ref[pl.ds(r0, BLOCK), a * LANES:(a + 1) * LANES] = jnp.where(low, outs[2 * a], outs[2 * a + 1])
        return carry

    lax.fori_loop(0, nblk, attend, 0)

    g = mixg_ref[...]
    yc = (_rms(conv) * g[:, :D_CONV]).astype(BF16)
    ys = (_rms(sgu_ref[...]) * g[:, D_CONV:D_CONV + D_SGU]).astype(BF16)
    ya = (_rms(attn_ref[...]) * g[:, D_CONV + D_SGU:]).astype(BF16)
    yl = (_dot(yc, wout_ref[0:D_CONV, :]) + _dot(ys, wout_ref[D_CONV:D_CONV + D_SGU, :])
          + _dot(ya, wout_ref[D_CONV + D_SGU:, :]))
    xn = x_ref[...] + mod_ref[2:3, :] * yl
    xo_ref[...] = xn
    h2 = _rms(xn) * n2g_ref[...] * (1.0 + mod_ref[4:5, :]) + mod_ref[3:4, :]
    h2_ref[...] = h2.astype(BF16)


def _mix_call(x, pc, ps, q, kv, kvc, mod, conv_w, sgu_g, sgu_w, sgu_b, sink_b, mix_g, w_out, n2g, *, local):
    b, r, d = x.shape
    c = kvc.shape[1]
    tq = min(ROW_TILE, r)
    nt = r // tq
    hb = tq // SUBLANES
    kb = tq // BLOCK
    row = lambda n: pl.BlockSpec((None, tq, n), lambda i, t: (i, t, 0))
    full = lambda shape: pl.BlockSpec(shape, lambda i, t: (0,) * len(shape))
    return pl.pallas_call(
        functools.partial(_mix_kernel, local=local),
        grid=(b, nt),
        in_specs=[row(d), row(CONV_END),
                  pl.BlockSpec((None, SUBLANES, CONV_END), lambda i, t: (i, jnp.maximum(t * hb - 1, 0), 0)),
                  pl.BlockSpec((None, SUBLANES, CONV_END), lambda i, t: (i, jnp.minimum((t + 1) * hb, nt * hb - 1), 0)),
                  row(2 * D_SGU), row(D_ATTN), row(KV_COLS),
                  pl.BlockSpec((None, BLOCK, KV_COLS), lambda i, t: (i, jnp.maximum(t * kb - 1, 0), 0)),
                  pl.BlockSpec((None, BLOCK, KV_COLS), lambda i, t: (i, jnp.minimum((t + 1) * kb, nt * kb - 1), 0)),
                  pl.BlockSpec((None, c, KV_COLS), lambda i, t: (i, 0, 0)),
                  pl.BlockSpec((None, 6, d), lambda i, t: (i, 0, 0)),
                  full((3, D_CONV)), full((1, D_SGU)), full((SGU_HEADS, SGU_CHUNK, SGU_CHUNK)),
                  full((SGU_CHUNK, D_SGU)), full((N_HEADS, LANES)), full((1, d)), full((d, d)), full((1, d))],
        out_specs=[row(d), row(d)],
        out_shape=[jax.ShapeDtypeStruct((b, r, d), F32), jax.ShapeDtypeStruct((b, r, d), BF16)],
        scratch_shapes=[pltpu.VMEM((tq + 2 * BLOCK, KV_COLS), BF16),
                        pltpu.VMEM((tq, D_ATTN), F32),
                        pltpu.VMEM((tq, D_SGU), F32)],
        compiler_params=_cparams(2),
        name="mixers_local" if local else "mixers_ctx",
    )(x, pc, pc, pc, ps, q, kv, kv, kv, kvc, mod, conv_w, sgu_g, sgu_w, sgu_b, sink_b, mix_g, w_out, n2g)


def _oddeven_merge(lo, hi, r):
    step = r * 2
    if step < hi - lo:
        yield from _oddeven_merge(lo, hi, step)
        yield from _oddeven_merge(lo + r, hi, step)
        yield from [(i, i + r) for i in range(lo + r, hi - r, step)]
    else:
        yield (lo, lo + r)


def _oddeven_sort(lo, hi):
    if hi - lo >= 1:
        mid = lo + (hi - lo) // 2
        yield from _oddeven_sort(lo, mid)
        yield from _oddeven_sort(mid + 1, hi)
        yield from _oddeven_merge(lo, hi, 1)


_SORT16 = tuple(_oddeven_sort(0, PEER_TOPK - 1))


def _sort16(x):
    x = list(x)
    for i, j in _SORT16:
        x[i], x[j] = jnp.maximum(x[i], x[j]), jnp.minimum(x[i], x[j])
    return x


def _merge_top16(a, b):
    n = PEER_TOPK
    c = [jnp.maximum(a[i], b[n - 1 - i]) for i in range(n)]
    d = n // 2
    while d:
        for i in range(n):
            if not i & d:
                c[i], c[i + d] = jnp.maximum(c[i], c[i + d]), jnp.minimum(c[i], c[i + d])
        d //= 2
    return c


def _top16(load):
    def tree(lo, n):
        if n == PEER_TOPK:
            return _sort16([load(lo + i) for i in range(n)])
        return _merge_top16(tree(lo, n // 2), tree(lo + n // 2, n // 2))
    return tree(0, N_KEYS)


def _peer_select(h_ref, ws_ref, u0_ref, ht_ref, e0_ref, e1_ref, th_ref, s0_ref, s1_ref, a0_ref):
    tm = h_ref.shape[0]
    nh = PEER_HEADS
    ht_ref[...] = h_ref[...].astype(F32).T.astype(BF16)
    s0 = _dot(_unpack_rows(ws_ref[0]), ht_ref[...])
    s1 = _dot(_unpack_rows(ws_ref[1]), ht_ref[...])
    for c in range(tm // LANES):
        s0_ref[c] = s0[:, c * LANES:(c + 1) * LANES]
        s1_ref[c] = s1[:, c * LANES:(c + 1) * LANES]

    rnd = lambda v: v.astype(BF16).astype(F32)

    def select(c, carry):
        lanes = pl.ds(pl.multiple_of(c * LANES, LANES), LANES)
        a0_ref[:, lanes] = _dot(_unpack_rows(u0_ref[...]), ht_ref[:, lanes]).astype(BF16)
        a = _top16(lambda i: s0_ref[c, i * nh:(i + 1) * nh, :])
        b = _top16(lambda j: s1_ref[c, j * nh:(j + 1) * nh, :])
        ea = [jnp.exp(v - a[0]) for v in a]
        eb = [jnp.exp(v - b[0]) for v in b]
        cand = [ea[k] * eb[l] for k, l in _CAND]
        rest = cand[PEER_TOPK:]
        rest = rest + [jnp.full_like(cand[0], -1.0)] * (-len(rest) % PEER_TOPK)
        best = cand[:PEER_TOPK]
        for g in range(0, len(rest), PEER_TOPK):
            best = _merge_top16(best, _sort16(rest[g:g + PEER_TOPK]))
        top = best[PEER_TOPK - 1]
        zsum = jnp.zeros_like(top)
        for p in cand:
            zsum = zsum + jnp.where(p >= top, p, 0.0)
        rz = 1.0 / zsum
        ean = [rnd(v * rz) for v in ea]
        ebn = [rnd(v) for v in eb]
        thn = jnp.full_like(top, jnp.inf)
        for (k, l), p in zip(_CAND, cand):
            thn = jnp.minimum(thn, jnp.where(p >= top, rnd(ean[k] * ebn[l]), jnp.inf))
        for hh in range(nh):
            th_ref[hh, :, lanes] = jnp.broadcast_to(thn[hh:hh + 1, :], (2 * SUBLANES, LANES)).astype(BF16)
        for i in range(N_KEYS):
            e0_ref[i, :, lanes] = jnp.exp(s0_ref[c, i * nh:(i + 1) * nh, :] - a[0]) * rz
        for hh in range(nh):
            s1_head = s1_ref[c, pl.ds(hh, N_KEYS, stride=nh), :]
            e1_ref[hh, :, lanes] = jnp.exp(s1_head - b[0][hh:hh + 1, :]).astype(BF16)
        return carry

    lax.fori_loop(0, tm // LANES, select, 0)


def _peer_gate(a_ref, row0, e0_ref, e1_ref, th_ref, hbuf_ref, lanes):
    n_lanes = lanes.stop - lanes.start
    pack = 2 * SUBLANES
    for ii in range(a_ref.shape[0] // N_KEYS):
        e0 = e0_ref[row0 + ii, :, lanes]
        e0r = [jnp.broadcast_to(e0[hh:hh + 1, :], (pack, n_lanes)).astype(BF16) for hh in range(PEER_HEADS)]
        for c in range(N_KEYS // pack):
            gate = None
            for hh in range(PEER_HEADS):
                p = e0r[hh] * e1_ref[hh, c * pack:(c + 1) * pack, lanes]
                sel = jnp.where(p >= th_ref[hh, :, lanes], p, jnp.zeros_like(p))
                gate = sel if gate is None else gate + sel
            r0 = ii * N_KEYS + c * pack
            act = _gelu_sigmoid_form(a_ref[r0:r0 + pack, lanes])
            hbuf_ref[r0:r0 + pack, lanes] = act * gate


def _peer_lane_split(tm):
    return 2 if tm % (2 * 2 * LANES) == 0 else 1


def _peer_kernel(h_ref, x_ref, mod_ref, fg_ref, ws_ref, u0_ref, uodd_ref, uevn_ref, vt_ref, o_ref,
                 ht_ref, e0_ref, e1_ref, th_ref, *scratch, final_norm):
    e = pl.program_id(2)
    ne = pl.num_programs(2)
    tm = h_ref.shape[0]
    n_split = _peer_lane_split(tm)
    s0_ref, s1_ref, aevn_ref, aodd_ref, hevn_ref, hodd_ref, acc_ref = scratch
    eb = 2 * u0_ref.shape[0]
    keys_per_block = eb // N_KEYS

    lane_ranges = [slice(s * (tm // n_split), (s + 1) * (tm // n_split)) for s in range(n_split)]
    stages = ((uodd_ref, aodd_ref, aevn_ref, hevn_ref), (uevn_ref, aevn_ref, aodd_ref, hodd_ref))
    chains = [(s, lanes) for s in range(len(stages)) for lanes in lane_ranges]

    @pl.when(e == 0)
    def _():
        _peer_select(h_ref, ws_ref, u0_ref, ht_ref, e0_ref, e1_ref, th_ref, s0_ref, s1_ref, aevn_ref)
        acc_ref[...] = jnp.zeros_like(acc_ref)

    def next_scores(s, lanes):
        u_next, a_next, _, _ = stages[s]
        a_next[:, lanes] = _dot(_unpack_rows(u_next[...]), ht_ref[:, lanes]).astype(BF16)

    def gates(s, lanes):
        _, _, a_cur, hbuf_ref = stages[s]
        _peer_gate(a_cur, (2 * e + s) * keys_per_block, e0_ref, e1_ref, th_ref, hbuf_ref, lanes)

    def values(s, lanes):
        hbuf_ref = stages[s][3]
        vt = _unpack_rows(vt_ref[:, s * eb:(s + 1) * eb])
        acc_ref[:, lanes] += _dot(vt, hbuf_ref[:, lanes])

    next_scores(*chains[0])
    for c, chain in enumerate(chains):
        if c + 1 < len(chains):
            next_scores(*chains[c + 1])
        gates(*chain)
        values(*chain)

    @pl.when(e == ne - 1)
    def _():
        y = x_ref[...] + mod_ref[5:6, :] * acc_ref[...].T
        if final_norm:
            y = _rms(y) * fg_ref[...]
        o_ref[...] = y


def _tables_kernel(u_ref, v_ref, up_ref, vtp_ref):
    up_ref[...] = _pack_rows_in_kernel(u_ref[...].astype(BF16))
    vtp_ref[...] = _pack_rows_in_kernel(v_ref[...].T.astype(BF16))


def _tables_call(peer_u, peer_v):
    depth, n_exp, d = peer_u.shape
    eb = PEER_EXPERTS
    return pl.pallas_call(
        _tables_kernel,
        grid=(depth, n_exp // eb),
        in_specs=[pl.BlockSpec((None, eb, d), lambda i, e: (i, e, 0)),
                  pl.BlockSpec((None, eb, d), lambda i, e: (i, e, 0))],
        out_specs=[pl.BlockSpec((None, eb // 2, d), lambda i, e: (i, e, 0)),
                   pl.BlockSpec((None, d // 2, eb), lambda i, e: (i, 0, e))],
        out_shape=[jax.ShapeDtypeStruct((depth, n_exp // 2, d), jnp.int32),
                   jax.ShapeDtypeStruct((depth, d // 2, n_exp), jnp.int32)],
        compiler_params=_cparams(2),
        name="pack_tables",
    )(peer_u, peer_v)


def _fold_kernel(k_ref, w_ref, o_ref):
    k = k_ref[...]
    w = w_ref[...]
    k_hi = k.astype(BF16)
    k_lo = (k - k_hi.astype(F32)).astype(BF16)
    w_hi = w.astype(BF16)
    w_lo = (w - w_hi.astype(F32)).astype(BF16)
    o_ref[...] = _pack_rows_in_kernel((_dot(k_hi, w_hi) + _dot(k_lo, w_hi) + _dot(k_hi, w_lo)).astype(BF16))


def _fold_call(keys_p, wq_t):
    _, m, k = keys_p.shape
    n = wq_t.shape[2]
    tn = FOLD_COLS
    return pl.pallas_call(
        _fold_kernel,
        grid=(2, n // tn),
        in_specs=[pl.BlockSpec((None, m, k), lambda p, j: (p, 0, 0)),
                  pl.BlockSpec((None, k, tn), lambda p, j: (p, 0, j))],
        out_specs=pl.BlockSpec((None, m // 2, tn), lambda p, j: (p, 0, j)),
        out_shape=jax.ShapeDtypeStruct((2, m // 2, n), jnp.int32),
        compiler_params=_cparams(2),
        name="fold_keys",
    )(keys_p, wq_t)


def _peer_call(h2, x, mod, final_g, w_s, u_b, vt_b, *, layer, final_norm):
    b, r, d = x.shape
    tm = min(PEER_TOKENS, r)
    eb = PEER_EXPERTS
    n_blocks = 2 * u_b.shape[1] // eb
    nh = PEER_HEADS
    score_chunks = pltpu.VMEM((tm // LANES, N_KEYS * nh, LANES), F32)
    row = lambda: pl.BlockSpec((None, tm, d), lambda i, t, e: (i, t, 0))
    full = lambda shape: pl.BlockSpec(shape, lambda i, t, e: (0,) * len(shape))
    return pl.pallas_call(
        functools.partial(_peer_kernel, final_norm=final_norm),
        grid=(b, r // tm, n_blocks // 2),
        in_specs=[row(), row(),
                  pl.BlockSpec((None, 6, d), lambda i, t, e: (i, 0, 0)),
                  full((1, d)), full(w_s.shape),
                  pl.BlockSpec((None, eb // 2, d), lambda i, t, e: (layer, 0, 0)),
                  pl.BlockSpec((None, eb // 2, d), lambda i, t, e: (layer, 2 * e + 1, 0)),
                  pl.BlockSpec((None, eb // 2, d),
                               lambda i, t, e: (layer, jnp.minimum(2 * e + 2, n_blocks - 2), 0)),
                  pl.BlockSpec((None, d // 2, 2 * eb), lambda i, t, e: (layer, 0, e))],
        out_specs=row(),
        out_shape=jax.ShapeDtypeStruct((b, r, d), F32),
        scratch_shapes=[pltpu.VMEM((d, tm), BF16),
                        pltpu.VMEM((N_KEYS, nh, tm), F32),
                        pltpu.VMEM((nh, N_KEYS, tm), BF16),
                        pltpu.VMEM((nh, 2 * SUBLANES, tm), BF16),
                        score_chunks,
                        score_chunks,
                        pltpu.VMEM((eb, tm), BF16),
                        pltpu.VMEM((eb, tm), BF16),
                        pltpu.VMEM((eb, tm), BF16),
                        pltpu.VMEM((eb, tm), BF16),
                        pltpu.VMEM((d, tm), F32)],
        compiler_params=_cparams(3),
        name="peer_final" if final_norm else "peer",
    )(h2, x, mod, final_g, w_s, u_b, u_b, u_b, vt_b)


def _rope_tables(length):
    rows = length // GRID_W
    row = jnp.repeat(jnp.arange(rows), GRID_W).astype(F32)
    col = jnp.tile(jnp.arange(GRID_W), rows).astype(F32)
    inv = ROPE_THETA ** (-jnp.arange(ROPE_FREQS, dtype=F32) / ROPE_FREQS)
    ar = row[:, None] * inv[None, :]
    ac = col[:, None] * inv[None, :]
    ang = jnp.concatenate([ar, ar, ac, ac, ar, ar, ac, ac], axis=-1)
    first = (jnp.arange(2 * HEAD_DIM) // ROPE_FREQS) % 2 == 0
    sin = jnp.sin(ang)
    return jnp.stack([jnp.cos(ang), jnp.where(first, -sin, 0.0), jnp.where(first, 0.0, sin)])


def kernel(x, c, ctx, c_ctx, w_ada, b_ada, norm1_g, norm2_g, w_in, conv_w, sgu_norm_g, sgu_w, sgu_b,
           attn_sink, mix_norm_g, w_out, peer_wq, peer_keys, peer_u, peer_v, final_g):
    bsz, length, d = x.shape
    n_ctx = ctx.shape[1]
    depth = w_ada.shape[0]
    nh = PEER_HEADS
    n_exp = peer_u.shape[1]
    assert w_in.shape[1:] == (d, D_IN) and w_out.shape[1:] == (d, d) and w_ada.shape[2] == 6 * d
    assert peer_wq.shape[1:] == (d, 2 * nh * PEER_DHALF) and peer_keys.shape[1:] == (nh, 2, N_KEYS, PEER_DHALF)
    assert n_exp == N_KEYS * N_KEYS and n_exp % (2 * PEER_EXPERTS) == 0 and PEER_EXPERTS % (SUBLANES * N_KEYS) == 0
    assert bsz + 1 <= SUBLANES and (6 * d) % MOD_COLS == 0 and d % FOLD_COLS == 0
    assert length % GRID_W == 0 and length % min(ROW_TILE, length) == 0 and min(ROW_TILE, length) % BLOCK == 0
    assert length % min(PEER_TOKENS, length) == 0 and (bsz * n_ctx) % min(PEER_TOKENS, bsz * n_ctx) == 0
    assert n_ctx % BLOCK == 0 and n_ctx % min(ROW_TILE, n_ctx) == 0

    cc = jnp.zeros((SUBLANES, d), F32).at[:bsz].set(c).at[bsz].set(c_ctx)
    mod = _mod_call(cc, w_ada, b_ada)

    rope_l = _rope_tables(length)
    rope_c = jnp.zeros((3, n_ctx, 2 * HEAD_DIM), F32).at[0].set(1.0)
    fg = final_g.reshape(1, d)
    u_packed, vt_packed = _tables_call(peer_u, peer_v)

    xl, xc = x, ctx
    for i in range(depth):
        last = i == depth - 1
        mod_l = mod[i, :bsz].reshape(bsz, 6, d)
        mod_c = jnp.broadcast_to(mod[i, bsz].reshape(1, 6, d), (bsz, 6, d))
        n1g = norm1_g[i].reshape(1, d)
        n2g = norm2_g[i].reshape(1, d)
        w_in_b = w_in[i].astype(BF16)
        sgu_g = sgu_norm_g[i].reshape(1, D_SGU)
        sgu_wb = sgu_w[i].astype(BF16)
        sgu_bias = jnp.repeat(sgu_b[i].T, D_SGU // SGU_HEADS, axis=1)
        sink_b = jnp.broadcast_to(attn_sink[i][:, None] * math.log2(math.e), (N_HEADS, LANES))
        mix_g = mix_norm_g[i].reshape(1, d)
        w_out_b = w_out[i].astype(BF16)
        wq_t = peer_wq[i].reshape(d, nh, 2, PEER_DHALF).transpose(2, 1, 3, 0).reshape(2 * nh * PEER_DHALF, d)
        keys_p = jnp.einsum('hpid,hg->pihgd', peer_keys[i], jnp.eye(nh, dtype=F32))
        keys_p = keys_p.reshape(2, N_KEYS * nh, nh * PEER_DHALF)
        w_s = _fold_call(keys_p, wq_t.reshape(2, nh * PEER_DHALF, d))
        peer = functools.partial(_peer_call, w_s=w_s, u_b=u_packed, vt_b=vt_packed, layer=i)
        mixer_w = (conv_w[i], sgu_g, sgu_wb, sgu_bias, sink_b, mix_g, w_out_b, n2g)

        pc_c, ps_c, q_c, kv_c = _in_call(xc, mod_c, n1g, rope_c, w_in_b)
        pc_l, ps_l, q_l, kv_l = _in_call(xl, mod_l, n1g, rope_l, w_in_b)
        xl, h2_l = _mix_call(xl, pc_l, ps_l, q_l, kv_l, kv_c, mod_l, *mixer_w, local=True)
        if not last:
            xc, h2_c = _mix_call(xc, pc_c, ps_c, q_c, kv_c, kv_c, mod_c, *mixer_w, local=False)
            rows = min(PEER_TOKENS, bsz * n_ctx)
            merged = (bsz * n_ctx // rows, rows, d)
            xc = peer(h2_c.reshape(merged), xc.reshape(merged), mod_c[:merged[0]], fg, final_norm=False)
            xc = xc.reshape(bsz, n_ctx, d)
        xl = peer(h2_l, xl, mod_l, fg, final_norm=last)
    return xl
```

```python
import functools
import math

import jax
import jax.numpy as jnp
from jax import lax
from jax.experimental import pallas as pl
from jax.experimental.pallas import tpu as pltpu

F32 = jnp.float32
BF16 = jnp.bfloat16

EPS = 1e-6
GRID_W = 64
D_CONV = 256
D_SGU = 256
SGU_HEADS = 4
SGU_CHUNK = 128
N_HEADS = 8
N_KV_HEADS = 2
HEAD_DIM = 64
D_ATTN = N_HEADS * HEAD_DIM
BLOCK = 128
ROPE_THETA = 10000.0
ROPE_FREQS = HEAD_DIM // 4
CONV_END = 3 * D_CONV
SGU_END = CONV_END + 2 * D_SGU
Q_END = SGU_END + D_ATTN
K_END = Q_END + N_KV_HEADS * HEAD_DIM
N_KEYS = 128
PEER_HEADS = 8
PEER_TOPK = 16
PEER_DHALF = 128

LANES = 128
SUBLANES = 8
VMEM_LIMIT_BYTES = 56 * 1024 * 1024

IN_TILE = 1024
ROW_TILE = 512
PEER_TOKENS = 512
PEER_EXPERTS = 1024

D_IN = K_END + N_KV_HEADS * HEAD_DIM
D_KV = N_KV_HEADS * HEAD_DIM
KV_K, KV_K_SWAPPED, KV_V, KV_V_SWAPPED = (slice(g * D_KV, (g + 1) * D_KV) for g in range(4))
KV_COLS = 4 * D_KV
MOD_COLS = 1536
FOLD_COLS = 256

_CAND = [(k, l) for k in range(PEER_TOPK) for l in range(PEER_TOPK) if (k + 1) * (l + 1) <= PEER_TOPK]


def _cparams(n_axes):
    return pltpu.CompilerParams(dimension_semantics=("arbitrary",) * n_axes,
                                vmem_limit_bytes=VMEM_LIMIT_BYTES)


def _gelu_sigmoid_form(x):
    k0 = -2.0 * math.sqrt(2.0 / math.pi) * math.log2(math.e)
    k1 = 0.044715 * k0
    return x / (1.0 + jnp.exp2(x * (x * x * k1 + k0)))


def _dot(a, b):
    return jnp.dot(a, b, preferred_element_type=F32)


def _dot_nt(a, b):
    return lax.dot_general(a, b, (((1,), (1,)), ((), ())), preferred_element_type=F32)


def _pack_rows_in_kernel(w):
    return pltpu.bitcast(w, jnp.int32)


def _unpack_rows(x):
    return pltpu.bitcast(x, BF16)


def _rms(x):
    return x * lax.rsqrt(jnp.mean(x * x, axis=-1, keepdims=True) + EPS)


def _mod_kernel(c_ref, w_ref, b_ref, o_ref):
    c = c_ref[...]
    sc = c / (1.0 + jnp.exp(-c))
    w = w_ref[...]
    c_hi = sc.astype(BF16)
    c_lo = (sc - c_hi.astype(F32)).astype(BF16)
    w_hi = w.astype(BF16)
    w_lo = (w - w_hi.astype(F32)).astype(BF16)
    o_ref[...] = _dot(c_hi, w_hi) + _dot(c_lo, w_hi) + _dot(c_hi, w_lo) + b_ref[...]


def _mod_call(cc, w_ada, b_ada):
    depth, d, n = w_ada.shape
    tn = MOD_COLS
    return pl.pallas_call(
        _mod_kernel,
        grid=(depth, n // tn),
        in_specs=[pl.BlockSpec((SUBLANES, d), lambda i, j: (0, 0)),
                  pl.BlockSpec((None, d, tn), lambda i, j: (i, 0, j)),
                  pl.BlockSpec((None, 1, tn), lambda i, j: (i, 0, j))],
        out_specs=pl.BlockSpec((None, SUBLANES, tn), lambda i, j: (i, 0, j)),
        out_shape=jax.ShapeDtypeStruct((depth, SUBLANES, n), F32),
        compiler_params=_cparams(2),
        name="adaln_mod",
    )(cc, w_ada, b_ada.reshape(depth, 1, n))


def _rotary(x, rope):
    n = x.shape[1]
    cos, sin_first, sin_second = (jnp.concatenate([rope[i]] * (n // LANES), axis=1) for i in range(3))
    from_right = pltpu.roll(x, n - ROPE_FREQS, axis=1)
    from_left = pltpu.roll(x, ROPE_FREQS, axis=1)
    return x * cos + from_right * sin_first + from_left * sin_second


def _in_kernel(x_ref, mod_ref, g_ref, rope_ref, w_ref, pc_ref, ps_ref, q_ref, kv_ref):
    x = x_ref[...]
    h = _rms(x) * g_ref[...] * (1.0 + mod_ref[1:2, :]) + mod_ref[0:1, :]
    hb = h.astype(BF16)

    def proj(lo, hi):
        return _dot(hb, w_ref[:, lo:hi])

    pc_ref[...] = proj(0, CONV_END)
    ps_ref[...] = proj(CONV_END, SGU_END)
    rope = rope_ref[...]
    q_ref[...] = (_rotary(proj(SGU_END, Q_END), rope) * (HEAD_DIM ** -0.5 * math.log2(math.e))).astype(BF16)
    k = _rotary(proj(Q_END, K_END), rope)
    v = proj(K_END, D_IN)
    kv_ref[:, KV_K] = k.astype(BF16)
    kv_ref[:, KV_K_SWAPPED] = pltpu.roll(k, HEAD_DIM, axis=1).astype(BF16)
    kv_ref[:, KV_V] = v.astype(BF16)
    kv_ref[:, KV_V_SWAPPED] = pltpu.roll(v, HEAD_DIM, axis=1).astype(BF16)


def _in_call(x, mod, norm_g, rope, w_in):
    b, r, d = x.shape
    tm = min(IN_TILE, r)
    row = lambda n: pl.BlockSpec((None, tm, n), lambda i, t: (i, t, 0))
    return pl.pallas_call(
        _in_kernel,
        grid=(b, r // tm),
        in_specs=[row(d),
                  pl.BlockSpec((None, 6, d), lambda i, t: (i, 0, 0)),
                  pl.BlockSpec((1, d), lambda i, t: (0, 0)),
                  pl.BlockSpec((3, tm, LANES), lambda i, t: (0, t, 0)),
                  pl.BlockSpec((d, D_IN), lambda i, t: (0, 0))],
        out_specs=[row(CONV_END), row(2 * D_SGU), row(D_ATTN), row(KV_COLS)],
        out_shape=[jax.ShapeDtypeStruct((b, r, CONV_END), F32),
                   jax.ShapeDtypeStruct((b, r, 2 * D_SGU), F32),
                   jax.ShapeDtypeStruct((b, r, D_ATTN), BF16),
                   jax.ShapeDtypeStruct((b, r, KV_COLS), BF16)],
        compiler_params=_cparams(2),
        name="in_proj",
    )(x, mod, norm_g, rope, w_in)


def _fold_lanes(blocks, op):
    parts = [b[:, c:c + LANES] for b in blocks for c in range(0, b.shape[1], LANES)]
    out = parts[0]
    for p in parts[1:]:
        out = op(out, p)
    return out


def _mix_kernel(x_ref, pc_ref, pcp_ref, pcn_ref, ps_ref, q_ref, kv_ref, kvp_ref, kvn_ref, kvc_ref,
                mod_ref, convw_ref, sgug_ref, sguw_ref, sgub_ref, sink_ref, mixg_ref, wout_ref, n2g_ref,
                xo_ref, h2_ref, kvx_ref, attn_ref, sgu_ref, *, local):
    tq = x_ref.shape[0]
    nblk = tq // BLOCK
    t = pl.program_id(1)
    nt = pl.num_programs(1)

    pc = pc_ref[...]
    z = pc[:, D_CONV:2 * D_CONV] * pc[:, 2 * D_CONV:]
    last = SUBLANES - 1
    z_before = pcp_ref[last:last + 1, D_CONV:2 * D_CONV] * pcp_ref[last:last + 1, 2 * D_CONV:]
    z_after = pcn_ref[0:1, D_CONV:2 * D_CONV] * pcn_ref[0:1, 2 * D_CONV:]
    z_before = jnp.where(t > 0, z_before, 0.0)
    z_after = jnp.where(t < nt - 1, z_after, 0.0)
    rows = lax.broadcasted_iota(jnp.int32, (tq, D_CONV), 0)
    z_prev = jnp.where(rows == 0, z_before, pltpu.roll(z, 1, axis=0))
    z_next = jnp.where(rows == tq - 1, z_after, pltpu.roll(z, tq - 1, axis=0))
    conv = pc[:, :D_CONV] * (z_prev * convw_ref[0:1, :] + z * convw_ref[1:2, :] + z_next * convw_ref[2:3, :])

    zg = _gelu_sigmoid_form(ps_ref[...])
    u = zg[:, :D_SGU]
    v = zg[:, D_SGU:]
    mu = jnp.mean(v, axis=-1, keepdims=True)
    vc = v - mu
    vn = vc * lax.rsqrt(jnp.mean(vc * vc, axis=-1, keepdims=True) + EPS) * sgug_ref[...]
    lane = lax.broadcasted_iota(jnp.int32, (BLOCK, LANES), 1)
    low = lane < HEAD_DIM
    for cb in range(nblk):
        pieces = []
        for a in range(SGU_HEADS // 2):
            vp = vn[cb * BLOCK:(cb + 1) * BLOCK, a * LANES:(a + 1) * LANES]
            v_lo = jnp.where(low, vp, 0.0).astype(BF16)
            v_hi = jnp.where(low, 0.0, vp).astype(BF16)
            pieces.append(_dot(sguw_ref[2 * a], v_lo) + _dot(sguw_ref[2 * a + 1], v_hi))
        s = jnp.concatenate(pieces, axis=1) + sgub_ref[...]
        sgu_ref[cb * BLOCK:(cb + 1) * BLOCK, :] = u[cb * BLOCK:(cb + 1) * BLOCK, :] * s

    if local:
        kvx_ref[0:BLOCK, :] = kvp_ref[...]
        kvx_ref[BLOCK:BLOCK + tq, :] = kv_ref[...]
        kvx_ref[BLOCK + tq:, :] = kvn_ref[...]
    kvc = kvc_ref[...]
    qi = lax.broadcasted_iota(jnp.int32, (2 * BLOCK, BLOCK), 0) & (BLOCK - 1)
    kj = lax.broadcasted_iota(jnp.int32, (2 * BLOCK, BLOCK), 1)
    upper = lax.broadcasted_iota(jnp.int32, (2 * BLOCK, 1), 0) < BLOCK

    def attend(jb, carry):
        r0 = pl.multiple_of(jb * BLOCK, BLOCK)
        if local:
            n = t * nblk + jb
            ok_prev = (kj >= qi) & (kj >= jnp.where(n > 0, 0, BLOCK))
            ok_next = (kj <= qi) & (kj < jnp.where(n < nt * nblk - 1, BLOCK, 0))
            kvl = kvx_ref[pl.ds(r0, 3 * BLOCK), :]

            def band(s):
                return jnp.concatenate([jnp.where(ok_prev, s[:, :BLOCK], -jnp.inf), s[:, BLOCK:2 * BLOCK],
                                        jnp.where(ok_next, s[:, 2 * BLOCK:], -jnp.inf)], axis=1)
        pairs = [q_ref[pl.ds(r0, BLOCK), a * LANES:(a + 1) * LANES] for a in range(N_HEADS // 2)]
        groups = []
        for kvh in range(N_KV_HEADS):
            for even in (True, False):
                members = [2 * kvh, 2 * kvh + 1]
                keep = low if even else jnp.logical_not(low)
                qg = jnp.concatenate([jnp.where(keep, pairs[a], jnp.zeros_like(pairs[a])) for a in members], axis=0)
                natural = (kvh == 0) == even
                ksel = KV_K if natural else KV_K_SWAPPED
                vsel = KV_V if natural else KV_V_SWAPPED
                s_list = [_dot_nt(qg, kvc[:, ksel])]
                v_list = [kvc[:, vsel]]
                if local:
                    s_list.append(band(_dot_nt(qg, kvl[:, ksel])))
                    v_list.append(kvl[:, vsel])
                hds = [2 * a + (0 if even else 1) for a in members]
                sink = jnp.where(upper, sink_ref[hds[0]:hds[0] + 1, 0:1], sink_ref[hds[1]:hds[1] + 1, 0:1])
                groups.append((hds, s_list, v_list, sink))
        probs = []
        for _, s_list, _, sink in groups:
            m = jnp.maximum(sink, jnp.max(_fold_lanes(s_list, jnp.maximum), axis=-1, keepdims=True))
            ps = [jnp.exp2(s - m) for s in s_list]
            denom = jnp.exp2(sink - m) + jnp.sum(_fold_lanes(ps, jnp.add), axis=-1, keepdims=True)
            probs.append(([p.astype(BF16) for p in ps], 1.0 / denom))
        outs = [None] * N_HEADS
        for (ps, rden), (hds, _, v_list, _) in zip(probs, groups):
            o = _dot(ps[0], v_list[0])
            for p, vv in zip(ps[1:], v_list[1:]):
                o = o + _dot(p, vv)
            o = o * rden
            outs[hds[0]] = o[:BLOCK]
            outs[hds[1]] = o[BLOCK:]
        for a in range(N_HEADS // 2):
            attn_ref[pl.ds(r0, BLOCK), a * LANES:(a + 1) * LANES] = jnp.where(low, outs[2 * a], outs[2 * a + 1])
        return carry

    lax.fori_loop(0, nblk, attend, 0)

    g = mixg_ref[...]
    yc = (_rms(conv) * g[:, :D_CONV]).astype(BF16)
    ys = (_rms(sgu_ref[...]) * g[:, D_CONV:D_CONV + D_SGU]).astype(BF16)
    ya = (_rms(attn_ref[...]) * g[:, D_CONV + D_SGU:]).astype(BF16)
    yl = (_dot(yc, wout_ref[0:D_CONV, :]) + _dot(ys, wout_ref[D_CONV:D_CONV + D_SGU, :])
          + _dot(ya, wout_ref[D_CONV + D_SGU:, :]))
    xn = x_ref[...] + mod_ref[2:3, :] * yl
    xo_ref[...] = xn
    h2 = _rms(xn) * n2g_ref[...] * (1.0 + mod_ref[4:5, :]) + mod_ref[3:4, :]
    h2_ref[...] = h2.astype(BF16)


def _mix_call(x, pc, ps, q, kv, kvc, mod, conv_w, sgu_g, sgu_w, sgu_b, sink_b, mix_g, w_out, n2g, *, local):
    b, r, d = x.shape
    c = kvc.shape[1]
    tq = min(ROW_TILE, r)
    nt = r // tq
    hb = tq // SUBLANES
    kb = tq // BLOCK
    row = lambda n: pl.BlockSpec((None, tq, n), lambda i, t: (i, t, 0))
    full = lambda shape: pl.BlockSpec(shape, lambda i, t: (0,) * len(shape))
    return pl.pallas_call(
        functools.partial(_mix_kernel, local=local),
        grid=(b, nt),
        in_specs=[row(d), row(CONV_END),
                  pl.BlockSpec((None, SUBLANES, CONV_END), lambda i, t: (i, jnp.maximum(t * hb - 1, 0), 0)),
                  pl.BlockSpec((None, SUBLANES, CONV_END), lambda i, t: (i, jnp.minimum((t + 1) * hb, nt * hb - 1), 0)),
                  row(2 * D_SGU), row(D_ATTN), row(KV_COLS),
                  pl.BlockSpec((None, BLOCK, KV_COLS), lambda i, t: (i, jnp.maximum(t * kb - 1, 0), 0)),
                  pl.BlockSpec((None, BLOCK, KV_COLS), lambda i, t: (i, jnp.minimum((t + 1) * kb, nt * kb - 1), 0)),
                  pl.BlockSpec((None, c, KV_COLS), lambda i, t: (i, 0, 0)),
                  pl.BlockSpec((None, 6, d), lambda i, t: (i, 0, 0)),
                  full((3, D_CONV)), full((1, D_SGU)), full((SGU_HEADS, SGU_CHUNK, SGU_CHUNK)),
                  full((SGU_CHUNK, D_SGU)), full((N_HEADS, LANES)), full((1, d)), full((d, d)), full((1, d))],
        out_specs=[row(d), row(d)],
        out_shape=[jax.ShapeDtypeStruct((b, r, d), F32), jax.ShapeDtypeStruct((b, r, d), BF16)],
        scratch_shapes=[pltpu.VMEM((tq + 2 * BLOCK, KV_COLS), BF16),
                        pltpu.VMEM((tq, D_ATTN), F32),
                        pltpu.VMEM((tq, D_SGU), F32)],
        compiler_params=_cparams(2),
        name="mixers_local" if local else "mixers_ctx",
    )(x, pc, pc, pc, ps, q, kv, kv, kv, kvc, mod, conv_w, sgu_g, sgu_w, sgu_b, sink_b, mix_g, w_out, n2g)


def _oddeven_merge(lo, hi, r):
    step = r * 2
    if step < hi - lo:
        yield from _oddeven_merge(lo, hi, step)
        yield from _oddeven_merge(lo + r, hi, step)
        yield from [(i, i + r) for i in range(lo + r, hi - r, step)]
    else:
        yield (lo, lo + r)


def _oddeven_sort(lo, hi):
    if hi - lo >= 1:
        mid = lo + (hi - lo) // 2
        yield from _oddeven_sort(lo, mid)
        yield from _oddeven_sort(mid + 1, hi)
        yield from _oddeven_merge(lo, hi, 1)


_SORT16 = tuple(_oddeven_sort(0, PEER_TOPK - 1))


def _sort16(x):
    x = list(x)
    for i, j in _SORT16:
        x[i], x[j] = jnp.maximum(x[i], x[j]), jnp.minimum(x[i], x[j])
    return x


def _merge_top16(a, b):
    n = PEER_TOPK
    c = [jnp.maximum(a[i], b[n - 1 - i]) for i in range(n)]
    d = n // 2
    while d:
        for i in range(n):
            if not i & d:
                c[i], c[i + d] = jnp.maximum(c[i], c[i + d]), jnp.minimum(c[i], c[i + d])
        d //= 2
    return c


def _top16(load):
    def tree(lo, n):
        if n == PEER_TOPK:
            return _sort16([load(lo + i) for i in range(n)])
        return _merge_top16(tree(lo, n // 2), tree(lo + n // 2, n // 2))
    return tree(0, N_KEYS)


def _peer_select(h_ref, ws_ref, u0_ref, ht_ref, e0_ref, e1_ref, th_ref, s0_ref, s1_ref, a0_ref):
    tm = h_ref.shape[0]
    nh = PEER_HEADS
    ht_ref[...] = h_ref[...].astype(F32).T.astype(BF16)
    s0 = _dot(_unpack_rows(ws_ref[0]), ht_ref[...])
    s1 = _dot(_unpack_rows(ws_ref[1]), ht_ref[...])
    for c in range(tm // LANES):
        s0_ref[c] = s0[:, c * LANES:(c + 1) * LANES]
        s1_ref[c] = s1[:, c * LANES:(c + 1) * LANES]

    rnd = lambda v: v.astype(BF16).astype(F32)

    def select(c, carry):
        lanes = pl.ds(pl.multiple_of(c * LANES, LANES), LANES)
        a0_ref[:, lanes] = _dot(_unpack_rows(u0_ref[...]), ht_ref[:, lanes]).astype(BF16)
        a = _top16(lambda i: s0_ref[c, i * nh:(i + 1) * nh, :])
        b = _top16(lambda j: s1_ref[c, j * nh:(j + 1) * nh, :])
        ea = [jnp.exp2(v - a[0]) for v in a]
        eb = [jnp.exp2(v - b[0]) for v in b]
        cand = [ea[k] * eb[l] for k, l in _CAND]
        rest = cand[PEER_TOPK:]
        rest = rest + [jnp.full_like(cand[0], -1.0)] * (-len(rest) % PEER_TOPK)
        best = cand[:PEER_TOPK]
        for g in range(0, len(rest), PEER_TOPK):
            best = _merge_top16(best, _sort16(rest[g:g + PEER_TOPK]))
        top = best[PEER_TOPK - 1]
        zsum = jnp.zeros_like(top)
        for p in cand:
            zsum = zsum + jnp.where(p >= top, p, 0.0)
        rz = 1.0 / zsum
        ean = [rnd(v * rz) for v in ea]
        ebn = [rnd(v) for v in eb]
        thn = jnp.full_like(top, jnp.inf)
        for (k, l), p in zip(_CAND, cand):
            thn = jnp.minimum(thn, jnp.where(p >= top, rnd(ean[k] * ebn[l]), jnp.inf))
        for hh in range(nh):
            th_ref[hh, :, lanes] = jnp.broadcast_to(thn[hh:hh + 1, :], (2 * SUBLANES, LANES)).astype(BF16)
        for i in range(N_KEYS):
            e0_ref[i, :, lanes] = jnp.exp2(s0_ref[c, i * nh:(i + 1) * nh, :] - a[0]) * rz
        for hh in range(nh):
            s1_head = s1_ref[c, pl.ds(hh, N_KEYS, stride=nh), :]
            e1_ref[hh, :, lanes] = jnp.exp2(s1_head - b[0][hh:hh + 1, :]).astype(BF16)
        return carry

    lax.fori_loop(0, tm // LANES, select, 0)


def _peer_gate(a_ref, row0, e0_ref, e1_ref, th_ref, hbuf_ref, lanes):
    n_lanes = lanes.stop - lanes.start
    pack = 2 * SUBLANES
    for ii in range(a_ref.shape[0] // N_KEYS):
        e0 = e0_ref[row0 + ii, :, lanes]
        e0r = [jnp.broadcast_to(e0[hh:hh + 1, :], (pack, n_lanes)).astype(BF16) for hh in range(PEER_HEADS)]
        for c in range(N_KEYS // pack):
            gate = None
            for hh in range(PEER_HEADS):
                p = e0r[hh] * e1_ref[hh, c * pack:(c + 1) * pack, lanes]
                sel = jnp.where(p >= th_ref[hh, :, lanes], p, jnp.zeros_like(p))
                gate = sel if gate is None else gate + sel
            r0 = ii * N_KEYS + c * pack
            act = _gelu_sigmoid_form(a_ref[r0:r0 + pack, lanes])
            hbuf_ref[r0:r0 + pack, lanes] = act * gate


def _peer_lane_split(tm):
    return 2 if tm % (2 * 2 * LANES) == 0 else 1


def _peer_kernel(h_ref, x_ref, mod_ref, fg_ref, ws_ref, u0_ref, uodd_ref, uevn_ref, vt_ref, o_ref,
                 ht_ref, e0_ref, e1_ref, th_ref, *scratch, final_norm):
    e = pl.program_id(2)
    ne = pl.num_programs(2)
    tm = h_ref.shape[0]
    n_split = _peer_lane_split(tm)
    s0_ref, s1_ref, aevn_ref, aodd_ref, hevn_ref, hodd_ref, acc_ref = scratch
    eb = 2 * u0_ref.shape[0]
    keys_per_block = eb // N_KEYS

    lane_ranges = [slice(s * (tm // n_split), (s + 1) * (tm // n_split)) for s in range(n_split)]
    stages = ((uodd_ref, aodd_ref, aevn_ref, hevn_ref), (uevn_ref, aevn_ref, aodd_ref, hodd_ref))
    chains = [(s, lanes) for s in range(len(stages)) for lanes in lane_ranges]

    @pl.when(e == 0)
    def _():
        _peer_select(h_ref, ws_ref, u0_ref, ht_ref, e0_ref, e1_ref, th_ref, s0_ref, s1_ref, aevn_ref)
        acc_ref[...] = jnp.zeros_like(acc_ref)

    def next_scores(s, lanes):
        u_next, a_next, _, _ = stages[s]
        a_next[:, lanes] = _dot(_unpack_rows(u_next[...]), ht_ref[:, lanes]).astype(BF16)

    def gates(s, lanes):
        _, _, a_cur, hbuf_ref = stages[s]
        _peer_gate(a_cur, (2 * e + s) * keys_per_block, e0_ref, e1_ref, th_ref, hbuf_ref, lanes)

    def values(s, lanes):
        hbuf_ref = stages[s][3]
        vt = _unpack_rows(vt_ref[:, s * eb:(s + 1) * eb])
        acc_ref[:, lanes] += _dot(vt, hbuf_ref[:, lanes])

    next_scores(*chains[0])
    for c, chain in enumerate(chains):
        if c + 1 < len(chains):
            next_scores(*chains[c + 1])
        gates(*chain)
        values(*chain)

    @pl.when(e == ne - 1)
    def _():
        y = x_ref[...] + mod_ref[5:6, :] * acc_ref[...].T
        if final_norm:
            y = _rms(y) * fg_ref[...]
        o_ref[...] = y


def _tables_kernel(u_ref, v_ref, up_ref, vtp_ref):
    up_ref[...] = _pack_rows_in_kernel(u_ref[...].astype(BF16))
    vtp_ref[...] = _pack_rows_in_kernel(v_ref[...].T.astype(BF16))


def _tables_call(peer_u, peer_v):
    depth, n_exp, d = peer_u.shape
    eb = PEER_EXPERTS
    return pl.pallas_call(
        _tables_kernel,
        grid=(depth, n_exp // eb),
        in_specs=[pl.BlockSpec((None, eb, d), lambda i, e: (i, e, 0)),
                  pl.BlockSpec((None, eb, d), lambda i, e: (i, e, 0))],
        out_specs=[pl.BlockSpec((None, eb // 2, d), lambda i, e: (i, e, 0)),
                   pl.BlockSpec((None, d // 2, eb), lambda i, e: (i, 0, e))],
        out_shape=[jax.ShapeDtypeStruct((depth, n_exp // 2, d), jnp.int32),
                   jax.ShapeDtypeStruct((depth, d // 2, n_exp), jnp.int32)],
        compiler_params=_cparams(2),
        name="pack_tables",
    )(peer_u, peer_v)


def _fold_kernel(k_ref, w_ref, o_ref):
    k = k_ref[...]
    w = w_ref[...]
    k_hi = k.astype(BF16)
    k_lo = (k - k_hi.astype(F32)).astype(BF16)
    w_hi = w.astype(BF16)
    w_lo = (w - w_hi.astype(F32)).astype(BF16)
    w_s = (_dot(k_hi, w_hi) + _dot(k_lo, w_hi) + _dot(k_hi, w_lo)) * math.log2(math.e)
    o_ref[...] = _pack_rows_in_kernel(w_s.astype(BF16))


def _fold_call(keys_p, wq_t):
    _, m, k = keys_p.shape
    n = wq_t.shape[2]
    tn = FOLD_COLS
    return pl.pallas_call(
        _fold_kernel,
        grid=(2, n // tn),
        in_specs=[pl.BlockSpec((None, m, k), lambda p, j: (p, 0, 0)),
                  pl.BlockSpec((None, k, tn), lambda p, j: (p, 0, j))],
        out_specs=pl.BlockSpec((None, m // 2, tn), lambda p, j: (p, 0, j)),
        out_shape=jax.ShapeDtypeStruct((2, m // 2, n), jnp.int32),
        compiler_params=_cparams(2),
        name="fold_keys",
    )(keys_p, wq_t)


def _peer_call(h2, x, mod, final_g, w_s, u_b, vt_b, *, layer, final_norm):
    b, r, d = x.shape
    tm = min(PEER_TOKENS, r)
    eb = PEER_EXPERTS
    n_blocks = 2 * u_b.shape[1] // eb
    nh = PEER_HEADS
    score_chunks = pltpu.VMEM((tm // LANES, N_KEYS * nh, LANES), F32)
    row = lambda: pl.BlockSpec((None, tm, d), lambda i, t, e: (i, t, 0))
    full = lambda shape: pl.BlockSpec(shape, lambda i, t, e: (0,) * len(shape))
    return pl.pallas_call(
        functools.partial(_peer_kernel, final_norm=final_norm),
        grid=(b, r // tm, n_blocks // 2),
        in_specs=[row(), row(),
                  pl.BlockSpec((None, 6, d), lambda i, t, e: (i, 0, 0)),
                  full((1, d)), full(w_s.shape),
                  pl.BlockSpec((None, eb // 2, d), lambda i, t, e: (layer, 0, 0)),
                  pl.BlockSpec((None, eb // 2, d), lambda i, t, e: (layer, 2 * e + 1, 0)),
                  pl.BlockSpec((None, eb // 2, d),
                               lambda i, t, e: (layer, jnp.minimum(2 * e + 2, n_blocks - 2), 0)),
                  pl.BlockSpec((None, d // 2, 2 * eb), lambda i, t, e: (layer, 0, e))],
        out_specs=row(),
        out_shape=jax.ShapeDtypeStruct((b, r, d), F32),
        scratch_shapes=[pltpu.VMEM((d, tm), BF16),
                        pltpu.VMEM((N_KEYS, nh, tm), F32),
                        pltpu.VMEM((nh, N_KEYS, tm), BF16),
                        pltpu.VMEM((nh, 2 * SUBLANES, tm), BF16),
                        score_chunks,
                        score_chunks,
                        pltpu.VMEM((eb, tm), BF16),
                        pltpu.VMEM((eb, tm), BF16),
                        pltpu.VMEM((eb, tm), BF16),
                        pltpu.VMEM((eb, tm), BF16),
                        pltpu.VMEM((d, tm), F32)],
        compiler_params=_cparams(3),
        name="peer_final" if final_norm else "peer",
    )(h2, x, mod, final_g, w_s, u_b, u_b, u_b, vt_b)


def _rope_tables(length):
    rows = length // GRID_W
    row = jnp.repeat(jnp.arange(rows), GRID_W).astype(F32)
    col = jnp.tile(jnp.arange(GRID_W), rows).astype(F32)
    inv = ROPE_THETA ** (-jnp.arange(ROPE_FREQS, dtype=F32) / ROPE_FREQS)
    ar = row[:, None] * inv[None, :]
    ac = col[:, None] * inv[None, :]
    ang = jnp.concatenate([ar, ar, ac, ac, ar, ar, ac, ac], axis=-1)
    first = (jnp.arange(2 * HEAD_DIM) // ROPE_FREQS) % 2 == 0
    sin = jnp.sin(ang)
    return jnp.stack([jnp.cos(ang), jnp.where(first, -sin, 0.0), jnp.where(first, 0.0, sin)])


def kernel(x, c, ctx, c_ctx, w_ada, b_ada, norm1_g, norm2_g, w_in, conv_w, sgu_norm_g, sgu_w, sgu_b,
           attn_sink, mix_norm_g, w_out, peer_wq, peer_keys, peer_u, peer_v, final_g):
    bsz, length, d = x.shape
    n_ctx = ctx.shape[1]
    depth = w_ada.shape[0]
    nh = PEER_HEADS
    n_exp = peer_u.shape[1]
    assert w_in.shape[1:] == (d, D_IN) and w_out.shape[1:] == (d, d) and w_ada.shape[2] == 6 * d
    assert peer_wq.shape[1:] == (d, 2 * nh * PEER_DHALF) and peer_keys.shape[1:] == (nh, 2, N_KEYS, PEER_DHALF)
    assert n_exp == N_KEYS * N_KEYS and n_exp % (2 * PEER_EXPERTS) == 0 and PEER_EXPERTS % (SUBLANES * N_KEYS) == 0
    assert bsz + 1 <= SUBLANES and (6 * d) % MOD_COLS == 0 and d % FOLD_COLS == 0
    assert length % GRID_W == 0 and length % min(ROW_TILE, length) == 0 and min(ROW_TILE, length) % BLOCK == 0
    assert length % min(PEER_TOKENS, length) == 0 and (bsz * n_ctx) % min(PEER_TOKENS, bsz * n_ctx) == 0
    assert n_ctx % BLOCK == 0 and n_ctx % min(ROW_TILE, n_ctx) == 0
    assert length % min(IN_TILE, length) == 0 and n_ctx % min(IN_TILE, n_ctx) == 0

    cc = jnp.zeros((SUBLANES, d), F32).at[:bsz].set(c).at[bsz].set(c_ctx)
    mod = _mod_call(cc, w_ada, b_ada)

    rope_l = _rope_tables(length)
    rope_c = jnp.zeros((3, n_ctx, 2 * HEAD_DIM), F32).at[0].set(1.0)
    fg = final_g.reshape(1, d)
    u_packed, vt_packed = _tables_call(peer_u, peer_v)

    xl, xc = x, ctx
    for i in range(depth):
        last = i == depth - 1
        mod_l = mod[i, :bsz].reshape(bsz, 6, d)
        mod_c = jnp.broadcast_to(mod[i, bsz].reshape(1, 6, d), (bsz, 6, d))
        n1g = norm1_g[i].reshape(1, d)
        n2g = norm2_g[i].reshape(1, d)
        w_in_b = w_in[i].astype(BF16)
        sgu_g = sgu_norm_g[i].reshape(1, D_SGU)
        sgu_wb = sgu_w[i].astype(BF16)
        sgu_bias = jnp.repeat(sgu_b[i].T, D_SGU // SGU_HEADS, axis=1)
        sink_b = jnp.broadcast_to(attn_sink[i][:, None] * math.log2(math.e), (N_HEADS, LANES))
        mix_g = mix_norm_g[i].reshape(1, d)
        w_out_b = w_out[i].astype(BF16)
        wq_t = peer_wq[i].reshape(d, nh, 2, PEER_DHALF).transpose(2, 1, 3, 0).reshape(2 * nh * PEER_DHALF, d)
        keys_p = jnp.einsum('hpid,hg->pihgd', peer_keys[i], jnp.eye(nh, dtype=F32))
        keys_p = keys_p.reshape(2, N_KEYS * nh, nh * PEER_DHALF)
        w_s = _fold_call(keys_p, wq_t.reshape(2, nh * PEER_DHALF, d))
        peer = functools.partial(_peer_call, w_s=w_s, u_b=u_packed, vt_b=vt_packed, layer=i)
        mixer_w = (conv_w[i], sgu_g, sgu_wb, sgu_bias, sink_b, mix_g, w_out_b, n2g)

        pc_c, ps_c, q_c, kv_c = _in_call(xc, mod_c, n1g, rope_c, w_in_b)
        pc_l, ps_l, q_l, kv_l = _in_call(xl, mod_l, n1g, rope_l, w_in_b)
        xl, h2_l = _mix_call(xl, pc_l, ps_l, q_l, kv_l, kv_c, mod_l, *mixer_w, local=True)
        if not last:
            xc, h2_c = _mix_call(xc, pc_c, ps_c, q_c, kv_c, kv_c, mod_c, *mixer_w, local=False)
            rows = min(PEER_TOKENS, bsz * n_ctx)
            merged = (bsz * n_ctx // rows, rows, d)
            xc = peer(h2_c.reshape(merged), xc.reshape(merged), mod_c[:merged[0]], fg, final_norm=False)
            xc = xc.reshape(bsz, n_ctx, d)
        xl = peer(h2_l, xl, mod_l, fg, final_norm=last)
    return xl
```
